```python
import jax, jax.numpy as jnp
from jax import lax
import numpy as np

D_MODEL = 2048
BATCH = 1
SEQ = 8192
DEPTH = 1
DEC_BATCH = 128
DEC_SEQ = 4
PAST_LEN = 16384
PAGE_SIZE = 128

D_MIX = D_MODEL
HEAD_DIM = 64
D_ATTN = D_MIX // 2
N_HEADS = D_ATTN // HEAD_DIM
N_KV_HEADS = N_HEADS // 8
GQA_GROUP = N_HEADS // N_KV_HEADS
KV_DIM = N_KV_HEADS * HEAD_DIM
WINDOW = 128
BLOCK = 128
D_CONV = D_MIX - D_ATTN
CONV_W = 31
D_IN = D_ATTN + 2 * KV_DIM + 2 * D_CONV
N_GROUPS = 4
EXPERTS_PER_GROUP = 8
N_EXPERTS = N_GROUPS * EXPERTS_PER_GROUP
TOP_K_INNER = 2
D_EXPERT = 256
RMS_EPS = 1e-6
LN_EPS = 1e-5

kernel_name = 'hymba_swa_sink_conformer_hmoe_step'


def rmsnorm(x, g):
    xf = x.astype(jnp.float32)
    y = xf * lax.rsqrt(jnp.mean(xf * xf, axis=-1, keepdims=True) + RMS_EPS)
    return (y * g.astype(jnp.float32)).astype(x.dtype)


def layernorm(x, g, b):
    xf = x.astype(jnp.float32)
    mu = jnp.mean(xf, axis=-1, keepdims=True)
    xc = xf - mu
    y = xc * lax.rsqrt(jnp.mean(xc * xc, axis=-1, keepdims=True) + LN_EPS)
    return (y * g.astype(jnp.float32) + b.astype(jnp.float32)).astype(x.dtype)


def split_in_proj(z):
    lead = z.shape[:-1]
    q = z[..., :D_ATTN].reshape(lead + (N_KV_HEADS, GQA_GROUP, HEAD_DIM))
    k = z[..., D_ATTN:D_ATTN + KV_DIM].reshape(lead + (N_KV_HEADS, HEAD_DIM))
    v = z[..., D_ATTN + KV_DIM:D_ATTN + 2 * KV_DIM].reshape(lead + (N_KV_HEADS, HEAD_DIM))
    c = z[..., D_ATTN + 2 * KV_DIM:]
    u = c[..., :D_CONV] * jax.nn.sigmoid(c[..., D_CONV:])
    return q, k, v, u


def sink_attention(q, k, v, mask, sink):
    s = jnp.einsum('...qkgd,...skd->...kgqs', q, k).astype(jnp.float32) * (HEAD_DIM ** -0.5)
    s = jnp.where(mask, s, -jnp.inf)
    sk = sink.astype(jnp.float32)[:, :, None, None]
    m = jnp.maximum(jnp.max(s, axis=-1, keepdims=True), sk)
    p = jnp.exp(s - m)
    denom = jnp.sum(p, axis=-1, keepdims=True) + jnp.exp(sk - m)
    p = (p / denom).astype(v.dtype)
    o = jnp.einsum('...kgqs,...skd->...qkgd', p, v)
    return o.reshape(o.shape[:-3] + (D_ATTN,))


def swa_prompt(q, k, v, sink):
    B, S = q.shape[0], q.shape[1]
    nb = S // BLOCK
    qb = q.reshape(B, nb, BLOCK, N_KV_HEADS, GQA_GROUP, HEAD_DIM)
    kb = k.reshape(B, nb, BLOCK, N_KV_HEADS, HEAD_DIM)
    vb = v.reshape(B, nb, BLOCK, N_KV_HEADS, HEAD_DIM)
    pad = ((0, 0), (1, 0), (0, 0), (0, 0), (0, 0))
    kk = jnp.concatenate([jnp.pad(kb, pad)[:, :-1], kb], axis=2)
    vv = jnp.concatenate([jnp.pad(vb, pad)[:, :-1], vb], axis=2)
    qi = jnp.arange(BLOCK)[:, None]
    kj = jnp.arange(2 * BLOCK)[None, :]
    diff = qi + BLOCK - kj
    band = (diff >= 0) & (diff <= WINDOW)
    valid = (jnp.arange(nb)[:, None, None] > 0) | (kj[None] >= BLOCK)
    mask = (band[None] & valid)[None, :, None, None]
    o = sink_attention(qb, kk, vv, mask, sink)
    return o.reshape(B, S, D_ATTN)


def swa_sample(q, kk, vv, sink):
    T = q.shape[1]
    qi = jnp.arange(T)[:, None]
    kj = jnp.arange(WINDOW + T)[None, :]
    diff = qi + WINDOW - kj
    mask = ((diff >= 0) & (diff <= WINDOW))[None, None, None]
    return sink_attention(q, kk, vv, mask, sink)


def conformer_conv(u_ext, dw_w, dw_b, ln_g, ln_b, pw_w, pw_b):
    y = lax.conv_general_dilated(
        u_ext, dw_w[:, None, :].astype(u_ext.dtype), window_strides=(1,), padding='VALID',
        dimension_numbers=('NWC', 'WIO', 'NWC'), feature_group_count=D_CONV)
    y = layernorm(y + dw_b, ln_g, ln_b)
    return jax.nn.silu(y) @ pw_w + pw_b


def hier_moe(x, rg_w, rg_b, re_w, re_b, e_gate, e_up, e_down):
    g_prob = jax.nn.softmax((x @ rg_w + rg_b).astype(jnp.float32), axis=-1)
    g_val, g_idx = lax.top_k(g_prob, 1)
    e_logits = (x @ re_w + re_b).astype(jnp.float32).reshape(-1, N_GROUPS, EXPERTS_PER_GROUP)
    e_sel = jnp.take_along_axis(e_logits, g_idx[:, :, None], axis=1)[:, 0]
    top_v, top_i = lax.top_k(jax.nn.softmax(e_sel, axis=-1), TOP_K_INNER)
    top_v = top_v / jnp.sum(top_v, axis=-1, keepdims=True)
    inner = jnp.sum(jax.nn.one_hot(top_i, EXPERTS_PER_GROUP, dtype=jnp.float32) * top_v[..., None], axis=1)
    gates = (jax.nn.one_hot(g_idx[:, 0], N_GROUPS, dtype=jnp.float32)[:, :, None]
             * (g_val[:, :, None] * inner[:, None, :])).reshape(-1, N_EXPERTS).astype(x.dtype)
    hg = jnp.einsum('nd,edf->nef', x, e_gate)
    hu = jnp.einsum('nd,edf->nef', x, e_up)
    hidden = jax.nn.silu(hg) * hu * gates[:, :, None]
    return jnp.einsum('nef,efd->nd', hidden, e_down)


def decoder_layer(h, win_k, win_v, conv_buf, norm1_g, w_in, attn_sink, dw_w, dw_b, ln_g, ln_b,
                  pw_w, pw_b, w_out, norm2_g, rg_w, rg_b, re_w, re_b, e_gate, e_up, e_down):
    B, T, _ = h.shape
    q, k, v, u = split_in_proj(rmsnorm(h, norm1_g) @ w_in)
    sink = attn_sink.reshape(N_KV_HEADS, GQA_GROUP)
    if win_k is None:
        attn = swa_prompt(q, k, v, sink)
        k_all, v_all = k, v
        u_ext = jnp.pad(u, ((0, 0), (CONV_W - 1, 0), (0, 0)))
    else:
        k_all = jnp.concatenate([win_k, k], axis=1)
        v_all = jnp.concatenate([win_v, v], axis=1)
        attn = swa_sample(q, k_all, v_all, sink)
        u_ext = jnp.concatenate([conv_buf, u], axis=1)
    conv = conformer_conv(u_ext, dw_w, dw_b, ln_g, ln_b, pw_w, pw_b)
    h = h + jnp.concatenate([attn, conv], axis=-1) @ w_out
    moe_out = hier_moe(rmsnorm(h, norm2_g).reshape(B * T, D_MODEL), rg_w, rg_b, re_w, re_b,
                       e_gate, e_up, e_down)
    h = h + moe_out.reshape(B, T, D_MODEL)
    return h, k_all[:, -WINDOW:], v_all[:, -WINDOW:], u_ext[:, -(CONV_W - 1):]


def setup_inputs(seed: int = 0) -> dict:
    key = jax.random.key(seed)
    ks = jax.random.split(key, 32)
    f32 = jnp.float32
    nrm = lambda k, shape, scale: jax.random.normal(k, shape, f32) * scale
    L = DEPTH
    return {
        'x_prompt': nrm(ks[0], (BATCH, SEQ, D_MODEL), 1.0),
        'x_sample': nrm(ks[1], (DEC_BATCH, DEC_SEQ, D_MODEL), 1.0),
        'cache_win_k': nrm(ks[2], (L, DEC_BATCH, WINDOW, N_KV_HEADS, HEAD_DIM), 1.0),
        'cache_win_v': nrm(ks[3], (L, DEC_BATCH, WINDOW, N_KV_HEADS, HEAD_DIM), 1.0),
        'state_conv': nrm(ks[4], (L, DEC_BATCH, CONV_W - 1, D_CONV), 0.5),
        'norm1_g': 1.0 + nrm(ks[5], (L, D_MODEL), 0.01),
        'w_in': nrm(ks[6], (L, D_MODEL, D_IN), D_MODEL ** -0.5),
        'attn_sink': nrm(ks[7], (L, N_HEADS), 0.5),
        'conv_dw_w': nrm(ks[8], (L, CONV_W, D_CONV), CONV_W ** -0.5),
        'conv_dw_b': nrm(ks[9], (L, D_CONV), 0.01),
        'conv_ln_g': 1.0 + nrm(ks[10], (L, D_CONV), 0.01),
        'conv_ln_b': nrm(ks[11], (L, D_CONV), 0.01),
        'conv_pw_w': nrm(ks[12], (L, D_CONV, D_CONV), D_CONV ** -0.5),
        'conv_pw_b': nrm(ks[13], (L, D_CONV), 0.01),
        'w_out': nrm(ks[14], (L, D_MIX, D_MODEL), D_MIX ** -0.5),
        'norm2_g': 1.0 + nrm(ks[15], (L, D_MODEL), 0.01),
        'router_group_w': nrm(ks[16], (L, D_MODEL, N_GROUPS), D_MODEL ** -0.5),
        'router_group_b': nrm(ks[17], (L, N_GROUPS), 0.01),
        'router_expert_w': nrm(ks[18], (L, D_MODEL, N_EXPERTS), D_MODEL ** -0.5),
        'router_expert_b': nrm(ks[19], (L, N_EXPERTS), 0.01),
        'expert_w_gate': nrm(ks[20], (L, N_EXPERTS, D_MODEL, D_EXPERT), D_MODEL ** -0.5),
        'expert_w_up': nrm(ks[21], (L, N_EXPERTS, D_MODEL, D_EXPERT), D_MODEL ** -0.5),
        'expert_w_down': nrm(ks[22], (L, N_EXPERTS, D_EXPERT, D_MODEL), D_EXPERT ** -0.5),
        'final_norm_g': 1.0 + nrm(ks[23], (D_MODEL,), 0.01),
    }


def reference(x_prompt, x_sample, cache_win_k, cache_win_v, state_conv, norm1_g, w_in, attn_sink,
              conv_dw_w, conv_dw_b, conv_ln_g, conv_ln_b, conv_pw_w, conv_pw_b, w_out, norm2_g,
              router_group_w, router_group_b, router_expert_w, router_expert_b,
              expert_w_gate, expert_w_up, expert_w_down, final_norm_g):
    hp, hs = x_prompt, x_sample
    kp_l, vp_l, cp_l, ks_l, vs_l, cs_l = [], [], [], [], [], []
    for l in range(DEPTH):
        w = (norm1_g[l], w_in[l], attn_sink[l], conv_dw_w[l], conv_dw_b[l], conv_ln_g[l], conv_ln_b[l],
             conv_pw_w[l], conv_pw_b[l], w_out[l], norm2_g[l], router_group_w[l], router_group_b[l],
             router_expert_w[l], router_expert_b[l], expert_w_gate[l], expert_w_up[l], expert_w_down[l])
        hp, kp, vp, cp = decoder_layer(hp, None, None, None, *w)
        hs, ksn, vsn, csn = decoder_layer(hs, cache_win_k[l], cache_win_v[l], state_conv[l], *w)
        kp_l.append(kp); vp_l.append(vp); cp_l.append(cp)
        ks_l.append(ksn); vs_l.append(vsn); cs_l.append(csn)
    y_prompt = rmsnorm(hp, final_norm_g)
    y_sample = rmsnorm(hs, final_norm_g)
    win_k_prompt = jnp.stack(kp_l, axis=0)
    win_v_prompt = jnp.stack(vp_l, axis=0)
    conv_prompt = jnp.stack(cp_l, axis=0)
    win_k_sample = jnp.stack(ks_l, axis=0)
    win_v_sample = jnp.stack(vs_l, axis=0)
    conv_sample = jnp.stack(cs_l, axis=0)
    return (y_prompt, y_sample, win_k_prompt, win_v_prompt, conv_prompt, win_k_sample, win_v_sample, conv_sample)
```

```python
import functools

import jax
import jax.numpy as jnp
from jax import lax
from jax.experimental import pallas as pl
from jax.experimental.pallas import tpu as pltpu

F32 = jnp.float32
BF16 = jnp.bfloat16

D_MODEL = 2048
HEAD_DIM = 64
N_HEADS = 16
N_KV_HEADS = 2
GQA_GROUP = 8
KV_DIM = N_KV_HEADS * HEAD_DIM
D_ATTN = N_HEADS * HEAD_DIM
D_CONV = D_MODEL - D_ATTN
WINDOW = 128
CONV_W = 31
D_IN = D_ATTN + 2 * KV_DIM + 2 * D_CONV
N_GROUPS = 4
EXPERTS_PER_GROUP = 8
N_EXPERTS = N_GROUPS * EXPERTS_PER_GROUP
D_EXPERT = 256
RMS_EPS = 1e-6
LN_EPS = 1e-5

LANES = 128
ROUTER_LANE0 = N_GROUPS
VMEM_LIMIT = 56 * 1024 * 1024

TM_PROJ = 512
TM_MIX = 256
SEQ_TILE = 8
TR = 256
TM_OUT = 512

_NT = (((1,), (1,)), ((), ()))


def _params(n_axes=1):
    return pltpu.CompilerParams(dimension_semantics=("arbitrary",) * n_axes, vmem_limit_bytes=VMEM_LIMIT)


def _resident(shape):
    return pl.BlockSpec(shape, lambda *_: (0,) * len(shape), pipeline_mode=pl.Buffered(1))


def _rmsnorm(x, g):
    ms = jnp.mean(x * x, axis=-1, keepdims=True)
    return x * lax.rsqrt(ms + RMS_EPS) * g


def _dot(a, b):
    return jnp.dot(a, b, preferred_element_type=F32)


def _in_proj_body(x_ref, g_ref, w_ref, q_ref, k_ref, v_ref, u_ref):
    xn = _rmsnorm(x_ref[...], g_ref[...]).astype(BF16)
    q_ref[...] = (_dot(xn, w_ref[:, :D_ATTN]) * (HEAD_DIM ** -0.5)).astype(BF16)
    kv = _dot(xn, w_ref[:, D_ATTN:D_ATTN + 2 * KV_DIM])
    k_ref[...] = kv[:, :KV_DIM]
    v_ref[...] = kv[:, KV_DIM:]
    c0 = D_ATTN + 2 * KV_DIM
    cw = 256
    for j in range(D_CONV // cw):
        a = _dot(xn, w_ref[:, c0 + j * cw:c0 + (j + 1) * cw])
        b = _dot(xn, w_ref[:, c0 + D_CONV + j * cw:c0 + D_CONV + (j + 1) * cw])
        u_ref[:, j * cw:(j + 1) * cw] = a * jax.nn.sigmoid(b)


def _in_proj(x, g, w_bf):
    n = x.shape[0]
    tm = min(TM_PROJ, n)
    row = lambda w: pl.BlockSpec((tm, w), lambda i: (i, 0))
    return pl.pallas_call(
        _in_proj_body,
        grid=(n // tm,),
        in_specs=[row(D_MODEL), _resident((1, D_MODEL)), _resident((D_MODEL, D_IN))],
        out_specs=[row(D_ATTN), row(KV_DIM), row(KV_DIM), row(D_CONV)],
        out_shape=[jax.ShapeDtypeStruct((n, D_ATTN), BF16), jax.ShapeDtypeStruct((n, KV_DIM), F32),
                   jax.ShapeDtypeStruct((n, KV_DIM), F32), jax.ShapeDtypeStruct((n, D_CONV), F32)],
        compiler_params=_params(),
        name="in_proj",
    )(x, g, w_bf)


def _attend(q, parts, sink_ref, kv_head):
    t = q.shape[0]
    qs = jnp.concatenate([q[:, g * HEAD_DIM:(g + 1) * HEAD_DIM] for g in range(GQA_GROUP)], axis=0)
    scores = [lax.dot_general(qs, kk, _NT, preferred_element_type=F32) for kk, _, _ in parts]
    ps = [[] for _ in parts]
    inv = []
    for g in range(GQA_GROUP):
        sg = [jnp.where(mask, s[g * t:(g + 1) * t], -jnp.inf) for s, (_, _, mask) in zip(scores, parts)]
        sk = sink_ref[kv_head * GQA_GROUP + g]
        m = sk
        for x in sg:
            m = jnp.maximum(jnp.max(x, axis=-1, keepdims=True), m)
        den = jnp.exp(sk - m)
        for k, x in enumerate(sg):
            p = jnp.exp(x - m)
            den = den + jnp.sum(p, axis=-1, keepdims=True)
            ps[k].append(p.astype(BF16))
        inv.append(1.0 / den)
    o = None
    for k, (_, vv, _) in enumerate(parts):
        ok = _dot(jnp.concatenate(ps[k], axis=0), vv)
        o = ok if o is None else o + ok
    return jnp.concatenate([o[g * t:(g + 1) * t] * inv[g] for g in range(GQA_GROUP)], axis=1)


def _conv_tail(y, dwb_ref, lng_ref, lnb_ref, pw_ref, pwb_ref):
    y = y + dwb_ref[...]
    mu = jnp.mean(y, axis=-1, keepdims=True)
    yc = y - mu
    yn = yc * lax.rsqrt(jnp.mean(yc * yc, axis=-1, keepdims=True) + LN_EPS)
    yn = yn * lng_ref[...] + lnb_ref[...]
    act = yn * jax.nn.sigmoid(yn)
    return _dot(act.astype(BF16), pw_ref[...]) + pwb_ref[...]


CONV_PAD = 32
CONV_ROWS = 32
CONV_COLS = 256


def _mixer_prompt_body(sink_ref, q_ref, k_ref, v_ref, u_ref, dww_ref, dwb_ref, lng_ref, lnb_ref, pw_ref, pwb_ref,
                       mix_ref, kprev, vprev, uext, ybuf):
    i = pl.program_id(0)
    tm = q_ref.shape[0]

    @pl.when(i == 0)
    def _():
        kprev[...] = jnp.zeros_like(kprev)
        vprev[...] = jnp.zeros_like(vprev)
        uext[0:CONV_PAD, :] = jnp.zeros((CONV_PAD, D_CONV), F32)

    qi = lax.broadcasted_iota(jnp.int32, (WINDOW, 2 * WINDOW), 0)
    kj = lax.broadcasted_iota(jnp.int32, (WINDOW, 2 * WINDOW), 1)
    diff = qi + WINDOW - kj
    band = (diff >= 0) & (diff <= WINDOW)
    kp, vp = kprev[...], vprev[...]
    for b in range(tm // WINDOW):
        rows = slice(b * WINDOW, (b + 1) * WINDOW)
        kb = k_ref[rows, :].astype(BF16)
        vb = v_ref[rows, :].astype(BF16)
        kk = jnp.concatenate([kp, kb], axis=0)
        vv = jnp.concatenate([vp, vb], axis=0)
        if b == 0:
            mask = band & (kj >= jnp.where(i > 0, 0, WINDOW))
        else:
            mask = band
        for h in range(N_KV_HEADS):
            cols = slice(h * HEAD_DIM, (h + 1) * HEAD_DIM)
            hq = slice(h * GQA_GROUP * HEAD_DIM, (h + 1) * GQA_GROUP * HEAD_DIM)
            o = _attend(q_ref[rows, hq], [(kk[:, cols], vv[:, cols], mask)], sink_ref, h)
            mix_ref[rows, hq] = o.astype(BF16)
        kp, vp = kb, vb
    kprev[...] = kp
    vprev[...] = vp

    uext[CONV_PAD:CONV_PAD + tm, :] = u_ref[...]
    off = CONV_PAD - (CONV_W - 1)
    for r in range(tm // CONV_ROWS):
        for c in range(D_CONV // CONV_COLS):
            cs = slice(c * CONV_COLS, (c + 1) * CONV_COLS)
            acc = jnp.zeros((CONV_ROWS, CONV_COLS), F32)
            for j in range(CONV_W):
                r0 = r * CONV_ROWS + off + j
                acc = acc + dww_ref[j:j + 1, cs] * uext[r0:r0 + CONV_ROWS, cs]
            ybuf[r * CONV_ROWS:(r + 1) * CONV_ROWS, cs] = acc
    uext[0:CONV_PAD, :] = uext[tm:tm + CONV_PAD, :]
    conv = _conv_tail(ybuf[...], dwb_ref, lng_ref, lnb_ref, pw_ref, pwb_ref)
    mix_ref[:, D_ATTN:] = conv.astype(BF16)


def _mixer_prompt(sink, q, k, v, u, dww, dwb, lng, lnb, pw_bf, pwb):
    n = q.shape[0]
    tm = TM_MIX
    row = lambda w: pl.BlockSpec((tm, w), lambda i, s: (i, 0))
    res = lambda shape: pl.BlockSpec(shape, lambda i, s: (0,) * len(shape), pipeline_mode=pl.Buffered(1))
    return pl.pallas_call(
        _mixer_prompt_body,
        grid_spec=pltpu.PrefetchScalarGridSpec(
            num_scalar_prefetch=1,
            grid=(n // tm,),
            in_specs=[row(D_ATTN), row(KV_DIM), row(KV_DIM), row(D_CONV), res((CONV_W, D_CONV)), res((1, D_CONV)),
                      res((1, D_CONV)), res((1, D_CONV)), res((D_CONV, D_CONV)), res((1, D_CONV))],
            out_specs=row(D_MODEL),
            scratch_shapes=[pltpu.VMEM((WINDOW, KV_DIM), BF16), pltpu.VMEM((WINDOW, KV_DIM), BF16),
                            pltpu.VMEM((CONV_PAD + tm, D_CONV), F32), pltpu.VMEM((tm, D_CONV), F32)],
        ),
        out_shape=jax.ShapeDtypeStruct((n, D_MODEL), BF16),
        compiler_params=_params(),
        name="mixer_prompt",
    )(sink, q, k, v, u, dww, dwb, lng, lnb, pw_bf, pwb)


UEXT_ROWS = 40


def _mixer_sample_body(sink_ref, q_ref, k_ref, v_ref, u_ref, ck_ref, cv_ref, cs_ref, dww_ref, dwb_ref, lng_ref,
                       lnb_ref, pw_ref, pwb_ref, mix_ref, wk_ref, wv_ref, wc_ref, uext, wsh, ybuf):
    i = pl.program_id(0)
    nt = q_ref.shape[0]
    sb = ck_ref.shape[0]
    t_new = nt // sb
    n_cache = sb * WINDOW

    @pl.when(i == 0)
    def _():
        uext[...] = jnp.zeros_like(uext)
        wsh[...] = jnp.zeros_like(wsh)
        for t in range(t_new):
            wsh[t, t:t + CONV_W, :] = dww_ref[...]

    knew = k_ref[...]
    vnew = v_ref[...]
    kc = ck_ref[...].reshape(n_cache, KV_DIM).astype(BF16)
    vc = cv_ref[...].reshape(n_cache, KV_DIM).astype(BF16)
    zpad = jnp.zeros((LANES - nt, KV_DIM), F32)
    kn = jnp.concatenate([knew, zpad], axis=0).astype(BF16)
    vn = jnp.concatenate([vnew, zpad], axis=0).astype(BF16)
    lt, lw = t_new.bit_length() - 1, WINDOW.bit_length() - 1
    qr = lax.broadcasted_iota(jnp.int32, (nt, n_cache), 0)
    cc = lax.broadcasted_iota(jnp.int32, (nt, n_cache), 1)
    mask_c = ((cc >> lw) == (qr >> lt)) & ((cc & (WINDOW - 1)) >= (qr & (t_new - 1)))
    qr = lax.broadcasted_iota(jnp.int32, (nt, LANES), 0)
    cn = lax.broadcasted_iota(jnp.int32, (nt, LANES), 1)
    mask_n = ((cn >> lt) == (qr >> lt)) & ((cn & (t_new - 1)) <= (qr & (t_new - 1)))
    for h in range(N_KV_HEADS):
        cols = slice(h * HEAD_DIM, (h + 1) * HEAD_DIM)
        hq = slice(h * GQA_GROUP * HEAD_DIM, (h + 1) * GQA_GROUP * HEAD_DIM)
        parts = [(kc[:, cols], vc[:, cols], mask_c), (kn[:, cols], vn[:, cols], mask_n)]
        o = _attend(q_ref[:, hq], parts, sink_ref, h)
        mix_ref[:, hq] = o.astype(BF16)

    n_state = CONV_W - 1
    for s in range(sb):
        new = slice(s * t_new, (s + 1) * t_new)
        wk_ref[s, 0:WINDOW - t_new, :] = ck_ref[s, t_new:WINDOW, :]
        wk_ref[s, WINDOW - t_new:WINDOW, :] = knew[new]
        wv_ref[s, 0:WINDOW - t_new, :] = cv_ref[s, t_new:WINDOW, :]
        wv_ref[s, WINDOW - t_new:WINDOW, :] = vnew[new]
        uext[s, 0:n_state, :] = cs_ref[s]
        uext[s, n_state:n_state + t_new, :] = u_ref[new, :]
        wc_ref[s] = uext[s, t_new:t_new + n_state, :]
        ue = uext[s]
        for t in range(t_new):
            ybuf[s * t_new + t:s * t_new + t + 1, :] = jnp.sum(wsh[t] * ue, axis=0, keepdims=True)
    conv = _conv_tail(ybuf[...], dwb_ref, lng_ref, lnb_ref, pw_ref, pwb_ref)
    mix_ref[:, D_ATTN:] = conv.astype(BF16)


def _mixer_sample(sink, q, k, v, u, ck, cv, cs, dww, dwb, lng, lnb, pw_bf, pwb):
    n = q.shape[0]
    n_seq = ck.shape[0]
    t_new = n // n_seq
    sb = SEQ_TILE
    nt = sb * t_new
    n_state = CONV_W - 1
    row = lambda w: pl.BlockSpec((nt, w), lambda i, s: (i, 0))
    seq = lambda r, w: pl.BlockSpec((sb, r, w), lambda i, s: (i, 0, 0))
    res = lambda shape: pl.BlockSpec(shape, lambda i, s: (0,) * len(shape), pipeline_mode=pl.Buffered(1))
    return pl.pallas_call(
        _mixer_sample_body,
        grid_spec=pltpu.PrefetchScalarGridSpec(
            num_scalar_prefetch=1,
            grid=(n_seq // sb,),
            in_specs=[row(D_ATTN), row(KV_DIM), row(KV_DIM), row(D_CONV), seq(WINDOW, KV_DIM), seq(WINDOW, KV_DIM),
                      seq(n_state, D_CONV), res((CONV_W, D_CONV)), res((1, D_CONV)), res((1, D_CONV)),
                      res((1, D_CONV)), res((D_CONV, D_CONV)), res((1, D_CONV))],
            out_specs=[row(D_MODEL), seq(WINDOW, KV_DIM), seq(WINDOW, KV_DIM), seq(n_state, D_CONV)],
            scratch_shapes=[pltpu.VMEM((sb, UEXT_ROWS, D_CONV), F32), pltpu.VMEM((t_new, UEXT_ROWS, D_CONV), F32),
                            pltpu.VMEM((nt, D_CONV), F32)],
        ),
        out_shape=[jax.ShapeDtypeStruct((n, D_MODEL), BF16), jax.ShapeDtypeStruct(ck.shape, F32),
                   jax.ShapeDtypeStruct(cv.shape, F32), jax.ShapeDtypeStruct(cs.shape, F32)],
        compiler_params=_params(),
        name="mixer_sample",
    )(sink, q, k, v, u, ck, cv, cs, dww, dwb, lng, lnb, pw_bf, pwb)


META_E0, META_E1, META_G0, META_G1, META_R0, META_R1 = range(6)


def _out_router_body(mixp_ref, xp_ref, mixs_ref, xs_ref, wout_ref, g2_ref, rw_ref, rb_ref, h_ref, meta_ref, cnt_ref,
                     carry, *, prompt_tiles):
    i = pl.program_id(0)
    tm = xp_ref.shape[0]

    @pl.when(i == 0)
    def _():
        carry[...] = jnp.zeros_like(carry)

    is_prompt = i < prompt_tiles
    x = jnp.where(is_prompt, xp_ref[...], xs_ref[...])
    mix = jnp.where(is_prompt, mixp_ref[...], mixs_ref[...])
    h = x + _dot(mix, wout_ref[...])
    h_ref[...] = h
    xn = _rmsnorm(h, g2_ref[...]).astype(BF16)
    logits = _dot(xn, rw_ref[...]) + rb_ref[...]
    lane = lax.broadcasted_iota(jnp.int32, logits.shape, 1).astype(F32)
    first = lambda cond: jnp.min(jnp.where(cond, lane, float(LANES)), axis=-1, keepdims=True)

    gl = jnp.where(lane < N_GROUPS, logits, -jnp.inf)
    gmax = jnp.max(gl, axis=-1, keepdims=True)
    gidx = first(gl == gmax)
    gval = 1.0 / jnp.sum(jnp.exp(gl - gmax), axis=-1, keepdims=True)

    lo = ROUTER_LANE0 + EXPERTS_PER_GROUP * gidx
    el = jnp.where((lane >= lo) & (lane < lo + EXPERTS_PER_GROUP), logits, -jnp.inf)
    m1 = jnp.max(el, axis=-1, keepdims=True)
    i1 = first(el == m1)
    el2 = jnp.where(lane == i1, -jnp.inf, el)
    m2 = jnp.max(el2, axis=-1, keepdims=True)
    i2 = first(el2 == m2)
    r = jnp.exp(m2 - m1)
    g0 = gval * (1.0 / (1.0 + r))
    g1 = gval * (r / (1.0 + r))

    sel0, sel1 = lane == i1, lane == i2
    onehot = jnp.where(sel0 | sel1, 1.0, 0.0)
    ti = lax.broadcasted_iota(jnp.int32, (tm, tm), 0)
    tj = lax.broadcasted_iota(jnp.int32, (tm, tm), 1)
    lower = jnp.where(tj < ti, 1.0, 0.0).astype(BF16)
    before = _dot(lower, onehot.astype(BF16)) + carry[0:1, :]
    r0 = jnp.sum(jnp.where(sel0, before, 0.0), axis=-1, keepdims=True)
    r1 = jnp.sum(jnp.where(sel1, before, 0.0), axis=-1, keepdims=True)
    carry[...] = carry[...] + jnp.sum(onehot, axis=0, keepdims=True)
    cnt_ref[...] = carry[...]

    lane_i = lax.broadcasted_iota(jnp.int32, logits.shape, 1)
    meta = jnp.zeros(logits.shape, F32)
    for slot, val in ((META_E0, i1 - ROUTER_LANE0), (META_E1, i2 - ROUTER_LANE0), (META_G0, g0), (META_G1, g1),
                      (META_R0, r0), (META_R1, r1)):
        meta = jnp.where(lane_i == slot, val, meta)
    meta_ref[...] = meta


def _out_router(mix_p, x_p, mix_s, x_s, wout_bf, g2, rw_bf, rb):
    n_p, n_s = x_p.shape[0], x_s.shape[0]
    tm = TM_PROJ
    assert n_p % tm == 0 and n_s % tm == 0
    tp, ts = n_p // tm, n_s // tm
    n = n_p + n_s
    prow = lambda w: pl.BlockSpec((tm, w), lambda i: (jnp.minimum(i, tp - 1), 0))
    srow = lambda w: pl.BlockSpec((tm, w), lambda i: (jnp.maximum(i - tp, 0), 0))
    row = lambda w: pl.BlockSpec((tm, w), lambda i: (i, 0))
    return pl.pallas_call(
        functools.partial(_out_router_body, prompt_tiles=tp),
        grid=(tp + ts,),
        in_specs=[prow(D_MODEL), prow(D_MODEL), srow(D_MODEL), srow(D_MODEL), _resident((D_MODEL, D_MODEL)),
                  _resident((1, D_MODEL)), _resident((D_MODEL, LANES)), _resident((1, LANES))],
        out_specs=[row(D_MODEL), row(LANES), pl.BlockSpec((8, LANES), lambda i: (0, 0))],
        out_shape=[jax.ShapeDtypeStruct((n, D_MODEL), F32), jax.ShapeDtypeStruct((n, LANES), F32),
                   jax.ShapeDtypeStruct((8, LANES), F32)],
        scratch_shapes=[pltpu.VMEM((8, LANES), F32)],
        compiler_params=_params(),
        name="out_router",
    )(mix_p, x_p, mix_s, x_s, wout_bf, g2, rw_bf, rb)


def _inverse_map_body(pos_ref, dst_ref):
    n_rows = dst_ref.shape[0]
    n_pairs = pos_ref.shape[0]

    def clear(r, c):
        dst_ref[r] = -1
        return c

    lax.fori_loop(0, n_rows, clear, 0)

    def put(p, c):
        dst_ref[pos_ref[p]] = p
        return c

    lax.fori_loop(0, n_pairs, put, 0)


def _inverse_map(pos, n_rows):
    return pl.pallas_call(
        _inverse_map_body,
        in_specs=[pl.BlockSpec(memory_space=pltpu.SMEM)],
        out_specs=pl.BlockSpec(memory_space=pltpu.SMEM),
        out_shape=jax.ShapeDtypeStruct((n_rows,), jnp.int32),
        name="inverse_map",
    )(pos)


def _moe_body(te_ref, nv_ref, dst_ref, h_ref, g2_ref, wg_ref, wu_ref, wd_ref, y_ref, xbuf, ybuf, hid, gsem, ssem):
    i = pl.program_id(0)
    n_steps = pl.num_programs(0)
    n_tok = h_ref.shape[0]
    slot = i % 2

    def gather(tile, buf_slot):
        def issue(j, c):
            d = dst_ref[tile * TR + j]
            tok = jnp.where(d >= n_tok, d - n_tok, jnp.maximum(d, 0))
            pltpu.make_async_copy(h_ref.at[pl.ds(tok, 1), :], xbuf.at[buf_slot, pl.ds(j, 1), :],
                                  gsem.at[buf_slot]).start()
            return c
        lax.fori_loop(0, TR, issue, 0)

    def scatter_row(tile, j):
        d = dst_ref[tile * TR + j]
        return pltpu.make_async_copy(ybuf.at[pl.ds(j, 1), :], y_ref.at[pl.ds(d, 1), :], ssem)

    def wait_scatter(tile):
        def w(j, c):
            scatter_row(tile, j).wait()
            return c
        lax.fori_loop(0, nv_ref[tile], w, 0)

    valid = nv_ref[i] > 0

    @pl.when(valid)
    def _():
        @pl.when(i == 0)
        def _():
            gather(0, 0)

        @pl.when(i + 1 < n_steps)
        def _():
            @pl.when(nv_ref[jnp.minimum(i + 1, n_steps - 1)] > 0)
            def _():
                gather(i + 1, 1 - slot)

        pltpu.make_async_copy(h_ref.at[pl.ds(0, TR), :], xbuf.at[slot], gsem.at[slot]).wait()
        xn = _rmsnorm(xbuf[slot], g2_ref[...]).astype(BF16)
        hg = _dot(xn, wg_ref[0].astype(BF16))
        hu = _dot(xn, wu_ref[0].astype(BF16))
        hid[...] = (hg * jax.nn.sigmoid(hg) * hu).astype(BF16)

    @pl.when(i > 0)
    def _():
        wait_scatter(jnp.maximum(i - 1, 0))

    @pl.when(valid)
    def _():
        ybuf[...] = _dot(hid[...], wd_ref[0].astype(BF16))

        def issue(j, c):
            scatter_row(i, j).start()
            return c
        lax.fori_loop(0, nv_ref[i], issue, 0)

    @pl.when(i == n_steps - 1)
    def _():
        wait_scatter(i)


def _moe(te, nv, dst, h_all, g2, wg, wu, wd):
    n_tok = h_all.shape[0]
    n_tiles = te.shape[0]
    wspec = lambda shape: pl.BlockSpec((1,) + shape, lambda i, te, nv, dst: (te[i], 0, 0))
    return pl.pallas_call(
        _moe_body,
        grid_spec=pltpu.PrefetchScalarGridSpec(
            num_scalar_prefetch=3,
            grid=(n_tiles,),
            in_specs=[pl.BlockSpec(memory_space=pl.ANY),
                      pl.BlockSpec((1, D_MODEL), lambda i, te, nv, dst: (0, 0)),
                      wspec((D_MODEL, D_EXPERT)), wspec((D_MODEL, D_EXPERT)), wspec((D_EXPERT, D_MODEL))],
            out_specs=pl.BlockSpec(memory_space=pl.ANY),
            scratch_shapes=[pltpu.VMEM((2, TR, D_MODEL), F32), pltpu.VMEM((TR, D_MODEL), F32),
                            pltpu.VMEM((TR, D_EXPERT), BF16), pltpu.SemaphoreType.DMA((2,)),
                            pltpu.SemaphoreType.DMA(())],
        ),
        out_shape=jax.ShapeDtypeStruct((2 * n_tok, D_MODEL), F32),
        compiler_params=_params(),
        name="moe",
    )(te, nv, dst, h_all, g2, wg, wu, wd)


def _combine_body(h_ref, meta_ref, y0_ref, y1_ref, g_ref, o_ref):
    meta = meta_ref[...]
    g0 = meta[:, META_G0:META_G0 + 1]
    g1 = meta[:, META_G1:META_G1 + 1]
    h = h_ref[...] + (g0 * y0_ref[...] + g1 * y1_ref[...])
    o_ref[...] = _rmsnorm(h, g_ref[...])


def _combine(h_all, meta_all, y2, gf, row0, n):
    n_tok = h_all.shape[0]
    tm = min(TM_OUT, n)
    b0 = row0 // tm
    b1 = (n_tok + row0) // tm
    return pl.pallas_call(
        _combine_body,
        grid=(n // tm,),
        in_specs=[pl.BlockSpec((tm, D_MODEL), lambda i: (b0 + i, 0)),
                  pl.BlockSpec((tm, LANES), lambda i: (b0 + i, 0)),
                  pl.BlockSpec((tm, D_MODEL), lambda i: (b0 + i, 0)),
                  pl.BlockSpec((tm, D_MODEL), lambda i: (b1 + i, 0)),
                  _resident((1, D_MODEL))],
        out_specs=pl.BlockSpec((tm, D_MODEL), lambda i: (i, 0)),
        out_shape=jax.ShapeDtypeStruct((n, D_MODEL), F32),
        compiler_params=_params(),
        name="combine",
    )(h_all, meta_all, y2, y2, gf)


def _routing_tables(meta_all, counts):
    n_tok = meta_all.shape[0]
    n_tiles = (2 * n_tok) // TR + N_EXPERTS
    cnt = counts[ROUTER_LANE0:ROUTER_LANE0 + N_EXPERTS].astype(jnp.int32)
    tiles_e = (cnt + TR - 1) // TR
    tile_end = jnp.cumsum(tiles_e)
    row_off = (tile_end - tiles_e) * TR
    e0 = meta_all[:, META_E0].astype(jnp.int32)
    e1 = meta_all[:, META_E1].astype(jnp.int32)
    pos0 = row_off[e0] + meta_all[:, META_R0].astype(jnp.int32)
    pos1 = row_off[e1] + meta_all[:, META_R1].astype(jnp.int32)
    pos = jnp.concatenate([pos0, pos1])
    tile = jnp.arange(n_tiles, dtype=jnp.int32)
    te = jnp.sum((tile[:, None] >= tile_end[None, :]).astype(jnp.int32), axis=1)
    te = jnp.minimum(te, N_EXPERTS - 1)
    used = tile < tile_end[-1]
    rows_left = cnt[te] - (tile - (tile_end - tiles_e)[te]) * TR
    nv = jnp.where(used, jnp.clip(rows_left, 0, TR), 0).astype(jnp.int32)
    last_e = te[jnp.maximum(tile_end[-1] - 1, 0)]
    te = jnp.where(used, te, last_e)
    return pos, te, nv, n_tiles


def kernel(x_prompt, x_sample, cache_win_k, cache_win_v, state_conv, norm1_g, w_in, attn_sink, conv_dw_w, conv_dw_b,
           conv_ln_g, conv_ln_b, conv_pw_w, conv_pw_b, w_out, norm2_g, router_group_w, router_group_b,
           router_expert_w, router_expert_b, expert_w_gate, expert_w_up, expert_w_down, final_norm_g):
    depth = w_in.shape[0]
    assert depth == 1, "single-layer step"
    bp, sp, _ = x_prompt.shape
    assert bp == 1, "one prompt sequence"
    n_seq, t_new, _ = x_sample.shape
    n_p, n_s = bp * sp, n_seq * t_new
    n_tok = n_p + n_s
    l = 0

    row = lambda a: a.reshape(1, -1)
    w_in_bf = w_in[l].astype(BF16)
    w_out_bf = w_out[l].astype(BF16)
    pw_bf = conv_pw_w[l].astype(BF16)
    rw = jnp.concatenate([router_group_w[l], router_expert_w[l]], axis=1)
    rw_bf = jnp.pad(rw, ((0, 0), (0, LANES - rw.shape[1]))).astype(BF16)
    rb = jnp.pad(jnp.concatenate([router_group_b[l], router_expert_b[l]]), (0, LANES - rw.shape[1])).reshape(1, LANES)
    g1, g2, gf = row(norm1_g[l]), row(norm2_g[l]), row(final_norm_g)
    sink = attn_sink[l]
    conv_w = (conv_dw_w[l], row(conv_dw_b[l]), row(conv_ln_g[l]), row(conv_ln_b[l]), pw_bf, row(conv_pw_b[l]))

    xp = x_prompt.reshape(n_p, D_MODEL)
    xs = x_sample.reshape(n_s, D_MODEL)
    ck = cache_win_k[l].reshape(n_seq, WINDOW, KV_DIM)
    cv = cache_win_v[l].reshape(n_seq, WINDOW, KV_DIM)
    cs = state_conv[l]

    qp, kp, vp, up = _in_proj(xp, g1, w_in_bf)
    qs, ks, vs, us = _in_proj(xs, g1, w_in_bf)
    mix_p = _mixer_prompt(sink, qp, kp, vp, up, *conv_w)
    mix_s, wk_s, wv_s, wc_s = _mixer_sample(sink, qs, ks, vs, us, ck, cv, cs, *conv_w)

    h_all, meta_all, cnt = _out_router(mix_p, xp, mix_s, xs, w_out_bf, g2, rw_bf, rb)

    pos, te, nv, n_tiles = _routing_tables(meta_all, cnt[0])
    dst = _inverse_map(pos, n_tiles * TR)
    y2 = _moe(te, nv, dst, h_all, g2, expert_w_gate[l], expert_w_up[l], expert_w_down[l])
    y_p = _combine(h_all, meta_all, y2, gf, 0, n_p)
    y_s = _combine(h_all, meta_all, y2, gf, n_p, n_s)

    kv_shape = (depth, bp, WINDOW, N_KV_HEADS, HEAD_DIM)
    kv_shape_s = (depth, n_seq, WINDOW, N_KV_HEADS, HEAD_DIM)
    return (y_p.reshape(bp, sp, D_MODEL), y_s.reshape(n_seq, t_new, D_MODEL),
            kp[n_p - WINDOW:].reshape(kv_shape), vp[n_p - WINDOW:].reshape(kv_shape),
            up[n_p - (CONV_W - 1):].reshape(depth, bp, CONV_W - 1, D_CONV),
            wk_s.reshape(kv_shape_s), wv_s.reshape(kv_shape_s), wc_s.reshape(depth, n_seq, CONV_W - 1, D_CONV))
```

```python
import functools

import jax
import jax.numpy as jnp
from jax import lax
from jax.experimental import pallas as pl
from jax.experimental.pallas import tpu as pltpu

F32 = jnp.float32
BF16 = jnp.bfloat16

D_MODEL = 2048
HEAD_DIM = 64
N_HEADS = 16
N_KV_HEADS = 2
GQA_GROUP = 8
KV_DIM = N_KV_HEADS * HEAD_DIM
D_ATTN = N_HEADS * HEAD_DIM
D_CONV = D_MODEL - D_ATTN
WINDOW = 128
CONV_W = 31
D_IN = D_ATTN + 2 * KV_DIM + 2 * D_CONV
N_GROUPS = 4
EXPERTS_PER_GROUP = 8
N_EXPERTS = N_GROUPS * EXPERTS_PER_GROUP
D_EXPERT = 256
RMS_EPS = 1e-6
LN_EPS = 1e-5

LANES = 128
SUBLANES = 8
ROUTER_LANE0 = N_GROUPS
VMEM_LIMIT = 56 * 1024 * 1024

TM_PROJ = 512
TM_MIX = 256
SEQ_TILE = 8
TR = 256
TM_OUT = 256

_NT = (((1,), (1,)), ((), ()))


def _params(n_axes=1):
    return pltpu.CompilerParams(dimension_semantics=("arbitrary",) * n_axes, vmem_limit_bytes=VMEM_LIMIT)


def _resident(shape):
    return pl.BlockSpec(shape, lambda *_: (0,) * len(shape), pipeline_mode=pl.Buffered(1))


def _rmsnorm(x, g):
    ms = jnp.mean(x * x, axis=-1, keepdims=True)
    return x * lax.rsqrt(ms + RMS_EPS) * g


def _dot(a, b):
    return jnp.dot(a, b, preferred_element_type=F32)


def _in_proj_body(x_ref, g_ref, w_ref, q_ref, k_ref, v_ref, u_ref):
    xn = _rmsnorm(x_ref[...], g_ref[...]).astype(BF16)
    q_ref[...] = (_dot(xn, w_ref[:, :D_ATTN]) * (HEAD_DIM ** -0.5)).astype(BF16)
    kv = _dot(xn, w_ref[:, D_ATTN:D_ATTN + 2 * KV_DIM])
    k_ref[...] = kv[:, :KV_DIM]
    v_ref[...] = kv[:, KV_DIM:]
    c0 = D_ATTN + 2 * KV_DIM
    cw = 256
    for j in range(D_CONV // cw):
        a = _dot(xn, w_ref[:, c0 + j * cw:c0 + (j + 1) * cw])
        b = _dot(xn, w_ref[:, c0 + D_CONV + j * cw:c0 + D_CONV + (j + 1) * cw])
        u_ref[:, j * cw:(j + 1) * cw] = a * jax.nn.sigmoid(b)


def _in_proj(x, g, w_bf):
    n = x.shape[0]
    tm = min(TM_PROJ, n)
    row = lambda w: pl.BlockSpec((tm, w), lambda i: (i, 0))
    return pl.pallas_call(
        _in_proj_body,
        grid=(n // tm,),
        in_specs=[row(D_MODEL), _resident((1, D_MODEL)), _resident((D_MODEL, D_IN))],
        out_specs=[row(D_ATTN), row(KV_DIM), row(KV_DIM), row(D_CONV)],
        out_shape=[jax.ShapeDtypeStruct((n, D_ATTN), BF16), jax.ShapeDtypeStruct((n, KV_DIM), F32),
                   jax.ShapeDtypeStruct((n, KV_DIM), F32), jax.ShapeDtypeStruct((n, D_CONV), F32)],
        compiler_params=_params(),
        name="in_proj",
    )(x, g, w_bf)


def _attend(q, parts, sink_ref, kv_head):
    t = q.shape[0]
    qs = jnp.concatenate([q[:, g * HEAD_DIM:(g + 1) * HEAD_DIM] for g in range(GQA_GROUP)], axis=0)
    scores = [lax.dot_general(qs, kk, _NT, preferred_element_type=F32) for kk, _, _ in parts]
    ps = [[] for _ in parts]
    inv = []
    for g in range(GQA_GROUP):
        sg = [jnp.where(mask, s[g * t:(g + 1) * t], -jnp.inf) for s, (_, _, mask) in zip(scores, parts)]
        sk = sink_ref[kv_head * GQA_GROUP + g]
        m = sk
        for x in sg:
            m = jnp.maximum(jnp.max(x, axis=-1, keepdims=True), m)
        den = jnp.exp(sk - m)
        for k, x in enumerate(sg):
            p = jnp.exp(x - m)
            den = den + jnp.sum(p, axis=-1, keepdims=True)
            ps[k].append(p.astype(BF16))
        inv.append(1.0 / den)
    o = None
    for k, (_, vv, _) in enumerate(parts):
        ok = _dot(jnp.concatenate(ps[k], axis=0), vv)
        o = ok if o is None else o + ok
    return jnp.concatenate([o[g * t:(g + 1) * t] * inv[g] for g in range(GQA_GROUP)], axis=1)


def _conv_tail(y, dwb_ref, lng_ref, lnb_ref, pw_ref, pwb_ref):
    y = y + dwb_ref[...]
    mu = jnp.mean(y, axis=-1, keepdims=True)
    yc = y - mu
    yn = yc * lax.rsqrt(jnp.mean(yc * yc, axis=-1, keepdims=True) + LN_EPS)
    yn = yn * lng_ref[...] + lnb_ref[...]
    act = yn * jax.nn.sigmoid(yn)
    return _dot(act.astype(BF16), pw_ref[...]) + pwb_ref[...]


CONV_PAD = 32
CONV_ROWS = 32
CONV_COLS = 256


def _mixer_prompt_body(sink_ref, q_ref, k_ref, v_ref, u_ref, dww_ref, dwb_ref, lng_ref, lnb_ref, pw_ref, pwb_ref,
                       mix_ref, kprev, vprev, uext, ushift, ybuf):
    i = pl.program_id(0)
    tm = q_ref.shape[0]

    @pl.when(i == 0)
    def _():
        kprev[...] = jnp.zeros_like(kprev)
        vprev[...] = jnp.zeros_like(vprev)
        uext[0:CONV_PAD, :] = jnp.zeros((CONV_PAD, D_CONV), F32)

    qi = lax.broadcasted_iota(jnp.int32, (WINDOW, 2 * WINDOW), 0)
    kj = lax.broadcasted_iota(jnp.int32, (WINDOW, 2 * WINDOW), 1)
    diff = qi + WINDOW - kj
    band = (diff >= 0) & (diff <= WINDOW)
    kp, vp = kprev[...], vprev[...]
    for b in range(tm // WINDOW):
        rows = slice(b * WINDOW, (b + 1) * WINDOW)
        kb = k_ref[rows, :].astype(BF16)
        vb = v_ref[rows, :].astype(BF16)
        kk = jnp.concatenate([kp, kb], axis=0)
        vv = jnp.concatenate([vp, vb], axis=0)
        if b == 0:
            mask = band & (kj >= jnp.where(i > 0, 0, WINDOW))
        else:
            mask = band
        for h in range(N_KV_HEADS):
            cols = slice(h * HEAD_DIM, (h + 1) * HEAD_DIM)
            hq = slice(h * GQA_GROUP * HEAD_DIM, (h + 1) * GQA_GROUP * HEAD_DIM)
            o = _attend(q_ref[rows, hq], [(kk[:, cols], vv[:, cols], mask)], sink_ref, h)
            mix_ref[rows, hq] = o.astype(BF16)
        kp, vp = kb, vb
    kprev[...] = kp
    vprev[...] = vp

    uext[CONV_PAD:CONV_PAD + tm, :] = u_ref[...]
    n_shift_rows = ushift.shape[1]
    for s in range(1, SUBLANES):
        ushift[s - 1] = uext[s:s + n_shift_rows, :]
    off = CONV_PAD - (CONV_W - 1)
    for r in range(tm // CONV_ROWS):
        for c in range(D_CONV // CONV_COLS):
            cs = slice(c * CONV_COLS, (c + 1) * CONV_COLS)
            acc = jnp.zeros((CONV_ROWS, CONV_COLS), F32)
            for j in range(CONV_W):
                a, s = divmod(off + j, SUBLANES)
                r0 = r * CONV_ROWS + a * SUBLANES
                src = uext[r0:r0 + CONV_ROWS, cs] if s == 0 else ushift[s - 1, r0:r0 + CONV_ROWS, cs]
                acc = acc + dww_ref[j:j + 1, cs] * src
            ybuf[r * CONV_ROWS:(r + 1) * CONV_ROWS, cs] = acc
    uext[0:CONV_PAD, :] = uext[tm:tm + CONV_PAD, :]
    conv = _conv_tail(ybuf[...], dwb_ref, lng_ref, lnb_ref, pw_ref, pwb_ref)
    mix_ref[:, D_ATTN:] = conv.astype(BF16)


def _mixer_prompt(sink, q, k, v, u, dww, dwb, lng, lnb, pw_bf, pwb):
    n = q.shape[0]
    tm = TM_MIX
    row = lambda w: pl.BlockSpec((tm, w), lambda i, s: (i, 0))
    res = lambda shape: pl.BlockSpec(shape, lambda i, s: (0,) * len(shape), pipeline_mode=pl.Buffered(1))
    return pl.pallas_call(
        _mixer_prompt_body,
        grid_spec=pltpu.PrefetchScalarGridSpec(
            num_scalar_prefetch=1,
            grid=(n // tm,),
            in_specs=[row(D_ATTN), row(KV_DIM), row(KV_DIM), row(D_CONV), res((CONV_W, D_CONV)), res((1, D_CONV)),
                      res((1, D_CONV)), res((1, D_CONV)), res((D_CONV, D_CONV)), res((1, D_CONV))],
            out_specs=row(D_MODEL),
            scratch_shapes=[pltpu.VMEM((WINDOW, KV_DIM), BF16), pltpu.VMEM((WINDOW, KV_DIM), BF16),
                            pltpu.VMEM((CONV_PAD + tm, D_CONV), F32),
                            pltpu.VMEM((SUBLANES - 1, CONV_PAD - SUBLANES + tm, D_CONV), F32),
                            pltpu.VMEM((tm, D_CONV), F32)],
        ),
        out_shape=jax.ShapeDtypeStruct((n, D_MODEL), BF16),
        compiler_params=_params(),
        name="mixer_prompt",
    )(sink, q, k, v, u, dww, dwb, lng, lnb, pw_bf, pwb)


UEXT_ROWS = 40


def _mixer_sample_body(sink_ref, q_ref, k_ref, v_ref, u_ref, ck_ref, cv_ref, cs_ref, dww_ref, dwb_ref, lng_ref,
                       lnb_ref, pw_ref, pwb_ref, mix_ref, wk_ref, wv_ref, wc_ref, uext, wsh, ybuf):
    i = pl.program_id(0)
    nt = q_ref.shape[0]
    sb = ck_ref.shape[0]
    t_new = nt // sb
    n_cache = sb * WINDOW

    @pl.when(i == 0)
    def _():
        uext[...] = jnp.zeros_like(uext)
        wsh[...] = jnp.zeros_like(wsh)
        for t in range(t_new):
            wsh[t, t:t + CONV_W, :] = dww_ref[...]

    knew = k_ref[...]
    vnew = v_ref[...]
    kc = ck_ref[...].reshape(n_cache, KV_DIM).astype(BF16)
    vc = cv_ref[...].reshape(n_cache, KV_DIM).astype(BF16)
    zpad = jnp.zeros((LANES - nt, KV_DIM), F32)
    kn = jnp.concatenate([knew, zpad], axis=0).astype(BF16)
    vn = jnp.concatenate([vnew, zpad], axis=0).astype(BF16)
    lt, lw = t_new.bit_length() - 1, WINDOW.bit_length() - 1
    qr = lax.broadcasted_iota(jnp.int32, (nt, n_cache), 0)
    cc = lax.broadcasted_iota(jnp.int32, (nt, n_cache), 1)
    mask_c = ((cc >> lw) == (qr >> lt)) & ((cc & (WINDOW - 1)) >= (qr & (t_new - 1)))
    qr = lax.broadcasted_iota(jnp.int32, (nt, LANES), 0)
    cn = lax.broadcasted_iota(jnp.int32, (nt, LANES), 1)
    mask_n = ((cn >> lt) == (qr >> lt)) & ((cn & (t_new - 1)) <= (qr & (t_new - 1)))
    for h in range(N_KV_HEADS):
        cols = slice(h * HEAD_DIM, (h + 1) * HEAD_DIM)
        hq = slice(h * GQA_GROUP * HEAD_DIM, (h + 1) * GQA_GROUP * HEAD_DIM)
        parts = [(kc[:, cols], vc[:, cols], mask_c), (kn[:, cols], vn[:, cols], mask_n)]
        o = _attend(q_ref[:, hq], parts, sink_ref, h)
        mix_ref[:, hq] = o.astype(BF16)

    n_state = CONV_W - 1
    for s in range(sb):
        new = slice(s * t_new, (s + 1) * t_new)
        wk_ref[s, 0:WINDOW - t_new, :] = ck_ref[s, t_new:WINDOW, :]
        wk_ref[s, WINDOW - t_new:WINDOW, :] = knew[new]
        wv_ref[s, 0:WINDOW - t_new, :] = cv_ref[s, t_new:WINDOW, :]
        wv_ref[s, WINDOW - t_new:WINDOW, :] = vnew[new]
        uext[s, 0:n_state, :] = cs_ref[s]
        uext[s, n_state:n_state + t_new, :] = u_ref[new, :]
        wc_ref[s] = uext[s, t_new:t_new + n_state, :]
        ue = uext[s]
        for t in range(t_new):
            ybuf[s * t_new + t:s * t_new + t + 1, :] = jnp.sum(wsh[t] * ue, axis=0, keepdims=True)
    conv = _conv_tail(ybuf[...], dwb_ref, lng_ref, lnb_ref, pw_ref, pwb_ref)
    mix_ref[:, D_ATTN:] = conv.astype(BF16)


def _mixer_sample(sink, q, k, v, u, ck, cv, cs, dww, dwb, lng, lnb, pw_bf, pwb):
    n = q.shape[0]
    n_seq = ck.shape[0]
    t_new = n // n_seq
    sb = SEQ_TILE
    nt = sb * t_new
    n_state = CONV_W - 1
    row = lambda w: pl.BlockSpec((nt, w), lambda i, s: (i, 0))
    seq = lambda r, w: pl.BlockSpec((sb, r, w), lambda i, s: (i, 0, 0))
    res = lambda shape: pl.BlockSpec(shape, lambda i, s: (0,) * len(shape), pipeline_mode=pl.Buffered(1))
    return pl.pallas_call(
        _mixer_sample_body,
        grid_spec=pltpu.PrefetchScalarGridSpec(
            num_scalar_prefetch=1,
            grid=(n_seq // sb,),
            in_specs=[row(D_ATTN), row(KV_DIM), row(KV_DIM), row(D_CONV), seq(WINDOW, KV_DIM), seq(WINDOW, KV_DIM),
                      seq(n_state, D_CONV), res((CONV_W, D_CONV)), res((1, D_CONV)), res((1, D_CONV)),
                      res((1, D_CONV)), res((D_CONV, D_CONV)), res((1, D_CONV))],
            out_specs=[row(D_MODEL), seq(WINDOW, KV_DIM), seq(WINDOW, KV_DIM), seq(n_state, D_CONV)],
            scratch_shapes=[pltpu.VMEM((sb, UEXT_ROWS, D_CONV), F32), pltpu.VMEM((t_new, UEXT_ROWS, D_CONV), F32),
                            pltpu.VMEM((nt, D_CONV), F32)],
        ),
        out_shape=[jax.ShapeDtypeStruct((n, D_MODEL), BF16), jax.ShapeDtypeStruct(ck.shape, F32),
                   jax.ShapeDtypeStruct(cv.shape, F32), jax.ShapeDtypeStruct(cs.shape, F32)],
        compiler_params=_params(),
        name="mixer_sample",
    )(sink, q, k, v, u, ck, cv, cs, dww, dwb, lng, lnb, pw_bf, pwb)


META_E0, META_E1, META_G0, META_G1, META_R0, META_R1 = range(6)


def _out_router_body(mixp_ref, xp_ref, mixs_ref, xs_ref, wout_ref, g2_ref, rw_ref, rb_ref, h_ref, meta_ref, cnt_ref,
                     carry, *, prompt_tiles):
    i = pl.program_id(0)
    tm = xp_ref.shape[0]

    @pl.when(i == 0)
    def _():
        carry[...] = jnp.zeros_like(carry)

    is_prompt = i < prompt_tiles
    x = jnp.where(is_prompt, xp_ref[...], xs_ref[...])
    mix = jnp.where(is_prompt, mixp_ref[...], mixs_ref[...])
    h = x + _dot(mix, wout_ref[...])
    h_ref[...] = h
    xn = _rmsnorm(h, g2_ref[...]).astype(BF16)
    logits = _dot(xn, rw_ref[...]) + rb_ref[...]
    lane = lax.broadcasted_iota(jnp.int32, logits.shape, 1).astype(F32)
    first = lambda cond: jnp.min(jnp.where(cond, lane, float(LANES)), axis=-1, keepdims=True)

    gl = jnp.where(lane < N_GROUPS, logits, -jnp.inf)
    gmax = jnp.max(gl, axis=-1, keepdims=True)
    gidx = first(gl == gmax)
    gval = 1.0 / jnp.sum(jnp.exp(gl - gmax), axis=-1, keepdims=True)

    lo = ROUTER_LANE0 + EXPERTS_PER_GROUP * gidx
    el = jnp.where((lane >= lo) & (lane < lo + EXPERTS_PER_GROUP), logits, -jnp.inf)
    m1 = jnp.max(el, axis=-1, keepdims=True)
    i1 = first(el == m1)
    el2 = jnp.where(lane == i1, -jnp.inf, el)
    m2 = jnp.max(el2, axis=-1, keepdims=True)
    i2 = first(el2 == m2)
    r = jnp.exp(m2 - m1)
    g0 = gval * (1.0 / (1.0 + r))
    g1 = gval * (r / (1.0 + r))

    sel0, sel1 = lane == i1, lane == i2
    onehot = jnp.where(sel0 | sel1, 1.0, 0.0)
    ti = lax.broadcasted_iota(jnp.int32, (tm, tm), 0)
    tj = lax.broadcasted_iota(jnp.int32, (tm, tm), 1)
    lower = jnp.where(tj < ti, 1.0, 0.0).astype(BF16)
    before = _dot(lower, onehot.astype(BF16)) + carry[0:1, :]
    r0 = jnp.sum(jnp.where(sel0, before, 0.0), axis=-1, keepdims=True)
    r1 = jnp.sum(jnp.where(sel1, before, 0.0), axis=-1, keepdims=True)
    carry[...] = carry[...] + jnp.sum(onehot, axis=0, keepdims=True)
    cnt_ref[...] = carry[...]

    lane_i = lax.broadcasted_iota(jnp.int32, logits.shape, 1)
    meta = jnp.zeros(logits.shape, F32)
    for slot, val in ((META_E0, i1 - ROUTER_LANE0), (META_E1, i2 - ROUTER_LANE0), (META_G0, g0), (META_G1, g1),
                      (META_R0, r0), (META_R1, r1)):
        meta = jnp.where(lane_i == slot, val, meta)
    meta_ref[...] = meta


def _out_router(mix_p, x_p, mix_s, x_s, wout_bf, g2, rw_bf, rb):
    n_p, n_s = x_p.shape[0], x_s.shape[0]
    tm = TM_PROJ
    assert n_p % tm == 0 and n_s % tm == 0
    tp, ts = n_p // tm, n_s // tm
    n = n_p + n_s
    prow = lambda w: pl.BlockSpec((tm, w), lambda i: (jnp.minimum(i, tp - 1), 0))
    srow = lambda w: pl.BlockSpec((tm, w), lambda i: (jnp.maximum(i - tp, 0), 0))
    row = lambda w: pl.BlockSpec((tm, w), lambda i: (i, 0))
    return pl.pallas_call(
        functools.partial(_out_router_body, prompt_tiles=tp),
        grid=(tp + ts,),
        in_specs=[prow(D_MODEL), prow(D_MODEL), srow(D_MODEL), srow(D_MODEL), _resident((D_MODEL, D_MODEL)),
                  _resident((1, D_MODEL)), _resident((D_MODEL, LANES)), _resident((1, LANES))],
        out_specs=[row(D_MODEL), row(LANES), pl.BlockSpec((8, LANES), lambda i: (0, 0))],
        out_shape=[jax.ShapeDtypeStruct((n, D_MODEL), F32), jax.ShapeDtypeStruct((n, LANES), F32),
                   jax.ShapeDtypeStruct((8, LANES), F32)],
        scratch_shapes=[pltpu.VMEM((8, LANES), F32)],
        compiler_params=_params(),
        name="out_router",
    )(mix_p, x_p, mix_s, x_s, wout_bf, g2, rw_bf, rb)


def _positions_body(meta_ref, off_ref, pos_ref):
    meta = meta_ref[...]
    lane = lax.broadcasted_iota(jnp.int32, meta.shape, 1)
    lane_f = lane.astype(F32)
    off = off_ref[...]
    pos = jnp.zeros(meta.shape, F32)
    for slot, (e_lane, r_lane) in enumerate(((META_E0, META_R0), (META_E1, META_R1))):
        e = meta[:, e_lane:e_lane + 1] + ROUTER_LANE0
        seg = jnp.sum(jnp.where(lane_f == e, off, 0.0), axis=-1, keepdims=True)
        pos = jnp.where(lane == slot, seg + meta[:, r_lane:r_lane + 1], pos)
    pos_ref[...] = pos.astype(jnp.int32)


def _positions(meta_all, off_lanes):
    n = meta_all.shape[0]
    tm = TM_PROJ
    assert n % tm == 0
    return pl.pallas_call(
        _positions_body,
        grid=(n // tm,),
        in_specs=[pl.BlockSpec((tm, LANES), lambda i: (i, 0)), _resident((1, LANES))],
        out_specs=pl.BlockSpec((tm, LANES), lambda i: (i, 0)),
        out_shape=jax.ShapeDtypeStruct((n, LANES), jnp.int32),
        compiler_params=_params(),
        name="positions",
    )(meta_all, off_lanes)


DMA_UNROLL = 8
N_LOAD_SLOTS = 3


def _pow2_sizes(n):
    return [1 << b for b in range(n.bit_length() - 1, -1, -1)]


def _dispatch_body(pos_ref, nv_ref, h_ref, hs_ref, hbuf, zbuf, lsem, ssem, zsem, *, n_tok, tm):
    i = pl.program_id(0)
    n_steps = pl.num_programs(0)
    n_tiles = nv_ref.shape[0]

    def load(t, slot):
        return pltpu.make_async_copy(h_ref.at[pl.ds(t * tm, tm), :], hbuf.at[slot], lsem.at[slot])

    def clear_unowned(start):
        def per_tile(t, c):
            nv = nv_ref[t]
            z = TR - nv
            row = t * TR + nv
            head = z & (SUBLANES - 1)
            for k in range(SUBLANES - 1):
                @pl.when(k < head)
                def _(k=k):
                    cp = pltpu.make_async_copy(zbuf.at[pl.ds(0, 1), :], hs_ref.at[pl.ds(row + k, 1), :], zsem)
                    cp.start() if start else cp.wait()
            row = row + head
            for size in _pow2_sizes(TR):
                if size < SUBLANES:
                    break
                @pl.when((z & size) != 0)
                def _(row=row, size=size):
                    dst = hs_ref.at[pl.ds(pl.multiple_of(row, SUBLANES), size), :]
                    cp = pltpu.make_async_copy(zbuf.at[pl.ds(0, size), :], dst, zsem)
                    cp.start() if start else cp.wait()
                row = row + (z & size)
            return c
        lax.fori_loop(0, n_tiles, per_tile, 0)

    def scatter_wait(slot):
        for _ in range(2):
            pltpu.make_async_copy(hbuf.at[slot], hs_ref.at[pl.ds(0, tm), :], ssem.at[slot]).wait()

    @pl.when(i == 0)
    def _():
        zbuf[...] = jnp.zeros_like(zbuf)
        clear_unowned(True)
        load(0, 0).start()

    @pl.when(i + 1 < n_steps)
    def _():
        load(i + 1, (i + 1) % N_LOAD_SLOTS).start()

    slot = i % N_LOAD_SLOTS
    load(i, slot).wait()

    def issue(jb, c):
        for u in range(DMA_UNROLL):
            j = jb * DMA_UNROLL + u
            for s in range(2):
                p = pos_ref[s * n_tok + i * tm + j]
                pltpu.make_async_copy(hbuf.at[slot, pl.ds(j, 1), :], hs_ref.at[pl.ds(p, 1), :], ssem.at[slot]).start()
        return c
    lax.fori_loop(0, tm // DMA_UNROLL, issue, 0)

    @pl.when(i > 0)
    def _():
        scatter_wait((i + N_LOAD_SLOTS - 1) % N_LOAD_SLOTS)

    @pl.when(i == n_steps - 1)
    def _():
        scatter_wait(slot)
        clear_unowned(False)


def _dispatch(pos, nv, h_all):
    n_tok = h_all.shape[0]
    n_tiles = nv.shape[0]
    tm = TM_PROJ
    assert n_tok % tm == 0
    return pl.pallas_call(
        functools.partial(_dispatch_body, n_tok=n_tok, tm=tm),
        grid_spec=pltpu.PrefetchScalarGridSpec(
            num_scalar_prefetch=2,
            grid=(n_tok // tm,),
            in_specs=[pl.BlockSpec(memory_space=pl.ANY)],
            out_specs=pl.BlockSpec(memory_space=pl.ANY),
            scratch_shapes=[pltpu.VMEM((N_LOAD_SLOTS, tm, D_MODEL), F32), pltpu.VMEM((TR, D_MODEL), F32),
                            pltpu.SemaphoreType.DMA((N_LOAD_SLOTS,)), pltpu.SemaphoreType.DMA((N_LOAD_SLOTS,)),
                            pltpu.SemaphoreType.DMA(())],
        ),
        out_shape=jax.ShapeDtypeStruct((n_tiles * TR, D_MODEL), F32),
        compiler_params=_params(),
        name="dispatch",
    )(pos, nv, h_all)


def _moe_body(te_ref, nv_ref, blk_ref, hs_ref, g2_ref, wg_ref, wu_ref, wd_ref, ys_ref, wg_bf, wu_bf, wd_bf):
    i = pl.program_id(0)
    valid = nv_ref[i] > 0

    @pl.when(valid)
    def _():
        @pl.when((i == 0) | (te_ref[i] != te_ref[jnp.maximum(i - 1, 0)]))
        def _():
            wg_bf[...] = wg_ref[0].astype(BF16)
            wu_bf[...] = wu_ref[0].astype(BF16)
            wd_bf[...] = wd_ref[0].astype(BF16)

        xn = _rmsnorm(hs_ref[...], g2_ref[...]).astype(BF16)
        hg = _dot(xn, wg_bf[...])
        hu = _dot(xn, wu_bf[...])
        hid = (hg * jax.nn.sigmoid(hg) * hu).astype(BF16)
        ys_ref[...] = _dot(hid, wd_bf[...])

    @pl.when(jnp.logical_not(valid))
    def _():
        ys_ref[...] = jnp.zeros_like(ys_ref)


def _moe(te, nv, blk, hs, g2, wg, wu, wd):
    n_tiles = te.shape[0]
    wspec = lambda shape: pl.BlockSpec((1,) + shape, lambda i, te, nv, blk: (te[i], 0, 0))
    return pl.pallas_call(
        _moe_body,
        grid_spec=pltpu.PrefetchScalarGridSpec(
            num_scalar_prefetch=3,
            grid=(n_tiles,),
            in_specs=[pl.BlockSpec((TR, D_MODEL), lambda i, te, nv, blk: (blk[i], 0)),
                      pl.BlockSpec((1, D_MODEL), lambda i, te, nv, blk: (0, 0)),
                      wspec((D_MODEL, D_EXPERT)), wspec((D_MODEL, D_EXPERT)), wspec((D_EXPERT, D_MODEL))],
            out_specs=pl.BlockSpec((TR, D_MODEL), lambda i, te, nv, blk: (i, 0)),
            scratch_shapes=[pltpu.VMEM((D_MODEL, D_EXPERT), BF16), pltpu.VMEM((D_MODEL, D_EXPERT), BF16),
                            pltpu.VMEM((D_EXPERT, D_MODEL), BF16)],
        ),
        out_shape=jax.ShapeDtypeStruct((n_tiles * TR, D_MODEL), F32),
        compiler_params=_params(),
        name="moe",
    )(te, nv, blk, hs, g2, wg, wu, wd)


def _combine_body(pos_ref, h_ref, meta_ref, g_ref, ys_ref, o_ref, ybuf, sem, *, n_tok, row0):
    i = pl.program_id(0)
    n_steps = pl.num_programs(0)
    tm = h_ref.shape[0]

    def gather(t, slot):
        def issue(jb, c):
            for u in range(DMA_UNROLL):
                j = jb * DMA_UNROLL + u
                for s in range(2):
                    p = pos_ref[s * n_tok + row0 + t * tm + j]
                    pltpu.make_async_copy(ys_ref.at[pl.ds(p, 1), :], ybuf.at[slot, s, pl.ds(j, 1), :],
                                          sem.at[slot]).start()
            return c
        lax.fori_loop(0, tm // DMA_UNROLL, issue, 0)

    @pl.when(i == 0)
    def _():
        gather(0, 0)

    @pl.when(i + 1 < n_steps)
    def _():
        gather(i + 1, (i + 1) % 2)

    slot = i % 2
    for s in range(2):
        pltpu.make_async_copy(ys_ref.at[pl.ds(0, tm), :], ybuf.at[slot, s], sem.at[slot]).wait()
    meta = meta_ref[...]
    g0 = meta[:, META_G0:META_G0 + 1]
    g1 = meta[:, META_G1:META_G1 + 1]
    h = h_ref[...] + (g0 * ybuf[slot, 0] + g1 * ybuf[slot, 1])
    o_ref[...] = _rmsnorm(h, g_ref[...])


def _combine(pos, h_all, meta_all, ys, gf, row0, n):
    n_tok = h_all.shape[0]
    tm = min(TM_OUT, n)
    assert row0 % tm == 0 and n % tm == 0
    b0 = row0 // tm
    return pl.pallas_call(
        functools.partial(_combine_body, n_tok=n_tok, row0=row0),
        grid_spec=pltpu.PrefetchScalarGridSpec(
            num_scalar_prefetch=1,
            grid=(n // tm,),
            in_specs=[pl.BlockSpec((tm, D_MODEL), lambda i, pos: (b0 + i, 0)),
                      pl.BlockSpec((tm, LANES), lambda i, pos: (b0 + i, 0)),
                      pl.BlockSpec((1, D_MODEL), lambda i, pos: (0, 0)),
                      pl.BlockSpec(memory_space=pl.ANY)],
            out_specs=pl.BlockSpec((tm, D_MODEL), lambda i, pos: (i, 0)),
            scratch_shapes=[pltpu.VMEM((2, 2, tm, D_MODEL), F32), pltpu.SemaphoreType.DMA((2,))],
        ),
        out_shape=jax.ShapeDtypeStruct((n, D_MODEL), F32),
        compiler_params=_params(),
        name="combine",
    )(pos, h_all, meta_all, gf, ys)


def _routing_tables(counts, n_tok):
    n_tiles = (2 * n_tok) // TR + N_EXPERTS
    cnt = counts[ROUTER_LANE0:ROUTER_LANE0 + N_EXPERTS].astype(jnp.int32)
    tiles_e = (cnt + TR - 1) // TR
    tile_end = jnp.cumsum(tiles_e)
    tile_start = tile_end - tiles_e
    n_used = tile_end[-1]
    off_lanes = jnp.pad((tile_start * TR).astype(F32), (ROUTER_LANE0, LANES - ROUTER_LANE0 - N_EXPERTS))
    tile = jnp.arange(n_tiles, dtype=jnp.int32)
    used = tile < n_used
    blk = jnp.minimum(tile, n_used - 1)
    te = jnp.sum((blk[:, None] >= tile_end[None, :]).astype(jnp.int32), axis=1)
    owner = te[:, None] == jnp.arange(N_EXPERTS, dtype=jnp.int32)[None, :]
    rows_left = jnp.sum(jnp.where(owner, cnt[None, :] - (tile[:, None] - tile_start[None, :]) * TR, 0), axis=1)
    nv = jnp.where(used, jnp.clip(rows_left, 0, TR), 0).astype(jnp.int32)
    return off_lanes.reshape(1, LANES), te.astype(jnp.int32), nv, blk.astype(jnp.int32)


def kernel(x_prompt, x_sample, cache_win_k, cache_win_v, state_conv, norm1_g, w_in, attn_sink, conv_dw_w, conv_dw_b,
           conv_ln_g, conv_ln_b, conv_pw_w, conv_pw_b, w_out, norm2_g, router_group_w, router_group_b,
           router_expert_w, router_expert_b, expert_w_gate, expert_w_up, expert_w_down, final_norm_g):
    depth = w_in.shape[0]
    assert depth == 1, "single-layer step"
    bp, sp, _ = x_prompt.shape
    assert bp == 1, "one prompt sequence"
    n_seq, t_new, _ = x_sample.shape
    n_p, n_s = bp * sp, n_seq * t_new
    n_tok = n_p + n_s
    l = 0

    row = lambda a: a.reshape(1, -1)
    w_in_bf = w_in[l].astype(BF16)
    w_out_bf = w_out[l].astype(BF16)
    pw_bf = conv_pw_w[l].astype(BF16)
    rw = jnp.concatenate([router_group_w[l], router_expert_w[l]], axis=1)
    rw_bf = jnp.pad(rw, ((0, 0), (0, LANES - rw.shape[1]))).astype(BF16)
    rb = jnp.pad(jnp.concatenate([router_group_b[l], router_expert_b[l]]), (0, LANES - rw.shape[1])).reshape(1, LANES)
    g1, g2, gf = row(norm1_g[l]), row(norm2_g[l]), row(final_norm_g)
    sink = attn_sink[l]
    conv_w = (conv_dw_w[l], row(conv_dw_b[l]), row(conv_ln_g[l]), row(conv_ln_b[l]), pw_bf, row(conv_pw_b[l]))

    xp = x_prompt.reshape(n_p, D_MODEL)
    xs = x_sample.reshape(n_s, D_MODEL)
    ck = cache_win_k[l].reshape(n_seq, WINDOW, KV_DIM)
    cv = cache_win_v[l].reshape(n_seq, WINDOW, KV_DIM)
    cs = state_conv[l]

    qp, kp, vp, up = _in_proj(xp, g1, w_in_bf)
    qs, ks, vs, us = _in_proj(xs, g1, w_in_bf)
    mix_p = _mixer_prompt(sink, qp, kp, vp, up, *conv_w)
    mix_s, wk_s, wv_s, wc_s = _mixer_sample(sink, qs, ks, vs, us, ck, cv, cs, *conv_w)

    h_all, meta_all, cnt = _out_router(mix_p, xp, mix_s, xs, w_out_bf, g2, rw_bf, rb)

    off_lanes, te, nv, blk = _routing_tables(cnt[0], n_tok)
    pos_lanes = _positions(meta_all, off_lanes)
    pos = jnp.concatenate([pos_lanes[:, 0], pos_lanes[:, 1]])
    hs = _dispatch(pos, nv, h_all)
    ys = _moe(te, nv, blk, hs, g2, expert_w_gate[l], expert_w_up[l], expert_w_down[l])
    y_p = _combine(pos, h_all, meta_all, ys, gf, 0, n_p)
    y_s = _combine(pos, h_all, meta_all, ys, gf, n_p, n_s)

    kv_shape = (depth, bp, WINDOW, N_KV_HEADS, HEAD_DIM)
    kv_shape_s = (depth, n_seq, WINDOW, N_KV_HEADS, HEAD_DIM)
    return (y_p.reshape(bp, sp, D_MODEL), y_s.reshape(n_seq, t_new, D_MODEL),
            kp[n_p - WINDOW:].reshape(kv_shape), vp[n_p - WINDOW:].reshape(kv_shape),
            up[n_p - (CONV_W - 1):].reshape(depth, bp, CONV_W - 1, D_CONV),
            wk_s.reshape(kv_shape_s), wv_s.reshape(kv_shape_s), wc_s.reshape(depth, n_seq, CONV_W - 1, D_CONV))
```

```python
import functools

import jax
import jax.numpy as jnp
from jax import lax
from jax.experimental import pallas as pl
from jax.experimental.pallas import tpu as pltpu

F32 = jnp.float32
BF16 = jnp.bfloat16

D_MODEL = 2048
HEAD_DIM = 64
N_HEADS = 16
N_KV_HEADS = 2
GQA_GROUP = 8
KV_DIM = N_KV_HEADS * HEAD_DIM
D_ATTN = N_HEADS * HEAD_DIM
D_CONV = D_MODEL - D_ATTN
WINDOW = 128
CONV_W = 31
D_IN = D_ATTN + 2 * KV_DIM + 2 * D_CONV
N_GROUPS = 4
EXPERTS_PER_GROUP = 8
N_EXPERTS = N_GROUPS * EXPERTS_PER_GROUP
D_EXPERT = 256
RMS_EPS = 1e-6
LN_EPS = 1e-5

LANES = 128
SUBLANES = 8
ROUTER_LANE0 = N_GROUPS
VMEM_LIMIT = 56 * 1024 * 1024

TM_PROJ = 512
TM_MIX = 256
SEQ_TILE = 8
TR = 256
TM_OUT = 256

_NT = (((1,), (1,)), ((), ()))


def _params(n_axes=1):
    return pltpu.CompilerParams(dimension_semantics=("arbitrary",) * n_axes, vmem_limit_bytes=VMEM_LIMIT)


def _resident(shape):
    return pl.BlockSpec(shape, lambda *_: (0,) * len(shape), pipeline_mode=pl.Buffered(1))


def _rmsnorm(x, g):
    ms = jnp.mean(x * x, axis=-1, keepdims=True)
    return x * lax.rsqrt(ms + RMS_EPS) * g


def _dot(a, b):
    return jnp.dot(a, b, preferred_element_type=F32)


def _in_proj_body(x_ref, g_ref, w_ref, q_ref, k_ref, v_ref, u_ref):
    xn = _rmsnorm(x_ref[...], g_ref[...]).astype(BF16)
    q_ref[...] = (_dot(xn, w_ref[:, :D_ATTN]) * (HEAD_DIM ** -0.5)).astype(BF16)
    kv = _dot(xn, w_ref[:, D_ATTN:D_ATTN + 2 * KV_DIM])
    k_ref[...] = kv[:, :KV_DIM]
    v_ref[...] = kv[:, KV_DIM:]
    c0 = D_ATTN + 2 * KV_DIM
    cw = 256
    for j in range(D_CONV // cw):
        a = _dot(xn, w_ref[:, c0 + j * cw:c0 + (j + 1) * cw])
        b = _dot(xn, w_ref[:, c0 + D_CONV + j * cw:c0 + D_CONV + (j + 1) * cw])
        u_ref[:, j * cw:(j + 1) * cw] = a * jax.nn.sigmoid(b)


def _in_proj(x, g, w_bf):
    n = x.shape[0]
    tm = min(TM_PROJ, n)
    row = lambda w: pl.BlockSpec((tm, w), lambda i: (i, 0))
    return pl.pallas_call(
        _in_proj_body,
        grid=(n // tm,),
        in_specs=[row(D_MODEL), _resident((1, D_MODEL)), _resident((D_MODEL, D_IN))],
        out_specs=[row(D_ATTN), row(KV_DIM), row(KV_DIM), row(D_CONV)],
        out_shape=[jax.ShapeDtypeStruct((n, D_ATTN), BF16), jax.ShapeDtypeStruct((n, KV_DIM), F32),
                   jax.ShapeDtypeStruct((n, KV_DIM), F32), jax.ShapeDtypeStruct((n, D_CONV), F32)],
        compiler_params=_params(),
        name="in_proj",
    )(x, g, w_bf)


def _attend(q, parts, sink_ref, kv_head):
    t = q.shape[0]
    qs = jnp.concatenate([q[:, g * HEAD_DIM:(g + 1) * HEAD_DIM] for g in range(GQA_GROUP)], axis=0)
    scores = [lax.dot_general(qs, kk, _NT, preferred_element_type=F32) for kk, _, _ in parts]
    ps = [[] for _ in parts]
    inv = []
    for g in range(GQA_GROUP):
        sg = [jnp.where(mask, s[g * t:(g + 1) * t], -jnp.inf) for s, (_, _, mask) in zip(scores, parts)]
        sk = sink_ref[kv_head * GQA_GROUP + g]
        m = sk
        for x in sg:
            m = jnp.maximum(jnp.max(x, axis=-1, keepdims=True), m)
        den = jnp.exp(sk - m)
        for k, x in enumerate(sg):
            p = jnp.exp(x - m)
            den = den + jnp.sum(p, axis=-1, keepdims=True)
            ps[k].append(p.astype(BF16))
        inv.append(1.0 / den)
    o = None
    for k, (_, vv, _) in enumerate(parts):
        ok = _dot(jnp.concatenate(ps[k], axis=0), vv)
        o = ok if o is None else o + ok
    return jnp.concatenate([o[g * t:(g + 1) * t] * inv[g] for g in range(GQA_GROUP)], axis=1)


def _conv_tail(y, dwb_ref, lng_ref, lnb_ref, pw_ref, pwb_ref):
    y = y + dwb_ref[...]
    mu = jnp.mean(y, axis=-1, keepdims=True)
    yc = y - mu
    yn = yc * lax.rsqrt(jnp.mean(yc * yc, axis=-1, keepdims=True) + LN_EPS)
    yn = yn * lng_ref[...] + lnb_ref[...]
    act = yn * jax.nn.sigmoid(yn)
    return _dot(act.astype(BF16), pw_ref[...]) + pwb_ref[...]


CONV_PAD = 32
CONV_ROWS = 32
CONV_COLS = 256


def _mixer_prompt_body(sink_ref, q_ref, k_ref, v_ref, u_ref, dww_ref, dwb_ref, lng_ref, lnb_ref, pw_ref, pwb_ref,
                       mix_ref, kprev, vprev, uext, ushift, ybuf):
    i = pl.program_id(0)
    tm = q_ref.shape[0]

    @pl.when(i == 0)
    def _():
        kprev[...] = jnp.zeros_like(kprev)
        vprev[...] = jnp.zeros_like(vprev)
        uext[0:CONV_PAD, :] = jnp.zeros((CONV_PAD, D_CONV), F32)

    qi = lax.broadcasted_iota(jnp.int32, (WINDOW, 2 * WINDOW), 0)
    kj = lax.broadcasted_iota(jnp.int32, (WINDOW, 2 * WINDOW), 1)
    diff = qi + WINDOW - kj
    band = (diff >= 0) & (diff <= WINDOW)
    kp, vp = kprev[...], vprev[...]
    for b in range(tm // WINDOW):
        rows = slice(b * WINDOW, (b + 1) * WINDOW)
        kb = k_ref[rows, :].astype(BF16)
        vb = v_ref[rows, :].astype(BF16)
        kk = jnp.concatenate([kp, kb], axis=0)
        vv = jnp.concatenate([vp, vb], axis=0)
        if b == 0:
            mask = band & (kj >= jnp.where(i > 0, 0, WINDOW))
        else:
            mask = band
        for h in range(N_KV_HEADS):
            cols = slice(h * HEAD_DIM, (h + 1) * HEAD_DIM)
            hq = slice(h * GQA_GROUP * HEAD_DIM, (h + 1) * GQA_GROUP * HEAD_DIM)
            o = _attend(q_ref[rows, hq], [(kk[:, cols], vv[:, cols], mask)], sink_ref, h)
            mix_ref[rows, hq] = o.astype(BF16)
        kp, vp = kb, vb
    kprev[...] = kp
    vprev[...] = vp

    uext[CONV_PAD:CONV_PAD + tm, :] = u_ref[...]
    n_shift_rows = ushift.shape[1]
    for s in range(1, SUBLANES):
        ushift[s - 1] = uext[s:s + n_shift_rows, :]
    off = CONV_PAD - (CONV_W - 1)
    for r in range(tm // CONV_ROWS):
        for c in range(D_CONV // CONV_COLS):
            cs = slice(c * CONV_COLS, (c + 1) * CONV_COLS)
            acc = jnp.zeros((CONV_ROWS, CONV_COLS), F32)
            for j in range(CONV_W):
                a, s = divmod(off + j, SUBLANES)
                r0 = r * CONV_ROWS + a * SUBLANES
                src = uext[r0:r0 + CONV_ROWS, cs] if s == 0 else ushift[s - 1, r0:r0 + CONV_ROWS, cs]
                acc = acc + dww_ref[j:j + 1, cs] * src
            ybuf[r * CONV_ROWS:(r + 1) * CONV_ROWS, cs] = acc
    uext[0:CONV_PAD, :] = uext[tm:tm + CONV_PAD, :]
    conv = _conv_tail(ybuf[...], dwb_ref, lng_ref, lnb_ref, pw_ref, pwb_ref)
    mix_ref[:, D_ATTN:] = conv.astype(BF16)


def _mixer_prompt(sink, q, k, v, u, dww, dwb, lng, lnb, pw_bf, pwb):
    n = q.shape[0]
    tm = TM_MIX
    row = lambda w: pl.BlockSpec((tm, w), lambda i, s: (i, 0))
    res = lambda shape: pl.BlockSpec(shape, lambda i, s: (0,) * len(shape), pipeline_mode=pl.Buffered(1))
    return pl.pallas_call(
        _mixer_prompt_body,
        grid_spec=pltpu.PrefetchScalarGridSpec(
            num_scalar_prefetch=1,
            grid=(n // tm,),
            in_specs=[row(D_ATTN), row(KV_DIM), row(KV_DIM), row(D_CONV), res((CONV_W, D_CONV)), res((1, D_CONV)),
                      res((1, D_CONV)), res((1, D_CONV)), res((D_CONV, D_CONV)), res((1, D_CONV))],
            out_specs=row(D_MODEL),
            scratch_shapes=[pltpu.VMEM((WINDOW, KV_DIM), BF16), pltpu.VMEM((WINDOW, KV_DIM), BF16),
                            pltpu.VMEM((CONV_PAD + tm, D_CONV), F32),
                            pltpu.VMEM((SUBLANES - 1, CONV_PAD - SUBLANES + tm, D_CONV), F32),
                            pltpu.VMEM((tm, D_CONV), F32)],
        ),
        out_shape=jax.ShapeDtypeStruct((n, D_MODEL), BF16),
        compiler_params=_params(),
        name="mixer_prompt",
    )(sink, q, k, v, u, dww, dwb, lng, lnb, pw_bf, pwb)


UEXT_ROWS = 40


def _mixer_sample_body(sink_ref, q_ref, k_ref, v_ref, u_ref, ck_ref, cv_ref, cs_ref, dww_ref, dwb_ref, lng_ref,
                       lnb_ref, pw_ref, pwb_ref, mix_ref, wk_ref, wv_ref, wc_ref, uext, wsh, ybuf):
    i = pl.program_id(0)
    nt = q_ref.shape[0]
    sb = ck_ref.shape[0]
    t_new = nt // sb
    n_cache = sb * WINDOW

    @pl.when(i == 0)
    def _():
        uext[...] = jnp.zeros_like(uext)
        wsh[...] = jnp.zeros_like(wsh)
        for t in range(t_new):
            wsh[t, t:t + CONV_W, :] = dww_ref[...]

    knew = k_ref[...]
    vnew = v_ref[...]
    kc = ck_ref[...].reshape(n_cache, KV_DIM).astype(BF16)
    vc = cv_ref[...].reshape(n_cache, KV_DIM).astype(BF16)
    zpad = jnp.zeros((LANES - nt, KV_DIM), F32)
    kn = jnp.concatenate([knew, zpad], axis=0).astype(BF16)
    vn = jnp.concatenate([vnew, zpad], axis=0).astype(BF16)
    lt, lw = t_new.bit_length() - 1, WINDOW.bit_length() - 1
    qr = lax.broadcasted_iota(jnp.int32, (nt, n_cache), 0)
    cc = lax.broadcasted_iota(jnp.int32, (nt, n_cache), 1)
    mask_c = ((cc >> lw) == (qr >> lt)) & ((cc & (WINDOW - 1)) >= (qr & (t_new - 1)))
    qr = lax.broadcasted_iota(jnp.int32, (nt, LANES), 0)
    cn = lax.broadcasted_iota(jnp.int32, (nt, LANES), 1)
    mask_n = ((cn >> lt) == (qr >> lt)) & ((cn & (t_new - 1)) <= (qr & (t_new - 1)))
    for h in range(N_KV_HEADS):
        cols = slice(h * HEAD_DIM, (h + 1) * HEAD_DIM)
        hq = slice(h * GQA_GROUP * HEAD_DIM, (h + 1) * GQA_GROUP * HEAD_DIM)
        parts = [(kc[:, cols], vc[:, cols], mask_c), (kn[:, cols], vn[:, cols], mask_n)]
        o = _attend(q_ref[:, hq], parts, sink_ref, h)
        mix_ref[:, hq] = o.astype(BF16)

    n_state = CONV_W - 1
    for s in range(sb):
        new = slice(s * t_new, (s + 1) * t_new)
        wk_ref[s, 0:WINDOW - t_new, :] = ck_ref[s, t_new:WINDOW, :]
        wk_ref[s, WINDOW - t_new:WINDOW, :] = knew[new]
        wv_ref[s, 0:WINDOW - t_new, :] = cv_ref[s, t_new:WINDOW, :]
        wv_ref[s, WINDOW - t_new:WINDOW, :] = vnew[new]
        uext[s, 0:n_state, :] = cs_ref[s]
        uext[s, n_state:n_state + t_new, :] = u_ref[new, :]
        wc_ref[s] = uext[s, t_new:t_new + n_state, :]
        ue = uext[s]
        for t in range(t_new):
            ybuf[s * t_new + t:s * t_new + t + 1, :] = jnp.sum(wsh[t] * ue, axis=0, keepdims=True)
    conv = _conv_tail(ybuf[...], dwb_ref, lng_ref, lnb_ref, pw_ref, pwb_ref)
    mix_ref[:, D_ATTN:] = conv.astype(BF16)


def _mixer_sample(sink, q, k, v, u, ck, cv, cs, dww, dwb, lng, lnb, pw_bf, pwb):
    n = q.shape[0]
    n_seq = ck.shape[0]
    t_new = n // n_seq
    sb = SEQ_TILE
    nt = sb * t_new
    n_state = CONV_W - 1
    row = lambda w: pl.BlockSpec((nt, w), lambda i, s: (i, 0))
    seq = lambda r, w: pl.BlockSpec((sb, r, w), lambda i, s: (i, 0, 0))
    state = pl.BlockSpec((None, sb, n_state, D_CONV), lambda i, s: (0, i, 0, 0))
    res = lambda shape: pl.BlockSpec(shape, lambda i, s: (0,) * len(shape), pipeline_mode=pl.Buffered(1))
    return pl.pallas_call(
        _mixer_sample_body,
        grid_spec=pltpu.PrefetchScalarGridSpec(
            num_scalar_prefetch=1,
            grid=(n_seq // sb,),
            in_specs=[row(D_ATTN), row(KV_DIM), row(KV_DIM), row(D_CONV), seq(WINDOW, KV_DIM), seq(WINDOW, KV_DIM),
                      state, res((CONV_W, D_CONV)), res((1, D_CONV)), res((1, D_CONV)),
                      res((1, D_CONV)), res((D_CONV, D_CONV)), res((1, D_CONV))],
            out_specs=[row(D_MODEL), seq(WINDOW, KV_DIM), seq(WINDOW, KV_DIM), state],
            scratch_shapes=[pltpu.VMEM((sb, UEXT_ROWS, D_CONV), F32), pltpu.VMEM((t_new, UEXT_ROWS, D_CONV), F32),
                            pltpu.VMEM((nt, D_CONV), F32)],
        ),
        out_shape=[jax.ShapeDtypeStruct((n, D_MODEL), BF16), jax.ShapeDtypeStruct(ck.shape, F32),
                   jax.ShapeDtypeStruct(cv.shape, F32), jax.ShapeDtypeStruct(cs.shape, F32)],
        compiler_params=_params(),
        name="mixer_sample",
    )(sink, q, k, v, u, ck, cv, cs, dww, dwb, lng, lnb, pw_bf, pwb)


META_E0, META_E1, META_G0, META_G1, META_R0, META_R1 = range(6)


def _out_router_body(mixp_ref, xp_ref, mixs_ref, xs_ref, wout_ref, g2_ref, rw_ref, rb_ref, h_ref, meta_ref, cnt_ref,
                     carry, *, prompt_tiles):
    i = pl.program_id(0)
    tm = xp_ref.shape[0]

    @pl.when(i == 0)
    def _():
        carry[...] = jnp.zeros_like(carry)

    is_prompt = i < prompt_tiles
    x = jnp.where(is_prompt, xp_ref[...], xs_ref[...])
    mix = jnp.where(is_prompt, mixp_ref[...], mixs_ref[...])
    h = x + _dot(mix, wout_ref[...])
    h_ref[...] = h
    xn = _rmsnorm(h, g2_ref[...]).astype(BF16)
    logits = _dot(xn, rw_ref[...]) + rb_ref[...]
    lane = lax.broadcasted_iota(jnp.int32, logits.shape, 1).astype(F32)
    first = lambda cond: jnp.min(jnp.where(cond, lane, float(LANES)), axis=-1, keepdims=True)

    gl = jnp.where(lane < N_GROUPS, logits, -jnp.inf)
    gmax = jnp.max(gl, axis=-1, keepdims=True)
    gidx = first(gl == gmax)
    gval = 1.0 / jnp.sum(jnp.exp(gl - gmax), axis=-1, keepdims=True)

    lo = ROUTER_LANE0 + EXPERTS_PER_GROUP * gidx
    el = jnp.where((lane >= lo) & (lane < lo + EXPERTS_PER_GROUP), logits, -jnp.inf)
    m1 = jnp.max(el, axis=-1, keepdims=True)
    i1 = first(el == m1)
    el2 = jnp.where(lane == i1, -jnp.inf, el)
    m2 = jnp.max(el2, axis=-1, keepdims=True)
    i2 = first(el2 == m2)
    r = jnp.exp(m2 - m1)
    g0 = gval * (1.0 / (1.0 + r))
    g1 = gval * (r / (1.0 + r))

    sel0, sel1 = lane == i1, lane == i2
    onehot = jnp.where(sel0 | sel1, 1.0, 0.0)
    ti = lax.broadcasted_iota(jnp.int32, (tm, tm), 0)
    tj = lax.broadcasted_iota(jnp.int32, (tm, tm), 1)
    lower = jnp.where(tj < ti, 1.0, 0.0).astype(BF16)
    before = _dot(lower, onehot.astype(BF16)) + carry[0:1, :]
    r0 = jnp.sum(jnp.where(sel0, before, 0.0), axis=-1, keepdims=True)
    r1 = jnp.sum(jnp.where(sel1, before, 0.0), axis=-1, keepdims=True)
    carry[...] = carry[...] + jnp.sum(onehot, axis=0, keepdims=True)
    cnt_ref[...] = carry[...]

    lane_i = lax.broadcasted_iota(jnp.int32, logits.shape, 1)
    meta = jnp.zeros(logits.shape, F32)
    for slot, val in ((META_E0, i1 - ROUTER_LANE0), (META_E1, i2 - ROUTER_LANE0), (META_G0, g0), (META_G1, g1),
                      (META_R0, r0), (META_R1, r1)):
        meta = jnp.where(lane_i == slot, val, meta)
    meta_ref[...] = meta


def _out_router(mix_p, x_p, mix_s, x_s, wout_bf, g2, rw_bf, rb):
    n_p, n_s = x_p.shape[0], x_s.shape[0]
    tm = TM_PROJ
    assert n_p % tm == 0 and n_s % tm == 0
    tp, ts = n_p // tm, n_s // tm
    n = n_p + n_s
    prow = lambda w: pl.BlockSpec((tm, w), lambda i: (jnp.minimum(i, tp - 1), 0))
    srow = lambda w: pl.BlockSpec((tm, w), lambda i: (jnp.maximum(i - tp, 0), 0))
    row = lambda w: pl.BlockSpec((tm, w), lambda i: (i, 0))
    return pl.pallas_call(
        functools.partial(_out_router_body, prompt_tiles=tp),
        grid=(tp + ts,),
        in_specs=[prow(D_MODEL), prow(D_MODEL), srow(D_MODEL), srow(D_MODEL), _resident((D_MODEL, D_MODEL)),
                  _resident((1, D_MODEL)), _resident((D_MODEL, LANES)), _resident((1, LANES))],
        out_specs=[row(D_MODEL), row(LANES), pl.BlockSpec((8, LANES), lambda i: (0, 0))],
        out_shape=[jax.ShapeDtypeStruct((n, D_MODEL), F32), jax.ShapeDtypeStruct((n, LANES), F32),
                   jax.ShapeDtypeStruct((8, LANES), F32)],
        scratch_shapes=[pltpu.VMEM((8, LANES), F32)],
        compiler_params=_params(),
        name="out_router",
    )(mix_p, x_p, mix_s, x_s, wout_bf, g2, rw_bf, rb)


def _positions_body(meta_ref, off_ref, pos_ref):
    meta = meta_ref[...]
    lane = lax.broadcasted_iota(jnp.int32, meta.shape, 1)
    lane_f = lane.astype(F32)
    off = off_ref[...]
    pos = jnp.zeros(meta.shape, F32)
    for slot, (e_lane, r_lane) in enumerate(((META_E0, META_R0), (META_E1, META_R1))):
        e = meta[:, e_lane:e_lane + 1] + ROUTER_LANE0
        seg = jnp.sum(jnp.where(lane_f == e, off, 0.0), axis=-1, keepdims=True)
        pos = jnp.where(lane == slot, seg + meta[:, r_lane:r_lane + 1], pos)
    pos_ref[...] = pos.astype(jnp.int32)


def _positions(meta_all, off_lanes):
    n = meta_all.shape[0]
    tm = TM_PROJ
    assert n % tm == 0
    return pl.pallas_call(
        _positions_body,
        grid=(n // tm,),
        in_specs=[pl.BlockSpec((tm, LANES), lambda i: (i, 0)), _resident((1, LANES))],
        out_specs=pl.BlockSpec((tm, LANES), lambda i: (i, 0)),
        out_shape=jax.ShapeDtypeStruct((n, LANES), jnp.int32),
        compiler_params=_params(),
        name="positions",
    )(meta_all, off_lanes)


N_LOAD_SLOTS = 3


def _pow2_sizes(n):
    return [1 << b for b in range(n.bit_length() - 1, -1, -1)]


def _dispatch_body(pos_ref, nv_ref, h_ref, hs_ref, hbuf, zbuf, lsem, ssem, zsem, *, n_tok, tm):
    i = pl.program_id(0)
    n_steps = pl.num_programs(0)
    n_tiles = nv_ref.shape[0]

    groups = tm // SUBLANES

    def load(t, slot, sem=lsem):
        return pltpu.make_async_copy(h_ref.at[pl.ds(t * groups, groups)], hbuf.at[slot], sem.at[slot])

    def clear_unowned(start):
        def per_tile(t, c):
            nv = nv_ref[t]
            z = TR - nv
            row = t * TR + nv
            head = z & (SUBLANES - 1)
            for k in range(SUBLANES - 1):
                @pl.when(k < head)
                def _(k=k):
                    cp = pltpu.make_async_copy(zbuf.at[pl.ds(0, 1), :], hs_ref.at[pl.ds(row + k, 1), :], zsem)
                    cp.start() if start else cp.wait()
            row = row + head
            for size in _pow2_sizes(TR):
                if size < SUBLANES:
                    break
                @pl.when((z & size) != 0)
                def _(row=row, size=size):
                    dst = hs_ref.at[pl.ds(pl.multiple_of(row, SUBLANES), size), :]
                    cp = pltpu.make_async_copy(zbuf.at[pl.ds(0, size), :], dst, zsem)
                    cp.start() if start else cp.wait()
                row = row + (z & size)
            return c
        lax.fori_loop(0, n_tiles, per_tile, 0)

    def scatter_wait(slot):
        for _ in range(2):
            load(0, slot, ssem).wait()

    @pl.when(i == 0)
    def _():
        zbuf[...] = jnp.zeros_like(zbuf)
        clear_unowned(True)
        load(0, 0).start()

    @pl.when(i + 1 < n_steps)
    def _():
        load(i + 1, (i + 1) % N_LOAD_SLOTS).start()

    slot = i % N_LOAD_SLOTS
    load(i, slot).wait()

    def issue(jb, c):
        for u in range(SUBLANES):
            for s in range(2):
                p = pos_ref[s * n_tok + i * tm + jb * SUBLANES + u]
                pltpu.make_async_copy(hbuf.at[slot, jb, pl.ds(u, 1), :], hs_ref.at[pl.ds(p, 1), :],
                                      ssem.at[slot]).start()
        return c
    lax.fori_loop(0, groups, issue, 0)

    @pl.when(i > 0)
    def _():
        scatter_wait((i + N_LOAD_SLOTS - 1) % N_LOAD_SLOTS)

    @pl.when(i == n_steps - 1)
    def _():
        scatter_wait(slot)
        clear_unowned(False)


def _dispatch(pos, nv, h_all):
    n_tok = h_all.shape[0]
    n_tiles = nv.shape[0]
    tm = TM_PROJ
    assert n_tok % tm == 0
    return pl.pallas_call(
        functools.partial(_dispatch_body, n_tok=n_tok, tm=tm),
        grid_spec=pltpu.PrefetchScalarGridSpec(
            num_scalar_prefetch=2,
            grid=(n_tok // tm,),
            in_specs=[pl.BlockSpec(memory_space=pl.ANY)],
            out_specs=pl.BlockSpec(memory_space=pl.ANY),
            scratch_shapes=[pltpu.VMEM((N_LOAD_SLOTS, tm // SUBLANES, SUBLANES, D_MODEL), F32),
                            pltpu.VMEM((TR, D_MODEL), F32),
                            pltpu.SemaphoreType.DMA((N_LOAD_SLOTS,)), pltpu.SemaphoreType.DMA((N_LOAD_SLOTS,)),
                            pltpu.SemaphoreType.DMA(())],
        ),
        out_shape=jax.ShapeDtypeStruct((n_tiles * TR, D_MODEL), F32),
        compiler_params=_params(),
        name="dispatch",
    )(pos, nv, h_all.reshape(n_tok // SUBLANES, SUBLANES, D_MODEL))


def _moe_body(ts_ref, hs_ref, g2_ref, wg_ref, wu_ref, wd_ref, ys_ref, xbuf, obuf, wg_bf, wu_bf, wd_bf, lsem, ssem):
    e = pl.program_id(0)
    n_tiles = ys_ref.shape[0] // TR
    n_used = ts_ref[N_EXPERTS]
    t0, t1 = ts_ref[e], ts_ref[e + 1]

    def load(g, slot):
        return pltpu.make_async_copy(hs_ref.at[pl.ds(pl.multiple_of(g * TR, TR), TR), :], xbuf.at[slot], lsem.at[slot])

    def store(g, slot):
        return pltpu.make_async_copy(obuf.at[slot], ys_ref.at[pl.ds(pl.multiple_of(g * TR, TR), TR), :], ssem.at[slot])

    @pl.when(e == 0)
    def _():
        load(0, 0).start()

    @pl.when(t1 > t0)
    def _():
        wg_bf[...] = wg_ref[0].astype(BF16)
        wu_bf[...] = wu_ref[0].astype(BF16)
        wd_bf[...] = wd_ref[0].astype(BF16)

    def tile(g, c):
        slot = g % 2

        @pl.when(g + 1 < n_used)
        def _():
            load(g + 1, 1 - slot).start()

        load(g, slot).wait()
        xn = _rmsnorm(xbuf[slot], g2_ref[...]).astype(BF16)
        hg = _dot(xn, wg_bf[...])
        hu = _dot(xn, wu_bf[...])
        hid = (hg * jax.nn.sigmoid(hg) * hu).astype(BF16)
        y = _dot(hid, wd_bf[...])

        @pl.when(g >= 2)
        def _():
            store(g - 2, slot).wait()

        obuf[slot] = y
        store(g, slot).start()
        return c

    lax.fori_loop(t0, t1, tile, 0)

    @pl.when(e == pl.num_programs(0) - 1)
    def _():
        @pl.when(n_used >= 2)
        def _():
            store(n_used - 2, n_used % 2).wait()
        store(n_used - 1, (n_used - 1) % 2).wait()
        obuf[0] = jnp.zeros((TR, D_MODEL), F32)

        def clear(g, c):
            store(g, 0).start()
            return c
        lax.fori_loop(n_used, n_tiles, clear, 0)

        def drain(g, c):
            store(g, 0).wait()
            return c
        lax.fori_loop(n_used, n_tiles, drain, 0)


def _moe(ts, hs, g2, wg, wu, wd):
    wspec = lambda shape: pl.BlockSpec((1,) + shape, lambda e, ts: (e, 0, 0))
    return pl.pallas_call(
        _moe_body,
        grid_spec=pltpu.PrefetchScalarGridSpec(
            num_scalar_prefetch=1,
            grid=(N_EXPERTS,),
            in_specs=[pl.BlockSpec(memory_space=pl.ANY),
                      pl.BlockSpec((1, D_MODEL), lambda e, ts: (0, 0)),
                      wspec((D_MODEL, D_EXPERT)), wspec((D_MODEL, D_EXPERT)), wspec((D_EXPERT, D_MODEL))],
            out_specs=pl.BlockSpec(memory_space=pl.ANY),
            scratch_shapes=[pltpu.VMEM((2, TR, D_MODEL), F32), pltpu.VMEM((2, TR, D_MODEL), F32),
                            pltpu.VMEM((D_MODEL, D_EXPERT), BF16), pltpu.VMEM((D_MODEL, D_EXPERT), BF16),
                            pltpu.VMEM((D_EXPERT, D_MODEL), BF16),
                            pltpu.SemaphoreType.DMA((2,)), pltpu.SemaphoreType.DMA((2,))],
        ),
        out_shape=jax.ShapeDtypeStruct(hs.shape, F32),
        compiler_params=_params(),
        name="moe",
    )(ts, hs, g2, wg, wu, wd)


def _combine_body(pos_ref, h_ref, meta_ref, g_ref, ys_ref, ys_grouped_ref, o_ref, ybuf, sem, *, n_tok, row0):
    i = pl.program_id(0)
    n_steps = pl.num_programs(0)
    tm = h_ref.shape[0]

    groups = tm // SUBLANES

    def gather(t, slot):
        def issue(jb, c):
            for u in range(SUBLANES):
                for s in range(2):
                    p = pos_ref[s * n_tok + row0 + t * tm + jb * SUBLANES + u]
                    pltpu.make_async_copy(ys_ref.at[pl.ds(p, 1), :], ybuf.at[slot, s, jb, pl.ds(u, 1), :],
                                          sem.at[slot]).start()
            return c
        lax.fori_loop(0, groups, issue, 0)

    @pl.when(i == 0)
    def _():
        gather(0, 0)

    @pl.when(i + 1 < n_steps)
    def _():
        gather(i + 1, (i + 1) % 2)

    slot = i % 2
    for s in range(2):
        pltpu.make_async_copy(ys_grouped_ref.at[pl.ds(0, groups)], ybuf.at[slot, s], sem.at[slot]).wait()
    meta = meta_ref[...]
    g0 = meta[:, META_G0:META_G0 + 1]
    g1 = meta[:, META_G1:META_G1 + 1]
    y0 = ybuf[slot, 0].reshape(tm, D_MODEL)
    y1 = ybuf[slot, 1].reshape(tm, D_MODEL)
    h = h_ref[...] + (g0 * y0 + g1 * y1)
    o_ref[...] = _rmsnorm(h, g_ref[...])


def _combine(pos, h_all, meta_all, ys, gf, row0, n):
    n_tok = h_all.shape[0]
    tm = min(TM_OUT, n)
    assert row0 % tm == 0 and n % tm == 0
    b0 = row0 // tm
    return pl.pallas_call(
        functools.partial(_combine_body, n_tok=n_tok, row0=row0),
        grid_spec=pltpu.PrefetchScalarGridSpec(
            num_scalar_prefetch=1,
            grid=(n // tm,),
            in_specs=[pl.BlockSpec((tm, D_MODEL), lambda i, pos: (b0 + i, 0)),
                      pl.BlockSpec((tm, LANES), lambda i, pos: (b0 + i, 0)),
                      pl.BlockSpec((1, D_MODEL), lambda i, pos: (0, 0)),
                      pl.BlockSpec(memory_space=pl.ANY), pl.BlockSpec(memory_space=pl.ANY)],
            out_specs=pl.BlockSpec((tm, D_MODEL), lambda i, pos: (i, 0)),
            scratch_shapes=[pltpu.VMEM((2, 2, tm // SUBLANES, SUBLANES, D_MODEL), F32),
                            pltpu.SemaphoreType.DMA((2,))],
        ),
        out_shape=jax.ShapeDtypeStruct((n, D_MODEL), F32),
        compiler_params=_params(),
        name="combine",
    )(pos, h_all, meta_all, gf, ys, ys.reshape(ys.shape[0] // SUBLANES, SUBLANES, D_MODEL))


def _routing_tables(counts, n_tok):
    n_tiles = (2 * n_tok) // TR + N_EXPERTS
    cnt = counts[ROUTER_LANE0:ROUTER_LANE0 + N_EXPERTS].astype(jnp.int32)
    tiles_e = (cnt + TR - 1) // TR
    tile_end = jnp.cumsum(tiles_e)
    tile_start = tile_end - tiles_e
    off_lanes = jnp.pad((tile_start * TR).astype(F32), (ROUTER_LANE0, LANES - ROUTER_LANE0 - N_EXPERTS))
    ts = jnp.concatenate([tile_start, tile_end[-1:]]).astype(jnp.int32)
    tile = jnp.arange(n_tiles, dtype=jnp.int32)
    owner = (tile[:, None] >= tile_start[None, :]) & (tile[:, None] < tile_end[None, :])
    rows_left = jnp.sum(jnp.where(owner, cnt[None, :] - (tile[:, None] - tile_start[None, :]) * TR, 0), axis=1)
    nv = jnp.clip(rows_left, 0, TR).astype(jnp.int32)
    return off_lanes.reshape(1, LANES), ts, nv


def kernel(x_prompt, x_sample, cache_win_k, cache_win_v, state_conv, norm1_g, w_in, attn_sink, conv_dw_w, conv_dw_b,
           conv_ln_g, conv_ln_b, conv_pw_w, conv_pw_b, w_out, norm2_g, router_group_w, router_group_b,
           router_expert_w, router_expert_b, expert_w_gate, expert_w_up, expert_w_down, final_norm_g):
    depth = w_in.shape[0]
    assert depth == 1, "single-layer step"
    bp, sp, _ = x_prompt.shape
    assert bp == 1, "one prompt sequence"
    n_seq, t_new, _ = x_sample.shape
    n_p, n_s = bp * sp, n_seq * t_new
    n_tok = n_p + n_s
    l = 0

    row = lambda a: a.reshape(1, -1)
    w_in_bf = w_in[l].astype(BF16)
    w_out_bf = w_out[l].astype(BF16)
    pw_bf = conv_pw_w[l].astype(BF16)
    rw = jnp.concatenate([router_group_w[l], router_expert_w[l]], axis=1)
    rw_bf = jnp.pad(rw, ((0, 0), (0, LANES - rw.shape[1]))).astype(BF16)
    rb = jnp.pad(jnp.concatenate([router_group_b[l], router_expert_b[l]]), (0, LANES - rw.shape[1])).reshape(1, LANES)
    g1, g2, gf = row(norm1_g[l]), row(norm2_g[l]), row(final_norm_g)
    sink = attn_sink[l]
    conv_w = (conv_dw_w[l], row(conv_dw_b[l]), row(conv_ln_g[l]), row(conv_ln_b[l]), pw_bf, row(conv_pw_b[l]))

    xp = x_prompt.reshape(n_p, D_MODEL)
    xs = x_sample.reshape(n_s, D_MODEL)
    ck = cache_win_k[l].reshape(n_seq, WINDOW, KV_DIM)
    cv = cache_win_v[l].reshape(n_seq, WINDOW, KV_DIM)
    cs = state_conv

    qp, kp, vp, up = _in_proj(xp, g1, w_in_bf)
    qs, ks, vs, us = _in_proj(xs, g1, w_in_bf)
    mix_p = _mixer_prompt(sink, qp, kp, vp, up, *conv_w)
    mix_s, wk_s, wv_s, wc_s = _mixer_sample(sink, qs, ks, vs, us, ck, cv, cs, *conv_w)

    h_all, meta_all, cnt = _out_router(mix_p, xp, mix_s, xs, w_out_bf, g2, rw_bf, rb)

    off_lanes, ts, nv = _routing_tables(cnt[0], n_tok)
    pos_lanes = _positions(meta_all, off_lanes)
    pos = jnp.concatenate([pos_lanes[:, 0], pos_lanes[:, 1]])
    hs = _dispatch(pos, nv, h_all)
    ys = _moe(ts, hs, g2, expert_w_gate[l], expert_w_up[l], expert_w_down[l])
    y_p = _combine(pos, h_all, meta_all, ys, gf, 0, n_p)
    y_s = _combine(pos, h_all, meta_all, ys, gf, n_p, n_s)

    kv_shape = (depth, bp, WINDOW, N_KV_HEADS, HEAD_DIM)
    kv_shape_s = (depth, n_seq, WINDOW, N_KV_HEADS, HEAD_DIM)
    return (y_p.reshape(bp, sp, D_MODEL), y_s.reshape(n_seq, t_new, D_MODEL),
            kp[n_p - WINDOW:].reshape(kv_shape), vp[n_p - WINDOW:].reshape(kv_shape),
            up[n_p - (CONV_W - 1):].reshape(depth, bp, CONV_W - 1, D_CONV),
            wk_s.reshape(kv_shape_s), wv_s.reshape(kv_shape_s), wc_s)
```

```python
import functools

import jax
import jax.numpy as jnp
from jax import lax
from jax.experimental import pallas as pl
from jax.experimental.pallas import tpu as pltpu

F32 = jnp.float32
BF16 = jnp.bfloat16

D_MODEL = 2048
HEAD_DIM = 64
N_HEADS = 16
N_KV_HEADS = 2
GQA_GROUP = 8
KV_DIM = N_KV_HEADS * HEAD_DIM
D_ATTN = N_HEADS * HEAD_DIM
D_CONV = D_MODEL - D_ATTN
WINDOW = 128
CONV_W = 31
D_IN = D_ATTN + 2 * KV_DIM + 2 * D_CONV
N_GROUPS = 4
EXPERTS_PER_GROUP = 8
N_EXPERTS = N_GROUPS * EXPERTS_PER_GROUP
D_EXPERT = 256
RMS_EPS = 1e-6
LN_EPS = 1e-5

LANES = 128
SUBLANES = 8
ROUTER_LANE0 = N_GROUPS
VMEM_LIMIT = 56 * 1024 * 1024

TM_PROJ = 512
TM_MIX = 256
SEQ_TILE = 8
TR = 128
TM_OUT = 256

_NT = (((1,), (1,)), ((), ()))


def _params(n_axes=1):
    return pltpu.CompilerParams(dimension_semantics=("arbitrary",) * n_axes, vmem_limit_bytes=VMEM_LIMIT)


def _resident(shape):
    return pl.BlockSpec(shape, lambda *_: (0,) * len(shape), pipeline_mode=pl.Buffered(1))


def _rmsnorm(x, g):
    ms = jnp.mean(x * x, axis=-1, keepdims=True)
    return x * lax.rsqrt(ms + RMS_EPS) * g


def _dot(a, b):
    return jnp.dot(a, b, preferred_element_type=F32)


def _dot_nt(a, b):
    return lax.dot_general(a, b, _NT, preferred_element_type=F32)


def _in_proj_body(x_ref, g_ref, w_ref, q_ref, k_ref, v_ref, u_ref):
    xn = _rmsnorm(x_ref[...], g_ref[...]).astype(BF16)
    q_ref[...] = (_dot(xn, w_ref[:, :D_ATTN]) * (HEAD_DIM ** -0.5)).astype(BF16)
    kv = _dot(xn, w_ref[:, D_ATTN:D_ATTN + 2 * KV_DIM])
    k_ref[...] = kv[:, :KV_DIM]
    v_ref[...] = kv[:, KV_DIM:]
    c0 = D_ATTN + 2 * KV_DIM
    cw = 256
    for j in range(D_CONV // cw):
        a = _dot(xn, w_ref[:, c0 + j * cw:c0 + (j + 1) * cw])
        b = _dot(xn, w_ref[:, c0 + D_CONV + j * cw:c0 + D_CONV + (j + 1) * cw])
        u_ref[:, j * cw:(j + 1) * cw] = a * jax.nn.sigmoid(b)


def _in_proj(x, g, w_bf):
    n = x.shape[0]
    tm = min(TM_PROJ, n)
    row = lambda w: pl.BlockSpec((tm, w), lambda i: (i, 0))
    return pl.pallas_call(
        _in_proj_body,
        grid=(n // tm,),
        in_specs=[row(D_MODEL), _resident((1, D_MODEL)), _resident((D_MODEL, D_IN))],
        out_specs=[row(D_ATTN), row(KV_DIM), row(KV_DIM), row(D_CONV)],
        out_shape=[jax.ShapeDtypeStruct((n, D_ATTN), BF16), jax.ShapeDtypeStruct((n, KV_DIM), F32),
                   jax.ShapeDtypeStruct((n, KV_DIM), F32), jax.ShapeDtypeStruct((n, D_CONV), F32)],
        compiler_params=_params(),
        name="in_proj",
    )(x, g, w_bf)


def _attend(q, parts, sink_ref, kv_head, keys_on_lanes=False):
    t = q.shape[0]
    score, weigh = (_dot, _dot_nt) if keys_on_lanes else (_dot_nt, _dot)
    qs = jnp.concatenate([q[:, g * HEAD_DIM:(g + 1) * HEAD_DIM] for g in range(GQA_GROUP)], axis=0)
    scores = [score(qs, kk) for kk, _, _ in parts]
    ps = [[] for _ in parts]
    inv = []
    for g in range(GQA_GROUP):
        sg = [jnp.where(mask, s[g * t:(g + 1) * t], -jnp.inf) for s, (_, _, mask) in zip(scores, parts)]
        sk = sink_ref[kv_head * GQA_GROUP + g]
        m = sk
        for x in sg:
            m = jnp.maximum(jnp.max(x, axis=-1, keepdims=True), m)
        den = jnp.exp(sk - m)
        for k, x in enumerate(sg):
            p = jnp.exp(x - m)
            den = den + jnp.sum(p, axis=-1, keepdims=True)
            ps[k].append(p.astype(BF16))
        inv.append(1.0 / den)
    o = None
    for k, (_, vv, _) in enumerate(parts):
        ok = weigh(jnp.concatenate(ps[k], axis=0), vv)
        o = ok if o is None else o + ok
    return jnp.concatenate([o[g * t:(g + 1) * t] * inv[g] for g in range(GQA_GROUP)], axis=1)


def _conv_tail(y, dwb_ref, lng_ref, lnb_ref, pw_ref, pwb_ref):
    y = y + dwb_ref[...]
    mu = jnp.mean(y, axis=-1, keepdims=True)
    yc = y - mu
    yn = yc * lax.rsqrt(jnp.mean(yc * yc, axis=-1, keepdims=True) + LN_EPS)
    yn = yn * lng_ref[...] + lnb_ref[...]
    act = yn * jax.nn.sigmoid(yn)
    return _dot(act.astype(BF16), pw_ref[...]) + pwb_ref[...]


CONV_PAD = 32
CONV_ROWS = 32
CONV_COLS = 256


def _mixer_prompt_body(sink_ref, q_ref, k_ref, v_ref, u_ref, dww_ref, dwb_ref, lng_ref, lnb_ref, pw_ref, pwb_ref,
                       mix_ref, kprev, vprev, uext, ushift, ybuf):
    i = pl.program_id(0)
    tm = q_ref.shape[0]

    @pl.when(i == 0)
    def _():
        kprev[...] = jnp.zeros_like(kprev)
        vprev[...] = jnp.zeros_like(vprev)
        uext[0:CONV_PAD, :] = jnp.zeros((CONV_PAD, D_CONV), F32)

    qi = lax.broadcasted_iota(jnp.int32, (WINDOW, 2 * WINDOW), 0)
    kj = lax.broadcasted_iota(jnp.int32, (WINDOW, 2 * WINDOW), 1)
    diff = qi + WINDOW - kj
    band = (diff >= 0) & (diff <= WINDOW)
    kp, vp = kprev[...], vprev[...]
    for b in range(tm // WINDOW):
        rows = slice(b * WINDOW, (b + 1) * WINDOW)
        kb = k_ref[rows, :].astype(BF16)
        vb = v_ref[rows, :].astype(BF16)
        kk = jnp.concatenate([kp, kb], axis=0)
        vv = jnp.concatenate([vp, vb], axis=0)
        if b == 0:
            mask = band & (kj >= jnp.where(i > 0, 0, WINDOW))
        else:
            mask = band
        for h in range(N_KV_HEADS):
            cols = slice(h * HEAD_DIM, (h + 1) * HEAD_DIM)
            hq = slice(h * GQA_GROUP * HEAD_DIM, (h + 1) * GQA_GROUP * HEAD_DIM)
            o = _attend(q_ref[rows, hq], [(kk[:, cols], vv[:, cols], mask)], sink_ref, h)
            mix_ref[rows, hq] = o.astype(BF16)
        kp, vp = kb, vb
    kprev[...] = kp
    vprev[...] = vp

    uext[CONV_PAD:CONV_PAD + tm, :] = u_ref[...]
    n_shift_rows = ushift.shape[1]
    for s in range(1, SUBLANES):
        ushift[s - 1] = uext[s:s + n_shift_rows, :]
    off = CONV_PAD - (CONV_W - 1)
    for r in range(tm // CONV_ROWS):
        for c in range(D_CONV // CONV_COLS):
            cs = slice(c * CONV_COLS, (c + 1) * CONV_COLS)
            acc = jnp.zeros((CONV_ROWS, CONV_COLS), F32)
            for j in range(CONV_W):
                a, s = divmod(off + j, SUBLANES)
                r0 = r * CONV_ROWS + a * SUBLANES
                src = uext[r0:r0 + CONV_ROWS, cs] if s == 0 else ushift[s - 1, r0:r0 + CONV_ROWS, cs]
                acc = acc + dww_ref[j:j + 1, cs] * src
            ybuf[r * CONV_ROWS:(r + 1) * CONV_ROWS, cs] = acc
    uext[0:CONV_PAD, :] = uext[tm:tm + CONV_PAD, :]
    conv = _conv_tail(ybuf[...], dwb_ref, lng_ref, lnb_ref, pw_ref, pwb_ref)
    mix_ref[:, D_ATTN:] = conv.astype(BF16)


def _mixer_prompt(sink, q, k, v, u, dww, dwb, lng, lnb, pw_bf, pwb):
    n = q.shape[0]
    tm = TM_MIX
    row = lambda w: pl.BlockSpec((tm, w), lambda i, s: (i, 0))
    res = lambda shape: pl.BlockSpec(shape, lambda i, s: (0,) * len(shape), pipeline_mode=pl.Buffered(1))
    return pl.pallas_call(
        _mixer_prompt_body,
        grid_spec=pltpu.PrefetchScalarGridSpec(
            num_scalar_prefetch=1,
            grid=(n // tm,),
            in_specs=[row(D_ATTN), row(KV_DIM), row(KV_DIM), row(D_CONV), res((CONV_W, D_CONV)), res((1, D_CONV)),
                      res((1, D_CONV)), res((1, D_CONV)), res((D_CONV, D_CONV)), res((1, D_CONV))],
            out_specs=row(D_MODEL),
            scratch_shapes=[pltpu.VMEM((WINDOW, KV_DIM), BF16), pltpu.VMEM((WINDOW, KV_DIM), BF16),
                            pltpu.VMEM((CONV_PAD + tm, D_CONV), F32),
                            pltpu.VMEM((SUBLANES - 1, CONV_PAD - SUBLANES + tm, D_CONV), F32),
                            pltpu.VMEM((tm, D_CONV), F32)],
        ),
        out_shape=jax.ShapeDtypeStruct((n, D_MODEL), BF16),
        compiler_params=_params(),
        name="mixer_prompt",
    )(sink, q, k, v, u, dww, dwb, lng, lnb, pw_bf, pwb)


def _mixer_sample_body(sink_ref, q_ref, k_ref, v_ref, u_ref, ck_ref, cv_ref, cs_ref, dww_ref, dwb_ref, lng_ref,
                       lnb_ref, pw_ref, pwb_ref, mix_ref, wk_ref, wv_ref, wc_ref, usel, ybuf):
    nt = q_ref.shape[0]
    sb = ck_ref.shape[0]
    t_new = nt // sb
    n_cache = sb * WINDOW
    n_state = CONV_W - 1

    pad = jnp.zeros((LANES - nt, KV_DIM), F32)
    kn_t = jnp.concatenate([k_ref[...], pad], axis=0).T
    vn_t = jnp.concatenate([v_ref[...], pad], axis=0).T

    lt, lw = t_new.bit_length() - 1, WINDOW.bit_length() - 1
    qr = lax.broadcasted_iota(jnp.int32, (nt, n_cache), 0)
    cc = lax.broadcasted_iota(jnp.int32, (nt, n_cache), 1)
    mask_c = ((cc >> lw) == (qr >> lt)) & ((cc & (WINDOW - 1)) >= (qr & (t_new - 1)))
    qr = lax.broadcasted_iota(jnp.int32, (nt, LANES), 0)
    cn = lax.broadcasted_iota(jnp.int32, (nt, LANES), 1)
    mask_n = ((cn >> lt) == (qr >> lt)) & ((cn & (t_new - 1)) <= (qr & (t_new - 1)))
    for h in range(N_KV_HEADS):
        dims = slice(h * HEAD_DIM, (h + 1) * HEAD_DIM)
        hq = slice(h * GQA_GROUP * HEAD_DIM, (h + 1) * GQA_GROUP * HEAD_DIM)
        kc = jnp.concatenate([ck_ref[s, h] for s in range(sb)], axis=1).astype(BF16)
        vc = jnp.concatenate([cv_ref[s, h] for s in range(sb)], axis=1).astype(BF16)
        parts = [(kc, vc, mask_c), (kn_t[dims, :].astype(BF16), vn_t[dims, :].astype(BF16), mask_n)]
        o = _attend(q_ref[:, hq], parts, sink_ref, h, keys_on_lanes=True)
        mix_ref[:, hq] = o.astype(BF16)

    is_new = lax.broadcasted_iota(jnp.int32, (HEAD_DIM, WINDOW), 1) >= WINDOW - t_new
    for s in range(sb):
        shift = (WINDOW - t_new - s * t_new) % LANES
        kn_s = pltpu.roll(kn_t, shift=shift, axis=1)
        vn_s = pltpu.roll(vn_t, shift=shift, axis=1)
        for h in range(N_KV_HEADS):
            dims = slice(h * HEAD_DIM, (h + 1) * HEAD_DIM)
            old_k = pltpu.roll(ck_ref[s, h], shift=WINDOW - t_new, axis=1)
            old_v = pltpu.roll(cv_ref[s, h], shift=WINDOW - t_new, axis=1)
            wk_ref[s, h] = jnp.where(is_new, kn_s[dims, :], old_k)
            wv_ref[s, h] = jnp.where(is_new, vn_s[dims, :], old_v)

    for s in range(sb):
        for t in range(t_new):
            usel[t, s:s + 1, :] = u_ref[s * t_new + t:s * t_new + t + 1, :]
    u_new = [usel[t] for t in range(t_new)]
    for t in range(t_new):
        acc = jnp.zeros((sb, D_CONV), F32)
        for r in range(t, n_state):
            acc = acc + dww_ref[r - t:r - t + 1, :] * cs_ref[r]
        for t2 in range(t + 1):
            j = n_state - t + t2
            acc = acc + dww_ref[j:j + 1, :] * u_new[t2]
        for s in range(sb):
            ybuf[s * t_new + t:s * t_new + t + 1, :] = acc[s:s + 1, :]
        wc_ref[n_state - t_new + t] = u_new[t]
    wc_ref[0:n_state - t_new] = cs_ref[t_new:n_state]
    conv = _conv_tail(ybuf[...], dwb_ref, lng_ref, lnb_ref, pw_ref, pwb_ref)
    mix_ref[:, D_ATTN:] = conv.astype(BF16)


def _mixer_sample(sink, q, k, v, u, ck, cv, cs, dww, dwb, lng, lnb, pw_bf, pwb):
    n = q.shape[0]
    n_seq = ck.shape[1]
    t_new = n // n_seq
    sb = SEQ_TILE
    nt = sb * t_new
    n_state = CONV_W - 1
    row = lambda w: pl.BlockSpec((nt, w), lambda i, s: (i, 0))
    cache = pl.BlockSpec((None, sb, N_KV_HEADS, HEAD_DIM, WINDOW), lambda i, s: (0, i, 0, 0, 0))
    state = pl.BlockSpec((None, n_state, sb, D_CONV), lambda i, s: (0, 0, i, 0))
    res = lambda shape: pl.BlockSpec(shape, lambda i, s: (0,) * len(shape), pipeline_mode=pl.Buffered(1))
    return pl.pallas_call(
        _mixer_sample_body,
        grid_spec=pltpu.PrefetchScalarGridSpec(
            num_scalar_prefetch=1,
            grid=(n_seq // sb,),
            in_specs=[row(D_ATTN), row(KV_DIM), row(KV_DIM), row(D_CONV), cache, cache, state,
                      res((CONV_W, D_CONV)), res((1, D_CONV)), res((1, D_CONV)), res((1, D_CONV)),
                      res((D_CONV, D_CONV)), res((1, D_CONV))],
            out_specs=[row(D_MODEL), cache, cache, state],
            scratch_shapes=[pltpu.VMEM((t_new, sb, D_CONV), F32), pltpu.VMEM((nt, D_CONV), F32)],
        ),
        out_shape=[jax.ShapeDtypeStruct((n, D_MODEL), BF16), jax.ShapeDtypeStruct(ck.shape, F32),
                   jax.ShapeDtypeStruct(cv.shape, F32), jax.ShapeDtypeStruct(cs.shape, F32)],
        compiler_params=_params(),
        name="mixer_sample",
    )(sink, q, k, v, u, ck, cv, cs, dww, dwb, lng, lnb, pw_bf, pwb)


META_E0, META_E1, META_G0, META_G1, META_R0, META_R1 = range(6)


def _out_router_body(mixp_ref, xp_ref, mixs_ref, xs_ref, wout_ref, g2_ref, rw_ref, rb_ref, h_ref, meta_ref, cnt_ref,
                     carry, *, prompt_tiles):
    i = pl.program_id(0)
    tm = xp_ref.shape[0]

    @pl.when(i == 0)
    def _():
        carry[...] = jnp.zeros_like(carry)

    is_prompt = i < prompt_tiles
    x = jnp.where(is_prompt, xp_ref[...], xs_ref[...])
    mix = jnp.where(is_prompt, mixp_ref[...], mixs_ref[...])
    h = x + _dot(mix, wout_ref[...])
    h_ref[...] = h
    xn = _rmsnorm(h, g2_ref[...]).astype(BF16)
    logits = _dot(xn, rw_ref[...]) + rb_ref[...]
    lane = lax.broadcasted_iota(jnp.int32, logits.shape, 1).astype(F32)
    first = lambda cond: jnp.min(jnp.where(cond, lane, float(LANES)), axis=-1, keepdims=True)

    gl = jnp.where(lane < N_GROUPS, logits, -jnp.inf)
    gmax = jnp.max(gl, axis=-1, keepdims=True)
    gidx = first(gl == gmax)
    gval = 1.0 / jnp.sum(jnp.exp(gl - gmax), axis=-1, keepdims=True)

    lo = ROUTER_LANE0 + EXPERTS_PER_GROUP * gidx
    el = jnp.where((lane >= lo) & (lane < lo + EXPERTS_PER_GROUP), logits, -jnp.inf)
    m1 = jnp.max(el, axis=-1, keepdims=True)
    i1 = first(el == m1)
    el2 = jnp.where(lane == i1, -jnp.inf, el)
    m2 = jnp.max(el2, axis=-1, keepdims=True)
    i2 = first(el2 == m2)
    r = jnp.exp(m2 - m1)
    g0 = gval * (1.0 / (1.0 + r))
    g1 = gval * (r / (1.0 + r))

    sel0, sel1 = lane == i1, lane == i2
    onehot = jnp.where(sel0 | sel1, 1.0, 0.0)
    ti = lax.broadcasted_iota(jnp.int32, (tm, tm), 0)
    tj = lax.broadcasted_iota(jnp.int32, (tm, tm), 1)
    lower = jnp.where(tj < ti, 1.0, 0.0).astype(BF16)
    before = _dot(lower, onehot.astype(BF16)) + carry[0:1, :]
    r0 = jnp.sum(jnp.where(sel0, before, 0.0), axis=-1, keepdims=True)
    r1 = jnp.sum(jnp.where(sel1, before, 0.0), axis=-1, keepdims=True)
    carry[...] = carry[...] + jnp.sum(onehot, axis=0, keepdims=True)
    cnt_ref[...] = carry[...]

    lane_i = lax.broadcasted_iota(jnp.int32, logits.shape, 1)
    meta = jnp.zeros(logits.shape, F32)
    for slot, val in ((META_E0, i1 - ROUTER_LANE0), (META_E1, i2 - ROUTER_LANE0), (META_G0, g0), (META_G1, g1),
                      (META_R0, r0), (META_R1, r1)):
        meta = jnp.where(lane_i == slot, val, meta)
    meta_ref[...] = meta


def _out_router(mix_p, x_p, mix_s, x_s, wout_bf, g2, rw_bf, rb):
    n_p, n_s = x_p.shape[0], x_s.shape[0]
    tm = TM_PROJ
    assert n_p % tm == 0 and n_s % tm == 0
    tp, ts = n_p // tm, n_s // tm
    n = n_p + n_s
    prow = lambda w: pl.BlockSpec((tm, w), lambda i: (jnp.minimum(i, tp - 1), 0))
    srow = lambda w: pl.BlockSpec((tm, w), lambda i: (jnp.maximum(i - tp, 0), 0))
    row = lambda w: pl.BlockSpec((tm, w), lambda i: (i, 0))
    return pl.pallas_call(
        functools.partial(_out_router_body, prompt_tiles=tp),
        grid=(tp + ts,),
        in_specs=[prow(D_MODEL), prow(D_MODEL), srow(D_MODEL), srow(D_MODEL), _resident((D_MODEL, D_MODEL)),
                  _resident((1, D_MODEL)), _resident((D_MODEL, LANES)), _resident((1, LANES))],
        out_specs=[row(D_MODEL), row(LANES), pl.BlockSpec((8, LANES), lambda i: (0, 0))],
        out_shape=[jax.ShapeDtypeStruct((n, D_MODEL), F32), jax.ShapeDtypeStruct((n, LANES), F32),
                   jax.ShapeDtypeStruct((8, LANES), F32)],
        scratch_shapes=[pltpu.VMEM((8, LANES), F32)],
        compiler_params=_params(),
        name="out_router",
    )(mix_p, x_p, mix_s, x_s, wout_bf, g2, rw_bf, rb)


def _positions_body(meta_ref, off_ref, pos_ref):
    meta = meta_ref[...]
    lane = lax.broadcasted_iota(jnp.int32, meta.shape, 1)
    lane_f = lane.astype(F32)
    off = off_ref[...]
    pos = jnp.zeros(meta.shape, F32)
    for slot, (e_lane, r_lane) in enumerate(((META_E0, META_R0), (META_E1, META_R1))):
        e = meta[:, e_lane:e_lane + 1] + ROUTER_LANE0
        seg = jnp.sum(jnp.where(lane_f == e, off, 0.0), axis=-1, keepdims=True)
        pos = jnp.where(lane == slot, seg + meta[:, r_lane:r_lane + 1], pos)
    pos_ref[...] = pos.astype(jnp.int32)


def _positions(meta_all, off_lanes):
    n = meta_all.shape[0]
    tm = TM_PROJ
    assert n % tm == 0
    return pl.pallas_call(
        _positions_body,
        grid=(n // tm,),
        in_specs=[pl.BlockSpec((tm, LANES), lambda i: (i, 0)), _resident((1, LANES))],
        out_specs=pl.BlockSpec((tm, LANES), lambda i: (i, 0)),
        out_shape=jax.ShapeDtypeStruct((n, LANES), jnp.int32),
        compiler_params=_params(),
        name="positions",
    )(meta_all, off_lanes)


N_LOAD_SLOTS = 3


def _pow2_sizes(n):
    return [1 << b for b in range(n.bit_length() - 1, -1, -1)]


def _dispatch_body(pos_ref, nv_ref, h_ref, hs_ref, hbuf, zbuf, lsem, ssem, zsem, *, n_tok, tm):
    i = pl.program_id(0)
    n_steps = pl.num_programs(0)
    n_tiles = nv_ref.shape[0]

    groups = tm // SUBLANES

    def load(t, slot, sem=lsem):
        return pltpu.make_async_copy(h_ref.at[pl.ds(t * groups, groups)], hbuf.at[slot], sem.at[slot])

    def clear_unowned(start):
        def per_tile(t, c):
            nv = nv_ref[t]
            z = TR - nv
            row = t * TR + nv
            head = z & (SUBLANES - 1)
            for k in range(SUBLANES - 1):
                @pl.when(k < head)
                def _(k=k):
                    cp = pltpu.make_async_copy(zbuf.at[pl.ds(0, 1), :], hs_ref.at[pl.ds(row + k, 1), :], zsem)
                    cp.start() if start else cp.wait()
            row = row + head
            for size in _pow2_sizes(TR):
                if size < SUBLANES:
                    break
                @pl.when((z & size) != 0)
                def _(row=row, size=size):
                    dst = hs_ref.at[pl.ds(pl.multiple_of(row, SUBLANES), size), :]
                    cp = pltpu.make_async_copy(zbuf.at[pl.ds(0, size), :], dst, zsem)
                    cp.start() if start else cp.wait()
                row = row + (z & size)
            return c
        lax.fori_loop(0, n_tiles, per_tile, 0)

    def scatter_wait(slot):
        for _ in range(2):
            load(0, slot, ssem).wait()

    @pl.when(i == 0)
    def _():
        zbuf[...] = jnp.zeros_like(zbuf)
        clear_unowned(True)
        load(0, 0).start()

    @pl.when(i + 1 < n_steps)
    def _():
        load(i + 1, (i + 1) % N_LOAD_SLOTS).start()

    slot = i % N_LOAD_SLOTS
    load(i, slot).wait()

    def issue(jb, c):
        for u in range(SUBLANES):
            for s in range(2):
                p = pos_ref[s * n_tok + i * tm + jb * SUBLANES + u]
                pltpu.make_async_copy(hbuf.at[slot, jb, pl.ds(u, 1), :], hs_ref.at[pl.ds(p, 1), :],
                                      ssem.at[slot]).start()
        return c
    lax.fori_loop(0, groups, issue, 0)

    @pl.when(i > 0)
    def _():
        scatter_wait((i + N_LOAD_SLOTS - 1) % N_LOAD_SLOTS)

    @pl.when(i == n_steps - 1)
    def _():
        scatter_wait(slot)
        clear_unowned(False)


def _dispatch(pos, nv, rows):
    n_tok, width = rows.shape
    n_tiles = nv.shape[0]
    tm = TM_PROJ
    assert n_tok % tm == 0
    return pl.pallas_call(
        functools.partial(_dispatch_body, n_tok=n_tok, tm=tm),
        grid_spec=pltpu.PrefetchScalarGridSpec(
            num_scalar_prefetch=2,
            grid=(n_tok // tm,),
            in_specs=[pl.BlockSpec(memory_space=pl.ANY)],
            out_specs=pl.BlockSpec(memory_space=pl.ANY),
            scratch_shapes=[pltpu.VMEM((N_LOAD_SLOTS, tm // SUBLANES, SUBLANES, width), rows.dtype),
                            pltpu.VMEM((TR, width), rows.dtype),
                            pltpu.SemaphoreType.DMA((N_LOAD_SLOTS,)), pltpu.SemaphoreType.DMA((N_LOAD_SLOTS,)),
                            pltpu.SemaphoreType.DMA(())],
        ),
        out_shape=jax.ShapeDtypeStruct((n_tiles * TR, width), rows.dtype),
        compiler_params=_params(),
        name="dispatch",
    )(pos, nv, rows.reshape(n_tok // SUBLANES, SUBLANES, width))


def _moe_body(ts_ref, hs_ref, g2_ref, wg_ref, wu_ref, wd_ref, ys_ref, xbuf, obuf, wg_bf, wu_bf, wd_bf, lsem, ssem):
    e = pl.program_id(0)
    n_tiles = ys_ref.shape[0] // TR
    n_used = ts_ref[N_EXPERTS]
    t0, t1 = ts_ref[e], ts_ref[e + 1]

    def load(g, slot):
        return pltpu.make_async_copy(hs_ref.at[pl.ds(pl.multiple_of(g * TR, TR), TR), :], xbuf.at[slot], lsem.at[slot])

    def store(g, slot):
        return pltpu.make_async_copy(obuf.at[slot], ys_ref.at[pl.ds(pl.multiple_of(g * TR, TR), TR), :], ssem.at[slot])

    @pl.when(e == 0)
    def _():
        load(0, 0).start()

    @pl.when(t1 > t0)
    def _():
        wg_bf[...] = wg_ref[0].astype(BF16)
        wu_bf[...] = wu_ref[0].astype(BF16)
        wd_bf[...] = wd_ref[0].astype(BF16)

    def tile(g, c):
        slot = g % 2

        @pl.when(g + 1 < n_used)
        def _():
            load(g + 1, 1 - slot).start()

        load(g, slot).wait()
        xn = _rmsnorm(xbuf[slot], g2_ref[...]).astype(BF16)
        hg = _dot(xn, wg_bf[...])
        hu = _dot(xn, wu_bf[...])
        hid = (hg * jax.nn.sigmoid(hg) * hu).astype(BF16)
        y = _dot(hid, wd_bf[...])

        @pl.when(g >= 2)
        def _():
            store(g - 2, slot).wait()

        obuf[slot] = y
        store(g, slot).start()
        return c

    lax.fori_loop(t0, t1, tile, 0)

    @pl.when(e == pl.num_programs(0) - 1)
    def _():
        @pl.when(n_used >= 2)
        def _():
            store(n_used - 2, n_used % 2).wait()
        store(n_used - 1, (n_used - 1) % 2).wait()
        obuf[0] = jnp.zeros(obuf.shape[1:], obuf.dtype)

        def clear(g, c):
            store(g, 0).start()
            return c
        lax.fori_loop(n_used, n_tiles, clear, 0)

        def drain(g, c):
            store(g, 0).wait()
            return c
        lax.fori_loop(n_used, n_tiles, drain, 0)


def _moe(ts, hs, g2, wg, wu, wd):
    wspec = lambda shape: pl.BlockSpec((1,) + shape, lambda e, ts: (e, 0, 0))
    tile_buf = pltpu.VMEM((2, TR, hs.shape[1]), hs.dtype)
    return pl.pallas_call(
        _moe_body,
        grid_spec=pltpu.PrefetchScalarGridSpec(
            num_scalar_prefetch=1,
            grid=(N_EXPERTS,),
            in_specs=[pl.BlockSpec(memory_space=pl.ANY),
                      pl.BlockSpec((1, D_MODEL), lambda e, ts: (0, 0)),
                      wspec((D_MODEL, D_EXPERT)), wspec((D_MODEL, D_EXPERT)), wspec((D_EXPERT, D_MODEL))],
            out_specs=pl.BlockSpec(memory_space=pl.ANY),
            scratch_shapes=[tile_buf, tile_buf,
                            pltpu.VMEM((D_MODEL, D_EXPERT), BF16), pltpu.VMEM((D_MODEL, D_EXPERT), BF16),
                            pltpu.VMEM((D_EXPERT, D_MODEL), BF16),
                            pltpu.SemaphoreType.DMA((2,)), pltpu.SemaphoreType.DMA((2,))],
        ),
        out_shape=jax.ShapeDtypeStruct(hs.shape, hs.dtype),
        compiler_params=_params(),
        name="moe",
    )(ts, hs, g2, wg, wu, wd)


def _combine_body(pos_ref, h_ref, meta_ref, g_ref, ys_ref, ys_grouped_ref, o_ref, ybuf, sem, *, n_tok, row0):
    i = pl.program_id(0)
    n_steps = pl.num_programs(0)
    tm = h_ref.shape[0]

    groups = tm // SUBLANES

    def gather(t, slot):
        def issue(jb, c):
            for u in range(SUBLANES):
                for s in range(2):
                    p = pos_ref[s * n_tok + row0 + t * tm + jb * SUBLANES + u]
                    pltpu.make_async_copy(ys_ref.at[pl.ds(p, 1), :], ybuf.at[slot, s, jb, pl.ds(u, 1), :],
                                          sem.at[slot]).start()
            return c
        lax.fori_loop(0, groups, issue, 0)

    @pl.when(i == 0)
    def _():
        gather(0, 0)

    @pl.when(i + 1 < n_steps)
    def _():
        gather(i + 1, (i + 1) % 2)

    slot = i % 2
    for s in range(2):
        pltpu.make_async_copy(ys_grouped_ref.at[pl.ds(0, groups)], ybuf.at[slot, s], sem.at[slot]).wait()
    meta = meta_ref[...]
    g0 = meta[:, META_G0:META_G0 + 1]
    g1 = meta[:, META_G1:META_G1 + 1]
    y0 = ybuf[slot, 0].reshape(tm, D_MODEL)
    y1 = ybuf[slot, 1].reshape(tm, D_MODEL)
    h = h_ref[...] + (g0 * y0 + g1 * y1)
    o_ref[...] = _rmsnorm(h, g_ref[...])


def _combine(pos, h_all, meta_all, ys, gf, row0, n):
    n_tok = h_all.shape[0]
    tm = min(TM_OUT, n)
    assert row0 % tm == 0 and n % tm == 0
    b0 = row0 // tm
    return pl.pallas_call(
        functools.partial(_combine_body, n_tok=n_tok, row0=row0),
        grid_spec=pltpu.PrefetchScalarGridSpec(
            num_scalar_prefetch=1,
            grid=(n // tm,),
            in_specs=[pl.BlockSpec((tm, D_MODEL), lambda i, pos: (b0 + i, 0)),
                      pl.BlockSpec((tm, LANES), lambda i, pos: (b0 + i, 0)),
                      pl.BlockSpec((1, D_MODEL), lambda i, pos: (0, 0)),
                      pl.BlockSpec(memory_space=pl.ANY), pl.BlockSpec(memory_space=pl.ANY)],
            out_specs=pl.BlockSpec((tm, D_MODEL), lambda i, pos: (i, 0)),
            scratch_shapes=[pltpu.VMEM((2, 2, tm // SUBLANES, SUBLANES, ys.shape[1]), ys.dtype),
                            pltpu.SemaphoreType.DMA((2,))],
        ),
        out_shape=jax.ShapeDtypeStruct((n, D_MODEL), F32),
        compiler_params=_params(),
        name="combine",
    )(pos, h_all, meta_all, gf, ys, ys.reshape(ys.shape[0] // SUBLANES, SUBLANES, ys.shape[1]))


def _routing_tables(counts, n_tok):
    n_tiles = (2 * n_tok) // TR + N_EXPERTS
    cnt = counts[ROUTER_LANE0:ROUTER_LANE0 + N_EXPERTS].astype(jnp.int32)
    tiles_e = (cnt + TR - 1) // TR
    tile_end = jnp.cumsum(tiles_e)
    tile_start = tile_end - tiles_e
    off_lanes = jnp.pad((tile_start * TR).astype(F32), (ROUTER_LANE0, LANES - ROUTER_LANE0 - N_EXPERTS))
    ts = jnp.concatenate([tile_start, tile_end[-1:]]).astype(jnp.int32)
    tile = jnp.arange(n_tiles, dtype=jnp.int32)
    owner = (tile[:, None] >= tile_start[None, :]) & (tile[:, None] < tile_end[None, :])
    rows_left = jnp.sum(jnp.where(owner, cnt[None, :] - (tile[:, None] - tile_start[None, :]) * TR, 0), axis=1)
    nv = jnp.clip(rows_left, 0, TR).astype(jnp.int32)
    return off_lanes.reshape(1, LANES), ts, nv


def kernel(x_prompt, x_sample, cache_win_k, cache_win_v, state_conv, norm1_g, w_in, attn_sink, conv_dw_w, conv_dw_b,
           conv_ln_g, conv_ln_b, conv_pw_w, conv_pw_b, w_out, norm2_g, router_group_w, router_group_b,
           router_expert_w, router_expert_b, expert_w_gate, expert_w_up, expert_w_down, final_norm_g):
    depth = w_in.shape[0]
    assert depth == 1, "single-layer step"
    bp, sp, _ = x_prompt.shape
    assert bp == 1, "one prompt sequence"
    n_seq, t_new, _ = x_sample.shape
    n_p, n_s = bp * sp, n_seq * t_new
    n_tok = n_p + n_s
    l = 0

    row = lambda a: a.reshape(1, -1)
    w_in_bf = w_in[l].astype(BF16)
    w_out_bf = w_out[l].astype(BF16)
    pw_bf = conv_pw_w[l].astype(BF16)
    rw = jnp.concatenate([router_group_w[l], router_expert_w[l]], axis=1)
    rw_bf = jnp.pad(rw, ((0, 0), (0, LANES - rw.shape[1]))).astype(BF16)
    rb = jnp.pad(jnp.concatenate([router_group_b[l], router_expert_b[l]]), (0, LANES - rw.shape[1])).reshape(1, LANES)
    g1, g2, gf = row(norm1_g[l]), row(norm2_g[l]), row(final_norm_g)
    sink = attn_sink[l]
    conv_w = (conv_dw_w[l], row(conv_dw_b[l]), row(conv_ln_g[l]), row(conv_ln_b[l]), pw_bf, row(conv_pw_b[l]))

    xp = x_prompt.reshape(n_p, D_MODEL)
    xs = x_sample.reshape(n_s, D_MODEL)
    cache_axes, cache_axes_back = (0, 1, 3, 4, 2), (0, 1, 4, 2, 3)
    ck = jnp.transpose(cache_win_k, cache_axes)
    cv = jnp.transpose(cache_win_v, cache_axes)
    cs = jnp.transpose(state_conv, (0, 2, 1, 3))

    qp, kp, vp, up = _in_proj(xp, g1, w_in_bf)
    qs, ks, vs, us = _in_proj(xs, g1, w_in_bf)
    mix_p = _mixer_prompt(sink, qp, kp, vp, up, *conv_w)
    mix_s, wk_s, wv_s, wc_s = _mixer_sample(sink, qs, ks, vs, us, ck, cv, cs, *conv_w)

    h_all, meta_all, cnt = _out_router(mix_p, xp, mix_s, xs, w_out_bf, g2, rw_bf, rb)

    off_lanes, ts, nv = _routing_tables(cnt[0], n_tok)
    pos_lanes = _positions(meta_all, off_lanes)
    pos = jnp.concatenate([pos_lanes[:, 0], pos_lanes[:, 1]])
    hs = _dispatch(pos, nv, h_all)
    ys = _moe(ts, hs, g2, expert_w_gate[l], expert_w_up[l], expert_w_down[l])
    y_p = _combine(pos, h_all, meta_all, ys, gf, 0, n_p)
    y_s = _combine(pos, h_all, meta_all, ys, gf, n_p, n_s)

    kv_shape = (depth, bp, WINDOW, N_KV_HEADS, HEAD_DIM)
    return (y_p.reshape(bp, sp, D_MODEL), y_s.reshape(n_seq, t_new, D_MODEL),
            kp[n_p - WINDOW:].reshape(kv_shape), vp[n_p - WINDOW:].reshape(kv_shape),
            up[n_p - (CONV_W - 1):].reshape(depth, bp, CONV_W - 1, D_CONV),
            jnp.transpose(wk_s, cache_axes_back), jnp.transpose(wv_s, cache_axes_back),
            jnp.transpose(wc_s, (0, 2, 1, 3)))
```

```python
import functools

import jax
import jax.numpy as jnp
from jax import lax
from jax.experimental import pallas as pl
from jax.experimental.pallas import tpu as pltpu

F32 = jnp.float32
BF16 = jnp.bfloat16

D_MODEL = 2048
HEAD_DIM = 64
N_HEADS = 16
N_KV_HEADS = 2
GQA_GROUP = 8
KV_DIM = N_KV_HEADS * HEAD_DIM
D_ATTN = N_HEADS * HEAD_DIM
D_CONV = D_MODEL - D_ATTN
WINDOW = 128
CONV_W = 31
D_IN = D_ATTN + 2 * KV_DIM + 2 * D_CONV
N_GROUPS = 4
EXPERTS_PER_GROUP = 8
N_EXPERTS = N_GROUPS * EXPERTS_PER_GROUP
D_EXPERT = 256
RMS_EPS = 1e-6
LN_EPS = 1e-5

LANES = 128
SUBLANES = 8
ROUTER_LANE0 = N_GROUPS
VMEM_LIMIT = 56 * 1024 * 1024

TM_PROJ = 512
TM_MIX = 256
SEQ_TILE = 8
TR = 256
TM_OUT = 256

_NT = (((1,), (1,)), ((), ()))


def _params(n_axes=1):
    return pltpu.CompilerParams(dimension_semantics=("arbitrary",) * n_axes, vmem_limit_bytes=VMEM_LIMIT)


def _resident(shape):
    return pl.BlockSpec(shape, lambda *_: (0,) * len(shape), pipeline_mode=pl.Buffered(1))


def _rmsnorm(x, g):
    ms = jnp.mean(x * x, axis=-1, keepdims=True)
    return x * lax.rsqrt(ms + RMS_EPS) * g


def _dot(a, b):
    return jnp.dot(a, b, preferred_element_type=F32)


def _dot_nt(a, b):
    return lax.dot_general(a, b, _NT, preferred_element_type=F32)


def _in_proj_body(x_ref, g_ref, w_ref, q_ref, k_ref, v_ref, u_ref):
    xn = _rmsnorm(x_ref[...], g_ref[...]).astype(BF16)
    q_ref[...] = (_dot(xn, w_ref[:, :D_ATTN]) * (HEAD_DIM ** -0.5)).astype(BF16)
    kv = _dot(xn, w_ref[:, D_ATTN:D_ATTN + 2 * KV_DIM])
    k_ref[...] = kv[:, :KV_DIM]
    v_ref[...] = kv[:, KV_DIM:]
    c0 = D_ATTN + 2 * KV_DIM
    cw = 256
    for j in range(D_CONV // cw):
        a = _dot(xn, w_ref[:, c0 + j * cw:c0 + (j + 1) * cw])
        b = _dot(xn, w_ref[:, c0 + D_CONV + j * cw:c0 + D_CONV + (j + 1) * cw])
        u_ref[:, j * cw:(j + 1) * cw] = a * jax.nn.sigmoid(b)


def _in_proj(x, g, w_bf):
    n = x.shape[0]
    tm = min(TM_PROJ, n)
    row = lambda w: pl.BlockSpec((tm, w), lambda i: (i, 0))
    return pl.pallas_call(
        _in_proj_body,
        grid=(n // tm,),
        in_specs=[row(D_MODEL), _resident((1, D_MODEL)), _resident((D_MODEL, D_IN))],
        out_specs=[row(D_ATTN), row(KV_DIM), row(KV_DIM), row(D_CONV)],
        out_shape=[jax.ShapeDtypeStruct((n, D_ATTN), BF16), jax.ShapeDtypeStruct((n, KV_DIM), F32),
                   jax.ShapeDtypeStruct((n, KV_DIM), F32), jax.ShapeDtypeStruct((n, D_CONV), F32)],
        compiler_params=_params(),
        name="in_proj",
    )(x, g, w_bf)


def _attend(q, parts, sink_ref, kv_head, keys_on_lanes=False):
    t = q.shape[0]
    score, weigh = (_dot, _dot_nt) if keys_on_lanes else (_dot_nt, _dot)
    qs = jnp.concatenate([q[:, g * HEAD_DIM:(g + 1) * HEAD_DIM] for g in range(GQA_GROUP)], axis=0)
    scores = [score(qs, kk) for kk, _, _ in parts]
    ps = [[] for _ in parts]
    inv = []
    for g in range(GQA_GROUP):
        sg = [jnp.where(mask, s[g * t:(g + 1) * t], -jnp.inf) for s, (_, _, mask) in zip(scores, parts)]
        sk = sink_ref[kv_head * GQA_GROUP + g]
        m = sk
        for x in sg:
            m = jnp.maximum(jnp.max(x, axis=-1, keepdims=True), m)
        den = jnp.exp(sk - m)
        for k, x in enumerate(sg):
            p = jnp.exp(x - m)
            den = den + jnp.sum(p, axis=-1, keepdims=True)
            ps[k].append(p.astype(BF16))
        inv.append(1.0 / den)
    o = None
    for k, (_, vv, _) in enumerate(parts):
        ok = weigh(jnp.concatenate(ps[k], axis=0), vv)
        o = ok if o is None else o + ok
    return jnp.concatenate([o[g * t:(g + 1) * t] * inv[g] for g in range(GQA_GROUP)], axis=1)


def _conv_tail(y, dwb_ref, lng_ref, lnb_ref, pw_ref, pwb_ref):
    y = y + dwb_ref[...]
    mu = jnp.mean(y, axis=-1, keepdims=True)
    yc = y - mu
    yn = yc * lax.rsqrt(jnp.mean(yc * yc, axis=-1, keepdims=True) + LN_EPS)
    yn = yn * lng_ref[...] + lnb_ref[...]
    act = yn * jax.nn.sigmoid(yn)
    return _dot(act.astype(BF16), pw_ref[...]) + pwb_ref[...]


CONV_PAD = 32
CONV_ROWS = 32
CONV_COLS = 256


def _mixer_prompt_body(sink_ref, q_ref, k_ref, v_ref, u_ref, dww_ref, dwb_ref, lng_ref, lnb_ref, pw_ref, pwb_ref,
                       mix_ref, kprev, vprev, uext, ushift, ybuf):
    i = pl.program_id(0)
    tm = q_ref.shape[0]

    @pl.when(i == 0)
    def _():
        kprev[...] = jnp.zeros_like(kprev)
        vprev[...] = jnp.zeros_like(vprev)
        uext[0:CONV_PAD, :] = jnp.zeros((CONV_PAD, D_CONV), F32)

    qi = lax.broadcasted_iota(jnp.int32, (WINDOW, 2 * WINDOW), 0)
    kj = lax.broadcasted_iota(jnp.int32, (WINDOW, 2 * WINDOW), 1)
    diff = qi + WINDOW - kj
    band = (diff >= 0) & (diff <= WINDOW)
    kp, vp = kprev[...], vprev[...]
    for b in range(tm // WINDOW):
        rows = slice(b * WINDOW, (b + 1) * WINDOW)
        kb = k_ref[rows, :].astype(BF16)
        vb = v_ref[rows, :].astype(BF16)
        kk = jnp.concatenate([kp, kb], axis=0)
        vv = jnp.concatenate([vp, vb], axis=0)
        if b == 0:
            mask = band & (kj >= jnp.where(i > 0, 0, WINDOW))
        else:
            mask = band
        for h in range(N_KV_HEADS):
            cols = slice(h * HEAD_DIM, (h + 1) * HEAD_DIM)
            hq = slice(h * GQA_GROUP * HEAD_DIM, (h + 1) * GQA_GROUP * HEAD_DIM)
            o = _attend(q_ref[rows, hq], [(kk[:, cols], vv[:, cols], mask)], sink_ref, h)
            mix_ref[rows, hq] = o.astype(BF16)
        kp, vp = kb, vb
    kprev[...] = kp
    vprev[...] = vp

    uext[CONV_PAD:CONV_PAD + tm, :] = u_ref[...]
    n_shift_rows = ushift.shape[1]
    for s in range(1, SUBLANES):
        ushift[s - 1] = uext[s:s + n_shift_rows, :]
    off = CONV_PAD - (CONV_W - 1)
    for r in range(tm // CONV_ROWS):
        for c in range(D_CONV // CONV_COLS):
            cs = slice(c * CONV_COLS, (c + 1) * CONV_COLS)
            acc = jnp.zeros((CONV_ROWS, CONV_COLS), F32)
            for j in range(CONV_W):
                a, s = divmod(off + j, SUBLANES)
                r0 = r * CONV_ROWS + a * SUBLANES
                src = uext[r0:r0 + CONV_ROWS, cs] if s == 0 else ushift[s - 1, r0:r0 + CONV_ROWS, cs]
                acc = acc + dww_ref[j:j + 1, cs] * src
            ybuf[r * CONV_ROWS:(r + 1) * CONV_ROWS, cs] = acc
    uext[0:CONV_PAD, :] = uext[tm:tm + CONV_PAD, :]
    conv = _conv_tail(ybuf[...], dwb_ref, lng_ref, lnb_ref, pw_ref, pwb_ref)
    mix_ref[:, D_ATTN:] = conv.astype(BF16)


def _mixer_prompt(sink, q, k, v, u, dww, dwb, lng, lnb, pw_bf, pwb):
    n = q.shape[0]
    tm = TM_MIX
    row = lambda w: pl.BlockSpec((tm, w), lambda i, s: (i, 0))
    res = lambda shape: pl.BlockSpec(shape, lambda i, s: (0,) * len(shape), pipeline_mode=pl.Buffered(1))
    return pl.pallas_call(
        _mixer_prompt_body,
        grid_spec=pltpu.PrefetchScalarGridSpec(
            num_scalar_prefetch=1,
            grid=(n // tm,),
            in_specs=[row(D_ATTN), row(KV_DIM), row(KV_DIM), row(D_CONV), res((CONV_W, D_CONV)), res((1, D_CONV)),
                      res((1, D_CONV)), res((1, D_CONV)), res((D_CONV, D_CONV)), res((1, D_CONV))],
            out_specs=row(D_MODEL),
            scratch_shapes=[pltpu.VMEM((WINDOW, KV_DIM), BF16), pltpu.VMEM((WINDOW, KV_DIM), BF16),
                            pltpu.VMEM((CONV_PAD + tm, D_CONV), F32),
                            pltpu.VMEM((SUBLANES - 1, CONV_PAD - SUBLANES + tm, D_CONV), F32),
                            pltpu.VMEM((tm, D_CONV), F32)],
        ),
        out_shape=jax.ShapeDtypeStruct((n, D_MODEL), BF16),
        compiler_params=_params(),
        name="mixer_prompt",
    )(sink, q, k, v, u, dww, dwb, lng, lnb, pw_bf, pwb)


def _mixer_sample_body(sink_ref, q_ref, k_ref, v_ref, u_ref, ck_ref, cv_ref, cs_ref, dww_ref, dwb_ref, lng_ref,
                       lnb_ref, pw_ref, pwb_ref, mix_ref, wk_ref, wv_ref, wc_ref, usel, ybuf):
    nt = q_ref.shape[0]
    sb = ck_ref.shape[0]
    t_new = nt // sb
    n_cache = sb * WINDOW
    n_state = CONV_W - 1

    pad = jnp.zeros((LANES - nt, KV_DIM), F32)
    kn_t = jnp.concatenate([k_ref[...], pad], axis=0).T
    vn_t = jnp.concatenate([v_ref[...], pad], axis=0).T

    lt, lw = t_new.bit_length() - 1, WINDOW.bit_length() - 1
    qr = lax.broadcasted_iota(jnp.int32, (nt, n_cache), 0)
    cc = lax.broadcasted_iota(jnp.int32, (nt, n_cache), 1)
    mask_c = ((cc >> lw) == (qr >> lt)) & ((cc & (WINDOW - 1)) >= (qr & (t_new - 1)))
    qr = lax.broadcasted_iota(jnp.int32, (nt, LANES), 0)
    cn = lax.broadcasted_iota(jnp.int32, (nt, LANES), 1)
    mask_n = ((cn >> lt) == (qr >> lt)) & ((cn & (t_new - 1)) <= (qr & (t_new - 1)))
    for h in range(N_KV_HEADS):
        dims = slice(h * HEAD_DIM, (h + 1) * HEAD_DIM)
        hq = slice(h * GQA_GROUP * HEAD_DIM, (h + 1) * GQA_GROUP * HEAD_DIM)
        kc = jnp.concatenate([ck_ref[s, h] for s in range(sb)], axis=1).astype(BF16)
        vc = jnp.concatenate([cv_ref[s, h] for s in range(sb)], axis=1).astype(BF16)
        parts = [(kc, vc, mask_c), (kn_t[dims, :].astype(BF16), vn_t[dims, :].astype(BF16), mask_n)]
        o = _attend(q_ref[:, hq], parts, sink_ref, h, keys_on_lanes=True)
        mix_ref[:, hq] = o.astype(BF16)

    is_new = lax.broadcasted_iota(jnp.int32, (HEAD_DIM, WINDOW), 1) >= WINDOW - t_new
    for s in range(sb):
        shift = (WINDOW - t_new - s * t_new) % LANES
        kn_s = pltpu.roll(kn_t, shift=shift, axis=1)
        vn_s = pltpu.roll(vn_t, shift=shift, axis=1)
        for h in range(N_KV_HEADS):
            dims = slice(h * HEAD_DIM, (h + 1) * HEAD_DIM)
            old_k = pltpu.roll(ck_ref[s, h], shift=WINDOW - t_new, axis=1)
            old_v = pltpu.roll(cv_ref[s, h], shift=WINDOW - t_new, axis=1)
            wk_ref[s, h] = jnp.where(is_new, kn_s[dims, :], old_k)
            wv_ref[s, h] = jnp.where(is_new, vn_s[dims, :], old_v)

    for s in range(sb):
        for t in range(t_new):
            usel[t, s:s + 1, :] = u_ref[s * t_new + t:s * t_new + t + 1, :]
    u_new = [usel[t] for t in range(t_new)]
    for t in range(t_new):
        acc = jnp.zeros((sb, D_CONV), F32)
        for r in range(t, n_state):
            acc = acc + dww_ref[r - t:r - t + 1, :] * cs_ref[r]
        for t2 in range(t + 1):
            j = n_state - t + t2
            acc = acc + dww_ref[j:j + 1, :] * u_new[t2]
        for s in range(sb):
            ybuf[s * t_new + t:s * t_new + t + 1, :] = acc[s:s + 1, :]
        wc_ref[n_state - t_new + t] = u_new[t]
    wc_ref[0:n_state - t_new] = cs_ref[t_new:n_state]
    conv = _conv_tail(ybuf[...], dwb_ref, lng_ref, lnb_ref, pw_ref, pwb_ref)
    mix_ref[:, D_ATTN:] = conv.astype(BF16)


def _mixer_sample(sink, q, k, v, u, ck, cv, cs, dww, dwb, lng, lnb, pw_bf, pwb):
    n = q.shape[0]
    n_seq = ck.shape[1]
    t_new = n // n_seq
    sb = SEQ_TILE
    nt = sb * t_new
    n_state = CONV_W - 1
    row = lambda w: pl.BlockSpec((nt, w), lambda i, s: (i, 0))
    cache = pl.BlockSpec((None, sb, N_KV_HEADS, HEAD_DIM, WINDOW), lambda i, s: (0, i, 0, 0, 0))
    state = pl.BlockSpec((None, n_state, sb, D_CONV), lambda i, s: (0, 0, i, 0))
    res = lambda shape: pl.BlockSpec(shape, lambda i, s: (0,) * len(shape), pipeline_mode=pl.Buffered(1))
    return pl.pallas_call(
        _mixer_sample_body,
        grid_spec=pltpu.PrefetchScalarGridSpec(
            num_scalar_prefetch=1,
            grid=(n_seq // sb,),
            in_specs=[row(D_ATTN), row(KV_DIM), row(KV_DIM), row(D_CONV), cache, cache, state,
                      res((CONV_W, D_CONV)), res((1, D_CONV)), res((1, D_CONV)), res((1, D_CONV)),
                      res((D_CONV, D_CONV)), res((1, D_CONV))],
            out_specs=[row(D_MODEL), cache, cache, state],
            scratch_shapes=[pltpu.VMEM((t_new, sb, D_CONV), F32), pltpu.VMEM((nt, D_CONV), F32)],
        ),
        out_shape=[jax.ShapeDtypeStruct((n, D_MODEL), BF16), jax.ShapeDtypeStruct(ck.shape, F32),
                   jax.ShapeDtypeStruct(cv.shape, F32), jax.ShapeDtypeStruct(cs.shape, F32)],
        compiler_params=_params(),
        name="mixer_sample",
    )(sink, q, k, v, u, ck, cv, cs, dww, dwb, lng, lnb, pw_bf, pwb)


META_E0, META_E1, META_G0, META_G1, META_R0, META_R1 = range(6)


def _out_router_body(mixp_ref, xp_ref, mixs_ref, xs_ref, wout_ref, g2_ref, rw_ref, rb_ref, h_ref, meta_ref, cnt_ref,
                     carry, lower, *, prompt_tiles):
    i = pl.program_id(0)
    tm = xp_ref.shape[0]

    @pl.when(i == 0)
    def _():
        carry[...] = jnp.zeros_like(carry)
        ti = lax.broadcasted_iota(jnp.int32, (tm, tm), 0)
        tj = lax.broadcasted_iota(jnp.int32, (tm, tm), 1)
        lower[...] = jnp.where(tj < ti, 1.0, 0.0).astype(BF16)

    is_prompt = i < prompt_tiles
    x = jnp.where(is_prompt, xp_ref[...], xs_ref[...])
    mix = jnp.where(is_prompt, mixp_ref[...], mixs_ref[...])
    h = x + _dot(mix, wout_ref[...])
    h_ref[...] = h
    xn = _rmsnorm(h, g2_ref[...]).astype(BF16)
    logits = _dot(xn, rw_ref[...]) + rb_ref[...]
    lane = lax.broadcasted_iota(jnp.int32, logits.shape, 1).astype(F32)
    first = lambda cond: jnp.min(jnp.where(cond, lane, float(LANES)), axis=-1, keepdims=True)

    gl = jnp.where(lane < N_GROUPS, logits, -jnp.inf)
    gmax = jnp.max(gl, axis=-1, keepdims=True)
    gidx = first(gl == gmax)
    gval = 1.0 / jnp.sum(jnp.exp(gl - gmax), axis=-1, keepdims=True)

    lo = ROUTER_LANE0 + EXPERTS_PER_GROUP * gidx
    el = jnp.where((lane >= lo) & (lane < lo + EXPERTS_PER_GROUP), logits, -jnp.inf)
    m1 = jnp.max(el, axis=-1, keepdims=True)
    i1 = first(el == m1)
    el2 = jnp.where(lane == i1, -jnp.inf, el)
    m2 = jnp.max(el2, axis=-1, keepdims=True)
    i2 = first(el2 == m2)
    r = jnp.exp(m2 - m1)
    g0 = gval * (1.0 / (1.0 + r))
    g1 = gval * (r / (1.0 + r))

    sel0, sel1 = lane == i1, lane == i2
    onehot = jnp.where(sel0 | sel1, 1.0, 0.0)
    before = _dot(lower[...], onehot.astype(BF16)) + carry[0:1, :]
    r0 = jnp.sum(jnp.where(sel0, before, 0.0), axis=-1, keepdims=True)
    r1 = jnp.sum(jnp.where(sel1, before, 0.0), axis=-1, keepdims=True)
    carry[...] = carry[...] + jnp.sum(onehot, axis=0, keepdims=True)
    cnt_ref[...] = carry[...]

    lane_i = lax.broadcasted_iota(jnp.int32, logits.shape, 1)
    meta = jnp.zeros(logits.shape, F32)
    for slot, val in ((META_E0, i1 - ROUTER_LANE0), (META_E1, i2 - ROUTER_LANE0), (META_G0, g0), (META_G1, g1),
                      (META_R0, r0), (META_R1, r1)):
        meta = jnp.where(lane_i == slot, val, meta)
    meta_ref[...] = meta


def _out_router(mix_p, x_p, mix_s, x_s, wout_bf, g2, rw_bf, rb):
    n_p, n_s = x_p.shape[0], x_s.shape[0]
    tm = TM_PROJ
    assert n_p % tm == 0 and n_s % tm == 0
    tp, ts = n_p // tm, n_s // tm
    n = n_p + n_s
    prow = lambda w: pl.BlockSpec((tm, w), lambda i: (jnp.minimum(i, tp - 1), 0))
    srow = lambda w: pl.BlockSpec((tm, w), lambda i: (jnp.maximum(i - tp, 0), 0))
    row = lambda w: pl.BlockSpec((tm, w), lambda i: (i, 0))
    return pl.pallas_call(
        functools.partial(_out_router_body, prompt_tiles=tp),
        grid=(tp + ts,),
        in_specs=[prow(D_MODEL), prow(D_MODEL), srow(D_MODEL), srow(D_MODEL), _resident((D_MODEL, D_MODEL)),
                  _resident((1, D_MODEL)), _resident((D_MODEL, LANES)), _resident((1, LANES))],
        out_specs=[row(D_MODEL), row(LANES), pl.BlockSpec((8, LANES), lambda i: (0, 0))],
        out_shape=[jax.ShapeDtypeStruct((n, D_MODEL), F32), jax.ShapeDtypeStruct((n, LANES), F32),
                   jax.ShapeDtypeStruct((8, LANES), F32)],
        scratch_shapes=[pltpu.VMEM((8, LANES), F32), pltpu.VMEM((tm, tm), BF16)],
        compiler_params=_params(),
        name="out_router",
    )(mix_p, x_p, mix_s, x_s, wout_bf, g2, rw_bf, rb)


def _positions_body(meta_ref, off_ref, pos_ref):
    meta = meta_ref[...]
    lane = lax.broadcasted_iota(jnp.int32, meta.shape, 1)
    lane_f = lane.astype(F32)
    off = off_ref[...]
    pos = jnp.zeros(meta.shape, F32)
    for slot, (e_lane, r_lane) in enumerate(((META_E0, META_R0), (META_E1, META_R1))):
        e = meta[:, e_lane:e_lane + 1] + ROUTER_LANE0
        seg = jnp.sum(jnp.where(lane_f == e, off, 0.0), axis=-1, keepdims=True)
        pos = jnp.where(lane == slot, seg + meta[:, r_lane:r_lane + 1], pos)
    pos_ref[...] = pos.astype(jnp.int32)


def _positions(meta_all, off_lanes):
    n = meta_all.shape[0]
    tm = TM_PROJ
    assert n % tm == 0
    return pl.pallas_call(
        _positions_body,
        grid=(n // tm,),
        in_specs=[pl.BlockSpec((tm, LANES), lambda i: (i, 0)), _resident((1, LANES))],
        out_specs=pl.BlockSpec((tm, LANES), lambda i: (i, 0)),
        out_shape=jax.ShapeDtypeStruct((n, LANES), jnp.int32),
        compiler_params=_params(),
        name="positions",
    )(meta_all, off_lanes)


N_LOAD_SLOTS = 3


def _pow2_sizes(n):
    return [1 << b for b in range(n.bit_length() - 1, -1, -1)]


def _dispatch_body(pos_ref, nv_ref, h_ref, hs_ref, hbuf, zbuf, lsem, ssem, zsem, *, n_tok, tm):
    i = pl.program_id(0)
    n_steps = pl.num_programs(0)
    n_tiles = nv_ref.shape[0]

    groups = tm // SUBLANES

    def load(t, slot, sem=lsem):
        return pltpu.make_async_copy(h_ref.at[pl.ds(t * groups, groups)], hbuf.at[slot], sem.at[slot])

    def clear_unowned(start):
        def per_tile(t, c):
            nv = nv_ref[t]
            z = TR - nv
            row = t * TR + nv
            head = z & (SUBLANES - 1)
            for k in range(SUBLANES - 1):
                @pl.when(k < head)
                def _(k=k):
                    cp = pltpu.make_async_copy(zbuf.at[pl.ds(0, 1), :], hs_ref.at[pl.ds(row + k, 1), :], zsem)
                    cp.start() if start else cp.wait()
            row = row + head
            for size in _pow2_sizes(TR):
                if size < SUBLANES:
                    break
                @pl.when((z & size) != 0)
                def _(row=row, size=size):
                    dst = hs_ref.at[pl.ds(pl.multiple_of(row, SUBLANES), size), :]
                    cp = pltpu.make_async_copy(zbuf.at[pl.ds(0, size), :], dst, zsem)
                    cp.start() if start else cp.wait()
                row = row + (z & size)
            return c
        lax.fori_loop(0, n_tiles, per_tile, 0)

    def scatter_wait(slot):
        for _ in range(2):
            load(0, slot, ssem).wait()

    @pl.when(i == 0)
    def _():
        zbuf[...] = jnp.zeros_like(zbuf)
        clear_unowned(True)
        load(0, 0).start()

    @pl.when(i + 1 < n_steps)
    def _():
        load(i + 1, (i + 1) % N_LOAD_SLOTS).start()

    slot = i % N_LOAD_SLOTS
    load(i, slot).wait()

    def issue(jb, c):
        for u in range(SUBLANES):
            for s in range(2):
                p = pos_ref[s * n_tok + i * tm + jb * SUBLANES + u]
                pltpu.make_async_copy(hbuf.at[slot, jb, pl.ds(u, 1), :], hs_ref.at[pl.ds(p, 1), :],
                                      ssem.at[slot]).start()
        return c
    lax.fori_loop(0, groups, issue, 0)

    @pl.when(i > 0)
    def _():
        scatter_wait((i + N_LOAD_SLOTS - 1) % N_LOAD_SLOTS)

    @pl.when(i == n_steps - 1)
    def _():
        scatter_wait(slot)
        clear_unowned(False)


def _dispatch(pos, nv, rows):
    n_tok, width = rows.shape
    n_tiles = nv.shape[0]
    tm = TM_PROJ
    assert n_tok % tm == 0
    return pl.pallas_call(
        functools.partial(_dispatch_body, n_tok=n_tok, tm=tm),
        grid_spec=pltpu.PrefetchScalarGridSpec(
            num_scalar_prefetch=2,
            grid=(n_tok // tm,),
            in_specs=[pl.BlockSpec(memory_space=pl.ANY)],
            out_specs=pl.BlockSpec(memory_space=pl.ANY),
            scratch_shapes=[pltpu.VMEM((N_LOAD_SLOTS, tm // SUBLANES, SUBLANES, width), rows.dtype),
                            pltpu.VMEM((TR, width), rows.dtype),
                            pltpu.SemaphoreType.DMA((N_LOAD_SLOTS,)), pltpu.SemaphoreType.DMA((N_LOAD_SLOTS,)),
                            pltpu.SemaphoreType.DMA(())],
        ),
        out_shape=jax.ShapeDtypeStruct((n_tiles * TR, width), rows.dtype),
        compiler_params=_params(),
        name="dispatch",
    )(pos, nv, rows.reshape(n_tok // SUBLANES, SUBLANES, width))


N_MOE_LOADS = 3


def _moe_body(ts_ref, hs_ref, g2_ref, wg_ref, wu_ref, wd_ref, ys_ref, xbuf, obuf, wg_bf, wu_bf, wd_bf, lsem, ssem):
    e = pl.program_id(0)
    n_tiles = ys_ref.shape[0] // TR
    n_used = ts_ref[N_EXPERTS]
    t0, t1 = ts_ref[e], ts_ref[e + 1]

    def load(g, slot):
        return pltpu.make_async_copy(hs_ref.at[pl.ds(pl.multiple_of(g * TR, TR), TR), :], xbuf.at[slot], lsem.at[slot])

    def store(g, slot):
        return pltpu.make_async_copy(obuf.at[slot], ys_ref.at[pl.ds(pl.multiple_of(g * TR, TR), TR), :], ssem.at[slot])

    @pl.when(e == 0)
    def _():
        for g in range(N_MOE_LOADS - 1):
            @pl.when(g < n_used)
            def _(g=g):
                load(g, g).start()

    @pl.when(t1 > t0)
    def _():
        wg_bf[...] = wg_ref[0].astype(BF16)
        wu_bf[...] = wu_ref[0].astype(BF16)
        wd_bf[...] = wd_ref[0].astype(BF16)

    def tile(g, c):
        slot = g % 2
        xslot = g % N_MOE_LOADS
        ahead = g + N_MOE_LOADS - 1

        @pl.when(ahead < n_used)
        def _():
            load(ahead, ahead % N_MOE_LOADS).start()

        load(g, xslot).wait()
        xn = _rmsnorm(xbuf[xslot], g2_ref[...]).astype(BF16)
        hg = _dot(xn, wg_bf[...])
        hu = _dot(xn, wu_bf[...])
        hid = (hg * jax.nn.sigmoid(hg) * hu).astype(BF16)
        y = _dot(hid, wd_bf[...])

        @pl.when(g >= 2)
        def _():
            store(g - 2, slot).wait()

        obuf[slot] = y
        store(g, slot).start()
        return c

    lax.fori_loop(t0, t1, tile, 0)

    @pl.when(e == pl.num_programs(0) - 1)
    def _():
        @pl.when(n_used >= 2)
        def _():
            store(n_used - 2, n_used % 2).wait()
        store(n_used - 1, (n_used - 1) % 2).wait()
        obuf[0] = jnp.zeros(obuf.shape[1:], obuf.dtype)

        def clear(g, c):
            store(g, 0).start()
            return c
        lax.fori_loop(n_used, n_tiles, clear, 0)

        def drain(g, c):
            store(g, 0).wait()
            return c
        lax.fori_loop(n_used, n_tiles, drain, 0)


def _moe(ts, hs, g2, wg, wu, wd):
    wspec = lambda shape: pl.BlockSpec((1,) + shape, lambda e, ts: (e, 0, 0))
    tile_buf = lambda slots: pltpu.VMEM((slots, TR, hs.shape[1]), hs.dtype)
    return pl.pallas_call(
        _moe_body,
        grid_spec=pltpu.PrefetchScalarGridSpec(
            num_scalar_prefetch=1,
            grid=(N_EXPERTS,),
            in_specs=[pl.BlockSpec(memory_space=pl.ANY),
                      pl.BlockSpec((1, D_MODEL), lambda e, ts: (0, 0)),
                      wspec((D_MODEL, D_EXPERT)), wspec((D_MODEL, D_EXPERT)), wspec((D_EXPERT, D_MODEL))],
            out_specs=pl.BlockSpec(memory_space=pl.ANY),
            scratch_shapes=[tile_buf(N_MOE_LOADS), tile_buf(2),
                            pltpu.VMEM((D_MODEL, D_EXPERT), BF16), pltpu.VMEM((D_MODEL, D_EXPERT), BF16),
                            pltpu.VMEM((D_EXPERT, D_MODEL), BF16),
                            pltpu.SemaphoreType.DMA((N_MOE_LOADS,)), pltpu.SemaphoreType.DMA((2,))],
        ),
        out_shape=jax.ShapeDtypeStruct(hs.shape, hs.dtype),
        compiler_params=_params(),
        name="moe",
    )(ts, hs, g2, wg, wu, wd)


def _combine_body(pos_ref, h_ref, meta_ref, g_ref, ys_ref, ys_grouped_ref, o_ref, ybuf, sem, *, n_tok, row0):
    i = pl.program_id(0)
    n_steps = pl.num_programs(0)
    tm = h_ref.shape[0]

    groups = tm // SUBLANES

    def gather(t, slot):
        def issue(jb, c):
            for u in range(SUBLANES):
                for s in range(2):
                    p = pos_ref[s * n_tok + row0 + t * tm + jb * SUBLANES + u]
                    pltpu.make_async_copy(ys_ref.at[pl.ds(p, 1), :], ybuf.at[slot, s, jb, pl.ds(u, 1), :],
                                          sem.at[slot]).start()
            return c
        lax.fori_loop(0, groups, issue, 0)

    @pl.when(i == 0)
    def _():
        gather(0, 0)

    @pl.when(i + 1 < n_steps)
    def _():
        gather(i + 1, (i + 1) % 2)

    slot = i % 2
    for s in range(2):
        pltpu.make_async_copy(ys_grouped_ref.at[pl.ds(0, groups)], ybuf.at[slot, s], sem.at[slot]).wait()
    meta = meta_ref[...]
    g0 = meta[:, META_G0:META_G0 + 1]
    g1 = meta[:, META_G1:META_G1 + 1]
    y0 = ybuf[slot, 0].reshape(tm, D_MODEL)
    y1 = ybuf[slot, 1].reshape(tm, D_MODEL)
    h = h_ref[...] + (g0 * y0 + g1 * y1)
    o_ref[...] = _rmsnorm(h, g_ref[...])


def _combine(pos, h_all, meta_all, ys, gf, row0, n):
    n_tok = h_all.shape[0]
    tm = min(TM_OUT, n)
    assert row0 % tm == 0 and n % tm == 0
    b0 = row0 // tm
    return pl.pallas_call(
        functools.partial(_combine_body, n_tok=n_tok, row0=row0),
        grid_spec=pltpu.PrefetchScalarGridSpec(
            num_scalar_prefetch=1,
            grid=(n // tm,),
            in_specs=[pl.BlockSpec((tm, D_MODEL), lambda i, pos: (b0 + i, 0)),
                      pl.BlockSpec((tm, LANES), lambda i, pos: (b0 + i, 0)),
                      pl.BlockSpec((1, D_MODEL), lambda i, pos: (0, 0)),
                      pl.BlockSpec(memory_space=pl.ANY), pl.BlockSpec(memory_space=pl.ANY)],
            out_specs=pl.BlockSpec((tm, D_MODEL), lambda i, pos: (i, 0)),
            scratch_shapes=[pltpu.VMEM((2, 2, tm // SUBLANES, SUBLANES, ys.shape[1]), ys.dtype),
                            pltpu.SemaphoreType.DMA((2,))],
        ),
        out_shape=jax.ShapeDtypeStruct((n, D_MODEL), F32),
        compiler_params=_params(),
        name="combine",
    )(pos, h_all, meta_all, gf, ys, ys.reshape(ys.shape[0] // SUBLANES, SUBLANES, ys.shape[1]))


def _routing_tables(counts, n_tok):
    n_tiles = (2 * n_tok) // TR + N_EXPERTS
    cnt = counts[ROUTER_LANE0:ROUTER_LANE0 + N_EXPERTS].astype(jnp.int32)
    tiles_e = (cnt + TR - 1) // TR
    tile_end = jnp.cumsum(tiles_e)
    tile_start = tile_end - tiles_e
    off_lanes = jnp.pad((tile_start * TR).astype(F32), (ROUTER_LANE0, LANES - ROUTER_LANE0 - N_EXPERTS))
    ts = jnp.concatenate([tile_start, tile_end[-1:]]).astype(jnp.int32)
    tile = jnp.arange(n_tiles, dtype=jnp.int32)
    owner = (tile[:, None] >= tile_start[None, :]) & (tile[:, None] < tile_end[None, :])
    rows_left = jnp.sum(jnp.where(owner, cnt[None, :] - (tile[:, None] - tile_start[None, :]) * TR, 0), axis=1)
    nv = jnp.clip(rows_left, 0, TR).astype(jnp.int32)
    return off_lanes.reshape(1, LANES), ts, nv


def kernel(x_prompt, x_sample, cache_win_k, cache_win_v, state_conv, norm1_g, w_in, attn_sink, conv_dw_w, conv_dw_b,
           conv_ln_g, conv_ln_b, conv_pw_w, conv_pw_b, w_out, norm2_g, router_group_w, router_group_b,
           router_expert_w, router_expert_b, expert_w_gate, expert_w_up, expert_w_down, final_norm_g):
    depth = w_in.shape[0]
    assert depth == 1, "single-layer step"
    bp, sp, _ = x_prompt.shape
    assert bp == 1, "one prompt sequence"
    n_seq, t_new, _ = x_sample.shape
    n_p, n_s = bp * sp, n_seq * t_new
    n_tok = n_p + n_s
    l = 0

    row = lambda a: a.reshape(1, -1)
    w_in_bf = w_in[l].astype(BF16)
    w_out_bf = w_out[l].astype(BF16)
    pw_bf = conv_pw_w[l].astype(BF16)
    rw = jnp.concatenate([router_group_w[l], router_expert_w[l]], axis=1)
    rw_bf = jnp.pad(rw, ((0, 0), (0, LANES - rw.shape[1]))).astype(BF16)
    rb = jnp.pad(jnp.concatenate([router_group_b[l], router_expert_b[l]]), (0, LANES - rw.shape[1])).reshape(1, LANES)
    g1, g2, gf = row(norm1_g[l]), row(norm2_g[l]), row(final_norm_g)
    sink = attn_sink[l]
    conv_w = (conv_dw_w[l], row(conv_dw_b[l]), row(conv_ln_g[l]), row(conv_ln_b[l]), pw_bf, row(conv_pw_b[l]))

    xp = x_prompt.reshape(n_p, D_MODEL)
    xs = x_sample.reshape(n_s, D_MODEL)
    cache_axes, cache_axes_back = (0, 1, 3, 4, 2), (0, 1, 4, 2, 3)
    ck = jnp.transpose(cache_win_k, cache_axes)
    cv = jnp.transpose(cache_win_v, cache_axes)
    cs = jnp.transpose(state_conv, (0, 2, 1, 3))

    qp, kp, vp, up = _in_proj(xp, g1, w_in_bf)
    qs, ks, vs, us = _in_proj(xs, g1, w_in_bf)
    mix_p = _mixer_prompt(sink, qp, kp, vp, up, *conv_w)
    mix_s, wk_s, wv_s, wc_s = _mixer_sample(sink, qs, ks, vs, us, ck, cv, cs, *conv_w)

    h_all, meta_all, cnt = _out_router(mix_p, xp, mix_s, xs, w_out_bf, g2, rw_bf, rb)

    off_lanes, ts, nv = _routing_tables(cnt[0], n_tok)
    pos_lanes = _positions(meta_all, off_lanes)
    pos = jnp.concatenate([pos_lanes[:, 0], pos_lanes[:, 1]])
    hs = _dispatch(pos, nv, h_all)
    ys = _moe(ts, hs, g2, expert_w_gate[l], expert_w_up[l], expert_w_down[l])
    y_p = _combine(pos, h_all, meta_all, ys, gf, 0, n_p)
    y_s = _combine(pos, h_all, meta_all, ys, gf, n_p, n_s)

    kv_shape = (depth, bp, WINDOW, N_KV_HEADS, HEAD_DIM)
    return (y_p.reshape(bp, sp, D_MODEL), y_s.reshape(n_seq, t_new, D_MODEL),
            kp[n_p - WINDOW:].reshape(kv_shape), vp[n_p - WINDOW:].reshape(kv_shape),
            up[n_p - (CONV_W - 1):].reshape(depth, bp, CONV_W - 1, D_CONV),
            jnp.transpose(wk_s, cache_axes_back), jnp.transpose(wv_s, cache_axes_back),
            jnp.transpose(wc_s, (0, 2, 1, 3)))
```

```python
import functools

import jax
import jax.numpy as jnp
from jax import lax
from jax.experimental import pallas as pl
from jax.experimental.pallas import tpu as pltpu

F32 = jnp.float32
BF16 = jnp.bfloat16

D_MODEL = 2048
HEAD_DIM = 64
N_HEADS = 16
N_KV_HEADS = 2
GQA_GROUP = 8
KV_DIM = N_KV_HEADS * HEAD_DIM
D_ATTN = N_HEADS * HEAD_DIM
D_CONV = D_MODEL - D_ATTN
WINDOW = 128
CONV_W = 31
D_IN = D_ATTN + 2 * KV_DIM + 2 * D_CONV
N_GROUPS = 4
EXPERTS_PER_GROUP = 8
N_EXPERTS = N_GROUPS * EXPERTS_PER_GROUP
D_EXPERT = 256
RMS_EPS = 1e-6
LN_EPS = 1e-5

LANES = 128
SUBLANES = 8
ROUTER_LANE0 = N_GROUPS
VMEM_LIMIT = 56 * 1024 * 1024

TM_PROJ = 512
TM_MIX = 256
SEQ_TILE = 8
TR = 128
TM_OUT = 256

_NT = (((1,), (1,)), ((), ()))


def _params(n_axes=1):
    return pltpu.CompilerParams(dimension_semantics=("arbitrary",) * n_axes, vmem_limit_bytes=VMEM_LIMIT)


def _resident(shape):
    return pl.BlockSpec(shape, lambda *_: (0,) * len(shape), pipeline_mode=pl.Buffered(1))


def _rmsnorm(x, g):
    ms = jnp.mean(x * x, axis=-1, keepdims=True)
    return x * lax.rsqrt(ms + RMS_EPS) * g


def _dot(a, b):
    return jnp.dot(a, b, preferred_element_type=F32)


def _dot_nt(a, b):
    return lax.dot_general(a, b, _NT, preferred_element_type=F32)


def _in_proj_body(x_ref, g_ref, w_ref, q_ref, k_ref, v_ref, u_ref):
    xn = _rmsnorm(x_ref[...], g_ref[...]).astype(BF16)
    q_ref[...] = (_dot(xn, w_ref[:, :D_ATTN]) * (HEAD_DIM ** -0.5)).astype(BF16)
    kv = _dot(xn, w_ref[:, D_ATTN:D_ATTN + 2 * KV_DIM])
    k_ref[...] = kv[:, :KV_DIM]
    v_ref[...] = kv[:, KV_DIM:]
    c0 = D_ATTN + 2 * KV_DIM
    cw = 256
    for j in range(D_CONV // cw):
        a = _dot(xn, w_ref[:, c0 + j * cw:c0 + (j + 1) * cw])
        b = _dot(xn, w_ref[:, c0 + D_CONV + j * cw:c0 + D_CONV + (j + 1) * cw])
        u_ref[:, j * cw:(j + 1) * cw] = a * jax.nn.sigmoid(b)


def _in_proj(x, g, w_bf):
    n = x.shape[0]
    tm = min(TM_PROJ, n)
    row = lambda w: pl.BlockSpec((tm, w), lambda i: (i, 0))
    return pl.pallas_call(
        _in_proj_body,
        grid=(n // tm,),
        in_specs=[row(D_MODEL), _resident((1, D_MODEL)), _resident((D_MODEL, D_IN))],
        out_specs=[row(D_ATTN), row(KV_DIM), row(KV_DIM), row(D_CONV)],
        out_shape=[jax.ShapeDtypeStruct((n, D_ATTN), BF16), jax.ShapeDtypeStruct((n, KV_DIM), F32),
                   jax.ShapeDtypeStruct((n, KV_DIM), F32), jax.ShapeDtypeStruct((n, D_CONV), F32)],
        compiler_params=_params(),
        name="in_proj",
    )(x, g, w_bf)


def _attend(q, parts, sink_ref, kv_head, keys_on_lanes=False):
    t = q.shape[0]
    score, weigh = (_dot, _dot_nt) if keys_on_lanes else (_dot_nt, _dot)
    qs = jnp.concatenate([q[:, g * HEAD_DIM:(g + 1) * HEAD_DIM] for g in range(GQA_GROUP)], axis=0)
    scores = [score(qs, kk) for kk, _, _ in parts]
    ps = [[] for _ in parts]
    inv = []
    for g in range(GQA_GROUP):
        sg = [jnp.where(mask, s[g * t:(g + 1) * t], -jnp.inf) for s, (_, _, mask) in zip(scores, parts)]
        sk = sink_ref[kv_head * GQA_GROUP + g]
        m = sk
        for x in sg:
            m = jnp.maximum(jnp.max(x, axis=-1, keepdims=True), m)
        den = jnp.exp(sk - m)
        for k, x in enumerate(sg):
            p = jnp.exp(x - m)
            den = den + jnp.sum(p, axis=-1, keepdims=True)
            ps[k].append(p.astype(BF16))
        inv.append(1.0 / den)
    o = None
    for k, (_, vv, _) in enumerate(parts):
        ok = weigh(jnp.concatenate(ps[k], axis=0), vv)
        o = ok if o is None else o + ok
    return jnp.concatenate([o[g * t:(g + 1) * t] * inv[g] for g in range(GQA_GROUP)], axis=1)


def _conv_tail(y, dwb_ref, lng_ref, lnb_ref, pw_ref, pwb_ref):
    y = y + dwb_ref[...]
    mu = jnp.mean(y, axis=-1, keepdims=True)
    yc = y - mu
    yn = yc * lax.rsqrt(jnp.mean(yc * yc, axis=-1, keepdims=True) + LN_EPS)
    yn = yn * lng_ref[...] + lnb_ref[...]
    act = yn * jax.nn.sigmoid(yn)
    return _dot(act.astype(BF16), pw_ref[...]) + pwb_ref[...]


CONV_PAD = 32
CONV_ROWS = 32
CONV_COLS = 256


def _mixer_prompt_body(sink_ref, q_ref, k_ref, v_ref, u_ref, dww_ref, dwb_ref, lng_ref, lnb_ref, pw_ref, pwb_ref,
                       mix_ref, kprev, vprev, uext, ushift, ybuf):
    i = pl.program_id(0)
    tm = q_ref.shape[0]

    @pl.when(i == 0)
    def _():
        kprev[...] = jnp.zeros_like(kprev)
        vprev[...] = jnp.zeros_like(vprev)
        uext[0:CONV_PAD, :] = jnp.zeros((CONV_PAD, D_CONV), F32)

    qi = lax.broadcasted_iota(jnp.int32, (WINDOW, 2 * WINDOW), 0)
    kj = lax.broadcasted_iota(jnp.int32, (WINDOW, 2 * WINDOW), 1)
    diff = qi + WINDOW - kj
    band = (diff >= 0) & (diff <= WINDOW)
    kp, vp = kprev[...], vprev[...]
    for b in range(tm // WINDOW):
        rows = slice(b * WINDOW, (b + 1) * WINDOW)
        kb = k_ref[rows, :].astype(BF16)
        vb = v_ref[rows, :].astype(BF16)
        kk = jnp.concatenate([kp, kb], axis=0)
        vv = jnp.concatenate([vp, vb], axis=0)
        if b == 0:
            mask = band & (kj >= jnp.where(i > 0, 0, WINDOW))
        else:
            mask = band
        for h in range(N_KV_HEADS):
            cols = slice(h * HEAD_DIM, (h + 1) * HEAD_DIM)
            hq = slice(h * GQA_GROUP * HEAD_DIM, (h + 1) * GQA_GROUP * HEAD_DIM)
            o = _attend(q_ref[rows, hq], [(kk[:, cols], vv[:, cols], mask)], sink_ref, h)
            mix_ref[rows, hq] = o.astype(BF16)
        kp, vp = kb, vb
    kprev[...] = kp
    vprev[...] = vp

    uext[CONV_PAD:CONV_PAD + tm, :] = u_ref[...]
    n_shift_rows = ushift.shape[1]
    for s in range(1, SUBLANES):
        ushift[s - 1] = uext[s:s + n_shift_rows, :]
    off = CONV_PAD - (CONV_W - 1)
    for r in range(tm // CONV_ROWS):
        for c in range(D_CONV // CONV_COLS):
            cs = slice(c * CONV_COLS, (c + 1) * CONV_COLS)
            acc = jnp.zeros((CONV_ROWS, CONV_COLS), F32)
            for j in range(CONV_W):
                a, s = divmod(off + j, SUBLANES)
                r0 = r * CONV_ROWS + a * SUBLANES
                src = uext[r0:r0 + CONV_ROWS, cs] if s == 0 else ushift[s - 1, r0:r0 + CONV_ROWS, cs]
                acc = acc + dww_ref[j:j + 1, cs] * src
            ybuf[r * CONV_ROWS:(r + 1) * CONV_ROWS, cs] = acc
    uext[0:CONV_PAD, :] = uext[tm:tm + CONV_PAD, :]
    conv = _conv_tail(ybuf[...], dwb_ref, lng_ref, lnb_ref, pw_ref, pwb_ref)
    mix_ref[:, D_ATTN:] = conv.astype(BF16)


def _mixer_prompt(sink, q, k, v, u, dww, dwb, lng, lnb, pw_bf, pwb):
    n = q.shape[0]
    tm = TM_MIX
    row = lambda w: pl.BlockSpec((tm, w), lambda i, s: (i, 0))
    res = lambda shape: pl.BlockSpec(shape, lambda i, s: (0,) * len(shape), pipeline_mode=pl.Buffered(1))
    return pl.pallas_call(
        _mixer_prompt_body,
        grid_spec=pltpu.PrefetchScalarGridSpec(
            num_scalar_prefetch=1,
            grid=(n // tm,),
            in_specs=[row(D_ATTN), row(KV_DIM), row(KV_DIM), row(D_CONV), res((CONV_W, D_CONV)), res((1, D_CONV)),
                      res((1, D_CONV)), res((1, D_CONV)), res((D_CONV, D_CONV)), res((1, D_CONV))],
            out_specs=row(D_MODEL),
            scratch_shapes=[pltpu.VMEM((WINDOW, KV_DIM), BF16), pltpu.VMEM((WINDOW, KV_DIM), BF16),
                            pltpu.VMEM((CONV_PAD + tm, D_CONV), F32),
                            pltpu.VMEM((SUBLANES - 1, CONV_PAD - SUBLANES + tm, D_CONV), F32),
                            pltpu.VMEM((tm, D_CONV), F32)],
        ),
        out_shape=jax.ShapeDtypeStruct((n, D_MODEL), BF16),
        compiler_params=_params(),
        name="mixer_prompt",
    )(sink, q, k, v, u, dww, dwb, lng, lnb, pw_bf, pwb)


def _mixer_sample_body(sink_ref, q_ref, k_ref, v_ref, u_ref, ck_ref, cv_ref, cs_ref, dww_ref, dwb_ref, lng_ref,
                       lnb_ref, pw_ref, pwb_ref, mix_ref, wk_ref, wv_ref, wc_ref, usel, ybuf):
    nt = q_ref.shape[0]
    sb = ck_ref.shape[0]
    t_new = nt // sb
    n_cache = sb * WINDOW
    n_state = CONV_W - 1

    pad = jnp.zeros((LANES - nt, KV_DIM), F32)
    kn_t = jnp.concatenate([k_ref[...], pad], axis=0).T
    vn_t = jnp.concatenate([v_ref[...], pad], axis=0).T

    lt, lw = t_new.bit_length() - 1, WINDOW.bit_length() - 1
    qr = lax.broadcasted_iota(jnp.int32, (nt, n_cache), 0)
    cc = lax.broadcasted_iota(jnp.int32, (nt, n_cache), 1)
    mask_c = ((cc >> lw) == (qr >> lt)) & ((cc & (WINDOW - 1)) >= (qr & (t_new - 1)))
    qr = lax.broadcasted_iota(jnp.int32, (nt, LANES), 0)
    cn = lax.broadcasted_iota(jnp.int32, (nt, LANES), 1)
    mask_n = ((cn >> lt) == (qr >> lt)) & ((cn & (t_new - 1)) <= (qr & (t_new - 1)))
    for h in range(N_KV_HEADS):
        dims = slice(h * HEAD_DIM, (h + 1) * HEAD_DIM)
        hq = slice(h * GQA_GROUP * HEAD_DIM, (h + 1) * GQA_GROUP * HEAD_DIM)
        kc = jnp.concatenate([ck_ref[s, h] for s in range(sb)], axis=1).astype(BF16)
        vc = jnp.concatenate([cv_ref[s, h] for s in range(sb)], axis=1).astype(BF16)
        parts = [(kc, vc, mask_c), (kn_t[dims, :].astype(BF16), vn_t[dims, :].astype(BF16), mask_n)]
        o = _attend(q_ref[:, hq], parts, sink_ref, h, keys_on_lanes=True)
        mix_ref[:, hq] = o.astype(BF16)

    is_new = lax.broadcasted_iota(jnp.int32, (HEAD_DIM, WINDOW), 1) >= WINDOW - t_new
    for s in range(sb):
        shift = (WINDOW - t_new - s * t_new) % LANES
        kn_s = pltpu.roll(kn_t, shift=shift, axis=1)
        vn_s = pltpu.roll(vn_t, shift=shift, axis=1)
        for h in range(N_KV_HEADS):
            dims = slice(h * HEAD_DIM, (h + 1) * HEAD_DIM)
            old_k = pltpu.roll(ck_ref[s, h], shift=WINDOW - t_new, axis=1)
            old_v = pltpu.roll(cv_ref[s, h], shift=WINDOW - t_new, axis=1)
            wk_ref[s, h] = jnp.where(is_new, kn_s[dims, :], old_k)
            wv_ref[s, h] = jnp.where(is_new, vn_s[dims, :], old_v)

    for s in range(sb):
        for t in range(t_new):
            usel[t, s:s + 1, :] = u_ref[s * t_new + t:s * t_new + t + 1, :]
    u_new = [usel[t] for t in range(t_new)]
    for t in range(t_new):
        acc = jnp.zeros((sb, D_CONV), F32)
        for r in range(t, n_state):
            acc = acc + dww_ref[r - t:r - t + 1, :] * cs_ref[r]
        for t2 in range(t + 1):
            j = n_state - t + t2
            acc = acc + dww_ref[j:j + 1, :] * u_new[t2]
        for s in range(sb):
            ybuf[s * t_new + t:s * t_new + t + 1, :] = acc[s:s + 1, :]
        wc_ref[n_state - t_new + t] = u_new[t]
    wc_ref[0:n_state - t_new] = cs_ref[t_new:n_state]
    conv = _conv_tail(ybuf[...], dwb_ref, lng_ref, lnb_ref, pw_ref, pwb_ref)
    mix_ref[:, D_ATTN:] = conv.astype(BF16)


def _mixer_sample(sink, q, k, v, u, ck, cv, cs, dww, dwb, lng, lnb, pw_bf, pwb):
    n = q.shape[0]
    n_seq = ck.shape[1]
    t_new = n // n_seq
    sb = SEQ_TILE
    nt = sb * t_new
    n_state = CONV_W - 1
    row = lambda w: pl.BlockSpec((nt, w), lambda i, s: (i, 0))
    cache = pl.BlockSpec((None, sb, N_KV_HEADS, HEAD_DIM, WINDOW), lambda i, s: (0, i, 0, 0, 0))
    state = pl.BlockSpec((None, n_state, sb, D_CONV), lambda i, s: (0, 0, i, 0))
    res = lambda shape: pl.BlockSpec(shape, lambda i, s: (0,) * len(shape), pipeline_mode=pl.Buffered(1))
    return pl.pallas_call(
        _mixer_sample_body,
        grid_spec=pltpu.PrefetchScalarGridSpec(
            num_scalar_prefetch=1,
            grid=(n_seq // sb,),
            in_specs=[row(D_ATTN), row(KV_DIM), row(KV_DIM), row(D_CONV), cache, cache, state,
                      res((CONV_W, D_CONV)), res((1, D_CONV)), res((1, D_CONV)), res((1, D_CONV)),
                      res((D_CONV, D_CONV)), res((1, D_CONV))],
            out_specs=[row(D_MODEL), cache, cache, state],
            scratch_shapes=[pltpu.VMEM((t_new, sb, D_CONV), F32), pltpu.VMEM((nt, D_CONV), F32)],
        ),
        out_shape=[jax.ShapeDtypeStruct((n, D_MODEL), BF16), jax.ShapeDtypeStruct(ck.shape, F32),
                   jax.ShapeDtypeStruct(cv.shape, F32), jax.ShapeDtypeStruct(cs.shape, F32)],
        compiler_params=_params(),
        name="mixer_sample",
    )(sink, q, k, v, u, ck, cv, cs, dww, dwb, lng, lnb, pw_bf, pwb)


META_E0, META_E1, META_G0, META_G1, META_R0, META_R1 = range(6)


def _out_router_body(mixp_ref, xp_ref, mixs_ref, xs_ref, wout_ref, g2_ref, rw_ref, rb_ref, h_ref, meta_ref, cnt_ref,
                     carry, lower, *, prompt_tiles):
    i = pl.program_id(0)
    tm = xp_ref.shape[0]

    @pl.when(i == 0)
    def _():
        carry[...] = jnp.zeros_like(carry)
        ti = lax.broadcasted_iota(jnp.int32, (tm, tm), 0)
        tj = lax.broadcasted_iota(jnp.int32, (tm, tm), 1)
        lower[...] = jnp.where(tj < ti, 1.0, 0.0).astype(BF16)

    is_prompt = i < prompt_tiles
    x = jnp.where(is_prompt, xp_ref[...], xs_ref[...])
    mix = jnp.where(is_prompt, mixp_ref[...], mixs_ref[...])
    h = x + _dot(mix, wout_ref[...])
    h_ref[...] = h
    xn = _rmsnorm(h, g2_ref[...]).astype(BF16)
    logits = _dot(xn, rw_ref[...]) + rb_ref[...]
    lane = lax.broadcasted_iota(jnp.int32, logits.shape, 1).astype(F32)
    first = lambda cond: jnp.min(jnp.where(cond, lane, float(LANES)), axis=-1, keepdims=True)

    gl = jnp.where(lane < N_GROUPS, logits, -jnp.inf)
    gmax = jnp.max(gl, axis=-1, keepdims=True)
    gidx = first(gl == gmax)
    gval = 1.0 / jnp.sum(jnp.exp(gl - gmax), axis=-1, keepdims=True)

    lo = ROUTER_LANE0 + EXPERTS_PER_GROUP * gidx
    el = jnp.where((lane >= lo) & (lane < lo + EXPERTS_PER_GROUP), logits, -jnp.inf)
    m1 = jnp.max(el, axis=-1, keepdims=True)
    i1 = first(el == m1)
    el2 = jnp.where(lane == i1, -jnp.inf, el)
    m2 = jnp.max(el2, axis=-1, keepdims=True)
    i2 = first(el2 == m2)
    r = jnp.exp(m2 - m1)
    g0 = gval * (1.0 / (1.0 + r))
    g1 = gval * (r / (1.0 + r))

    sel0, sel1 = lane == i1, lane == i2
    onehot = jnp.where(sel0 | sel1, 1.0, 0.0)
    before = _dot(lower[...], onehot.astype(BF16)) + carry[0:1, :]
    r0 = jnp.sum(jnp.where(sel0, before, 0.0), axis=-1, keepdims=True)
    r1 = jnp.sum(jnp.where(sel1, before, 0.0), axis=-1, keepdims=True)
    carry[...] = carry[...] + jnp.sum(onehot, axis=0, keepdims=True)
    cnt_ref[...] = carry[...]

    lane_i = lax.broadcasted_iota(jnp.int32, logits.shape, 1)
    meta = jnp.zeros(logits.shape, F32)
    for slot, val in ((META_E0, i1 - ROUTER_LANE0), (META_E1, i2 - ROUTER_LANE0), (META_G0, g0), (META_G1, g1),
                      (META_R0, r0), (META_R1, r1)):
        meta = jnp.where(lane_i == slot, val, meta)
    meta_ref[...] = meta


def _out_router(mix_p, x_p, mix_s, x_s, wout_bf, g2, rw_bf, rb):
    n_p, n_s = x_p.shape[0], x_s.shape[0]
    tm = TM_PROJ
    assert n_p % tm == 0 and n_s % tm == 0
    tp, ts = n_p // tm, n_s // tm
    n = n_p + n_s
    prow = lambda w: pl.BlockSpec((tm, w), lambda i: (jnp.minimum(i, tp - 1), 0))
    srow = lambda w: pl.BlockSpec((tm, w), lambda i: (jnp.maximum(i - tp, 0), 0))
    row = lambda w: pl.BlockSpec((tm, w), lambda i: (i, 0))
    return pl.pallas_call(
        functools.partial(_out_router_body, prompt_tiles=tp),
        grid=(tp + ts,),
        in_specs=[prow(D_MODEL), prow(D_MODEL), srow(D_MODEL), srow(D_MODEL), _resident((D_MODEL, D_MODEL)),
                  _resident((1, D_MODEL)), _resident((D_MODEL, LANES)), _resident((1, LANES))],
        out_specs=[row(D_MODEL), row(LANES), pl.BlockSpec((8, LANES), lambda i: (0, 0))],
        out_shape=[jax.ShapeDtypeStruct((n, D_MODEL), F32), jax.ShapeDtypeStruct((n, LANES), F32),
                   jax.ShapeDtypeStruct((8, LANES), F32)],
        scratch_shapes=[pltpu.VMEM((8, LANES), F32), pltpu.VMEM((tm, tm), BF16)],
        compiler_params=_params(),
        name="out_router",
    )(mix_p, x_p, mix_s, x_s, wout_bf, g2, rw_bf, rb)


def _positions_body(meta_ref, off_ref, pos_ref):
    meta = meta_ref[...]
    lane = lax.broadcasted_iota(jnp.int32, meta.shape, 1)
    lane_f = lane.astype(F32)
    off = off_ref[...]
    pos = jnp.zeros(meta.shape, F32)
    for slot, (e_lane, r_lane) in enumerate(((META_E0, META_R0), (META_E1, META_R1))):
        e = meta[:, e_lane:e_lane + 1] + ROUTER_LANE0
        seg = jnp.sum(jnp.where(lane_f == e, off, 0.0), axis=-1, keepdims=True)
        pos = jnp.where(lane == slot, seg + meta[:, r_lane:r_lane + 1], pos)
    pos_ref[...] = pos.astype(jnp.int32)


def _positions(meta_all, off_lanes):
    n = meta_all.shape[0]
    tm = TM_PROJ
    assert n % tm == 0
    return pl.pallas_call(
        _positions_body,
        grid=(n // tm,),
        in_specs=[pl.BlockSpec((tm, LANES), lambda i: (i, 0)), _resident((1, LANES))],
        out_specs=pl.BlockSpec((tm, LANES), lambda i: (i, 0)),
        out_shape=jax.ShapeDtypeStruct((n, LANES), jnp.int32),
        compiler_params=_params(),
        name="positions",
    )(meta_all, off_lanes)


N_LOAD_SLOTS = 3


def _pow2_sizes(n):
    return [1 << b for b in range(n.bit_length() - 1, -1, -1)]


def _dispatch_body(pos_ref, nv_ref, h_ref, hs_ref, hbuf, zbuf, lsem, ssem, zsem, *, n_tok, tm):
    i = pl.program_id(0)
    n_steps = pl.num_programs(0)
    n_tiles = nv_ref.shape[0]

    groups = tm // SUBLANES

    def load(t, slot, sem=lsem):
        return pltpu.make_async_copy(h_ref.at[pl.ds(t * groups, groups)], hbuf.at[slot], sem.at[slot])

    def clear_unowned(start):
        def per_tile(t, c):
            nv = nv_ref[t]
            z = TR - nv
            row = t * TR + nv
            head = z & (SUBLANES - 1)
            for k in range(SUBLANES - 1):
                @pl.when(k < head)
                def _(k=k):
                    cp = pltpu.make_async_copy(zbuf.at[pl.ds(0, 1), :], hs_ref.at[pl.ds(row + k, 1), :], zsem)
                    cp.start() if start else cp.wait()
            row = row + head
            for size in _pow2_sizes(TR):
                if size < SUBLANES:
                    break
                @pl.when((z & size) != 0)
                def _(row=row, size=size):
                    dst = hs_ref.at[pl.ds(pl.multiple_of(row, SUBLANES), size), :]
                    cp = pltpu.make_async_copy(zbuf.at[pl.ds(0, size), :], dst, zsem)
                    cp.start() if start else cp.wait()
                row = row + (z & size)
            return c
        lax.fori_loop(0, n_tiles, per_tile, 0)

    def scatter_wait(slot):
        for _ in range(2):
            load(0, slot, ssem).wait()

    @pl.when(i == 0)
    def _():
        zbuf[...] = jnp.zeros_like(zbuf)
        clear_unowned(True)
        load(0, 0).start()

    @pl.when(i + 1 < n_steps)
    def _():
        load(i + 1, (i + 1) % N_LOAD_SLOTS).start()

    slot = i % N_LOAD_SLOTS
    load(i, slot).wait()

    def issue(jb, c):
        for u in range(SUBLANES):
            for s in range(2):
                p = pos_ref[s * n_tok + i * tm + jb * SUBLANES + u]
                pltpu.make_async_copy(hbuf.at[slot, jb, pl.ds(u, 1), :], hs_ref.at[pl.ds(p, 1), :],
                                      ssem.at[slot]).start()
        return c
    lax.fori_loop(0, groups, issue, 0)

    @pl.when(i > 0)
    def _():
        scatter_wait((i + N_LOAD_SLOTS - 1) % N_LOAD_SLOTS)

    @pl.when(i == n_steps - 1)
    def _():
        scatter_wait(slot)
        clear_unowned(False)


def _dispatch(pos, nv, rows):
    n_tok, width = rows.shape
    n_tiles = nv.shape[0]
    tm = TM_PROJ
    assert n_tok % tm == 0
    return pl.pallas_call(
        functools.partial(_dispatch_body, n_tok=n_tok, tm=tm),
        grid_spec=pltpu.PrefetchScalarGridSpec(
            num_scalar_prefetch=2,
            grid=(n_tok // tm,),
            in_specs=[pl.BlockSpec(memory_space=pl.ANY)],
            out_specs=pl.BlockSpec(memory_space=pl.ANY),
            scratch_shapes=[pltpu.VMEM((N_LOAD_SLOTS, tm // SUBLANES, SUBLANES, width), rows.dtype),
                            pltpu.VMEM((TR, width), rows.dtype),
                            pltpu.SemaphoreType.DMA((N_LOAD_SLOTS,)), pltpu.SemaphoreType.DMA((N_LOAD_SLOTS,)),
                            pltpu.SemaphoreType.DMA(())],
        ),
        out_shape=jax.ShapeDtypeStruct((n_tiles * TR, width), rows.dtype),
        compiler_params=_params(),
        name="dispatch",
    )(pos, nv, rows.reshape(n_tok // SUBLANES, SUBLANES, width))


N_MOE_LOADS = 4


def _moe_body(ts_ref, hs_ref, g2_ref, wg_ref, wu_ref, wd_ref, ys_ref, xbuf, obuf, wg_bf, wu_bf, wd_bf, lsem, ssem):
    e = pl.program_id(0)
    n_tiles = ys_ref.shape[0] // TR
    n_used = ts_ref[N_EXPERTS]
    t0, t1 = ts_ref[e], ts_ref[e + 1]

    def load(g, slot):
        return pltpu.make_async_copy(hs_ref.at[pl.ds(pl.multiple_of(g * TR, TR), TR), :], xbuf.at[slot], lsem.at[slot])

    def store(g, slot):
        return pltpu.make_async_copy(obuf.at[slot], ys_ref.at[pl.ds(pl.multiple_of(g * TR, TR), TR), :], ssem.at[slot])

    @pl.when(e == 0)
    def _():
        for g in range(N_MOE_LOADS - 1):
            @pl.when(g < n_used)
            def _(g=g):
                load(g, g).start()

    @pl.when(t1 > t0)
    def _():
        wg_bf[...] = wg_ref[0].astype(BF16)
        wu_bf[...] = wu_ref[0].astype(BF16)
        wd_bf[...] = wd_ref[0].astype(BF16)

    def tile(g, c):
        slot = g % 2
        xslot = g % N_MOE_LOADS
        ahead = g + N_MOE_LOADS - 1

        @pl.when(ahead < n_used)
        def _():
            load(ahead, ahead % N_MOE_LOADS).start()

        load(g, xslot).wait()
        xn = _rmsnorm(xbuf[xslot], g2_ref[...]).astype(BF16)
        hg = _dot(xn, wg_bf[...])
        hu = _dot(xn, wu_bf[...])
        hid = (hg * jax.nn.sigmoid(hg) * hu).astype(BF16)
        y = _dot(hid, wd_bf[...])

        @pl.when(g >= 2)
        def _():
            store(g - 2, slot).wait()

        obuf[slot] = y
        store(g, slot).start()
        return c

    lax.fori_loop(t0, t1, tile, 0)

    @pl.when(e == pl.num_programs(0) - 1)
    def _():
        @pl.when(n_used >= 2)
        def _():
            store(n_used - 2, n_used % 2).wait()
        store(n_used - 1, (n_used - 1) % 2).wait()
        obuf[0] = jnp.zeros(obuf.shape[1:], obuf.dtype)

        def clear(g, c):
            store(g, 0).start()
            return c
        lax.fori_loop(n_used, n_tiles, clear, 0)

        def drain(g, c):
            store(g, 0).wait()
            return c
        lax.fori_loop(n_used, n_tiles, drain, 0)


def _moe(ts, hs, g2, wg, wu, wd):
    wspec = lambda shape: pl.BlockSpec((1,) + shape, lambda e, ts: (e, 0, 0))
    tile_buf = lambda slots: pltpu.VMEM((slots, TR, hs.shape[1]), hs.dtype)
    return pl.pallas_call(
        _moe_body,
        grid_spec=pltpu.PrefetchScalarGridSpec(
            num_scalar_prefetch=1,
            grid=(N_EXPERTS,),
            in_specs=[pl.BlockSpec(memory_space=pl.ANY),
                      pl.BlockSpec((1, D_MODEL), lambda e, ts: (0, 0)),
                      wspec((D_MODEL, D_EXPERT)), wspec((D_MODEL, D_EXPERT)), wspec((D_EXPERT, D_MODEL))],
            out_specs=pl.BlockSpec(memory_space=pl.ANY),
            scratch_shapes=[tile_buf(N_MOE_LOADS), tile_buf(2),
                            pltpu.VMEM((D_MODEL, D_EXPERT), BF16), pltpu.VMEM((D_MODEL, D_EXPERT), BF16),
                            pltpu.VMEM((D_EXPERT, D_MODEL), BF16),
                            pltpu.SemaphoreType.DMA((N_MOE_LOADS,)), pltpu.SemaphoreType.DMA((2,))],
        ),
        out_shape=jax.ShapeDtypeStruct(hs.shape, hs.dtype),
        compiler_params=_params(),
        name="moe",
    )(ts, hs, g2, wg, wu, wd)


def _combine_body(pos_ref, h_ref, meta_ref, g_ref, ys_ref, ys_grouped_ref, o_ref, ybuf, sem, *, n_tok, row0):
    i = pl.program_id(0)
    n_steps = pl.num_programs(0)
    tm = h_ref.shape[0]

    groups = tm // SUBLANES

    def gather(t, slot):
        def issue(jb, c):
            for u in range(SUBLANES):
                for s in range(2):
                    p = pos_ref[s * n_tok + row0 + t * tm + jb * SUBLANES + u]
                    pltpu.make_async_copy(ys_ref.at[pl.ds(p, 1), :], ybuf.at[slot, s, jb, pl.ds(u, 1), :],
                                          sem.at[slot]).start()
            return c
        lax.fori_loop(0, groups, issue, 0)

    @pl.when(i == 0)
    def _():
        gather(0, 0)

    @pl.when(i + 1 < n_steps)
    def _():
        gather(i + 1, (i + 1) % 2)

    slot = i % 2
    for s in range(2):
        pltpu.make_async_copy(ys_grouped_ref.at[pl.ds(0, groups)], ybuf.at[slot, s], sem.at[slot]).wait()
    meta = meta_ref[...]
    g0 = meta[:, META_G0:META_G0 + 1]
    g1 = meta[:, META_G1:META_G1 + 1]
    y0 = ybuf[slot, 0].reshape(tm, D_MODEL)
    y1 = ybuf[slot, 1].reshape(tm, D_MODEL)
    h = h_ref[...] + (g0 * y0 + g1 * y1)
    o_ref[...] = _rmsnorm(h, g_ref[...])


def _combine(pos, h_all, meta_all, ys, gf, row0, n):
    n_tok = h_all.shape[0]
    tm = min(TM_OUT, n)
    assert row0 % tm == 0 and n % tm == 0
    b0 = row0 // tm
    return pl.pallas_call(
        functools.partial(_combine_body, n_tok=n_tok, row0=row0),
        grid_spec=pltpu.PrefetchScalarGridSpec(
            num_scalar_prefetch=1,
            grid=(n // tm,),
            in_specs=[pl.BlockSpec((tm, D_MODEL), lambda i, pos: (b0 + i, 0)),
                      pl.BlockSpec((tm, LANES), lambda i, pos: (b0 + i, 0)),
                      pl.BlockSpec((1, D_MODEL), lambda i, pos: (0, 0)),
                      pl.BlockSpec(memory_space=pl.ANY), pl.BlockSpec(memory_space=pl.ANY)],
            out_specs=pl.BlockSpec((tm, D_MODEL), lambda i, pos: (i, 0)),
            scratch_shapes=[pltpu.VMEM((2, 2, tm // SUBLANES, SUBLANES, ys.shape[1]), ys.dtype),
                            pltpu.SemaphoreType.DMA((2,))],
        ),
        out_shape=jax.ShapeDtypeStruct((n, D_MODEL), F32),
        compiler_params=_params(),
        name="combine",
    )(pos, h_all, meta_all, gf, ys, ys.reshape(ys.shape[0] // SUBLANES, SUBLANES, ys.shape[1]))


def _routing_tables(counts, n_tok):
    n_tiles = (2 * n_tok) // TR + N_EXPERTS
    cnt = counts[ROUTER_LANE0:ROUTER_LANE0 + N_EXPERTS].astype(jnp.int32)
    tiles_e = (cnt + TR - 1) // TR
    tile_end = jnp.cumsum(tiles_e)
    tile_start = tile_end - tiles_e
    off_lanes = jnp.pad((tile_start * TR).astype(F32), (ROUTER_LANE0, LANES - ROUTER_LANE0 - N_EXPERTS))
    ts = jnp.concatenate([tile_start, tile_end[-1:]]).astype(jnp.int32)
    tile = jnp.arange(n_tiles, dtype=jnp.int32)
    owner = (tile[:, None] >= tile_start[None, :]) & (tile[:, None] < tile_end[None, :])
    rows_left = jnp.sum(jnp.where(owner, cnt[None, :] - (tile[:, None] - tile_start[None, :]) * TR, 0), axis=1)
    nv = jnp.clip(rows_left, 0, TR).astype(jnp.int32)
    return off_lanes.reshape(1, LANES), ts, nv


def kernel(x_prompt, x_sample, cache_win_k, cache_win_v, state_conv, norm1_g, w_in, attn_sink, conv_dw_w, conv_dw_b,
           conv_ln_g, conv_ln_b, conv_pw_w, conv_pw_b, w_out, norm2_g, router_group_w, router_group_b,
           router_expert_w, router_expert_b, expert_w_gate, expert_w_up, expert_w_down, final_norm_g):
    depth = w_in.shape[0]
    assert depth == 1, "single-layer step"
    bp, sp, _ = x_prompt.shape
    assert bp == 1, "one prompt sequence"
    n_seq, t_new, _ = x_sample.shape
    n_p, n_s = bp * sp, n_seq * t_new
    n_tok = n_p + n_s
    l = 0

    row = lambda a: a.reshape(1, -1)
    w_in_bf = w_in[l].astype(BF16)
    w_out_bf = w_out[l].astype(BF16)
    pw_bf = conv_pw_w[l].astype(BF16)
    rw = jnp.concatenate([router_group_w[l], router_expert_w[l]], axis=1)
    rw_bf = jnp.pad(rw, ((0, 0), (0, LANES - rw.shape[1]))).astype(BF16)
    rb = jnp.pad(jnp.concatenate([router_group_b[l], router_expert_b[l]]), (0, LANES - rw.shape[1])).reshape(1, LANES)
    g1, g2, gf = row(norm1_g[l]), row(norm2_g[l]), row(final_norm_g)
    sink = attn_sink[l]
    conv_w = (conv_dw_w[l], row(conv_dw_b[l]), row(conv_ln_g[l]), row(conv_ln_b[l]), pw_bf, row(conv_pw_b[l]))

    xp = x_prompt.reshape(n_p, D_MODEL)
    xs = x_sample.reshape(n_s, D_MODEL)
    cache_axes, cache_axes_back = (0, 1, 3, 4, 2), (0, 1, 4, 2, 3)
    ck = jnp.transpose(cache_win_k, cache_axes)
    cv = jnp.transpose(cache_win_v, cache_axes)
    cs = jnp.transpose(state_conv, (0, 2, 1, 3))

    qp, kp, vp, up = _in_proj(xp, g1, w_in_bf)
    qs, ks, vs, us = _in_proj(xs, g1, w_in_bf)
    mix_p = _mixer_prompt(sink, qp, kp, vp, up, *conv_w)
    mix_s, wk_s, wv_s, wc_s = _mixer_sample(sink, qs, ks, vs, us, ck, cv, cs, *conv_w)

    h_all, meta_all, cnt = _out_router(mix_p, xp, mix_s, xs, w_out_bf, g2, rw_bf, rb)

    off_lanes, ts, nv = _routing_tables(cnt[0], n_tok)
    pos_lanes = _positions(meta_all, off_lanes)
    pos = jnp.concatenate([pos_lanes[:, 0], pos_lanes[:, 1]])
    hs = _dispatch(pos, nv, h_all)
    ys = _moe(ts, hs, g2, expert_w_gate[l], expert_w_up[l], expert_w_down[l])
    y_p = _combine(pos, h_all, meta_all, ys, gf, 0, n_p)
    y_s = _combine(pos, h_all, meta_all, ys, gf, n_p, n_s)

    kv_shape = (depth, bp, WINDOW, N_KV_HEADS, HEAD_DIM)
    return (y_p.reshape(bp, sp, D_MODEL), y_s.reshape(n_seq, t_new, D_MODEL),
            kp[n_p - WINDOW:].reshape(kv_shape), vp[n_p - WINDOW:].reshape(kv_shape),
            up[n_p - (CONV_W - 1):].reshape(depth, bp, CONV_W - 1, D_CONV),
            jnp.transpose(wk_s, cache_axes_back), jnp.transpose(wv_s, cache_axes_back),
            jnp.transpose(wc_s, (0, 2, 1, 3)))
```

```python
import functools

import jax
import jax.numpy as jnp
from jax import lax
from jax.experimental import pallas as pl
from jax.experimental.pallas import tpu as pltpu

F32 = jnp.float32
BF16 = jnp.bfloat16

D_MODEL = 2048
HEAD_DIM = 64
N_HEADS = 16
N_KV_HEADS = 2
GQA_GROUP = 8
KV_DIM = N_KV_HEADS * HEAD_DIM
D_ATTN = N_HEADS * HEAD_DIM
D_CONV = D_MODEL - D_ATTN
WINDOW = 128
CONV_W = 31
D_IN = D_ATTN + 2 * KV_DIM + 2 * D_CONV
N_GROUPS = 4
EXPERTS_PER_GROUP = 8
N_EXPERTS = N_GROUPS * EXPERTS_PER_GROUP
D_EXPERT = 256
RMS_EPS = 1e-6
LN_EPS = 1e-5

LANES = 128
SUBLANES = 8
ROUTER_LANE0 = N_GROUPS
VMEM_LIMIT = 56 * 1024 * 1024

TM_PROJ = 512
TM_MIX = 256
SEQ_TILE = 8
TR = 128
TM_OUT = 512

_NT = (((1,), (1,)), ((), ()))


def _params(n_axes=1):
    return pltpu.CompilerParams(dimension_semantics=("arbitrary",) * n_axes, vmem_limit_bytes=VMEM_LIMIT)


def _resident(shape):
    return pl.BlockSpec(shape, lambda *_: (0,) * len(shape), pipeline_mode=pl.Buffered(1))


def _rmsnorm(x, g):
    ms = jnp.mean(x * x, axis=-1, keepdims=True)
    return x * lax.rsqrt(ms + RMS_EPS) * g


def _dot(a, b):
    return jnp.dot(a, b, preferred_element_type=F32)


def _dot_nt(a, b):
    return lax.dot_general(a, b, _NT, preferred_element_type=F32)


def _in_proj_body(x_ref, g_ref, w_ref, q_ref, k_ref, v_ref, u_ref):
    xn = _rmsnorm(x_ref[...], g_ref[...]).astype(BF16)
    q_ref[...] = (_dot(xn, w_ref[:, :D_ATTN]) * (HEAD_DIM ** -0.5)).astype(BF16)
    kv = _dot(xn, w_ref[:, D_ATTN:D_ATTN + 2 * KV_DIM])
    k_ref[...] = kv[:, :KV_DIM]
    v_ref[...] = kv[:, KV_DIM:]
    c0 = D_ATTN + 2 * KV_DIM
    cw = 256
    for j in range(D_CONV // cw):
        a = _dot(xn, w_ref[:, c0 + j * cw:c0 + (j + 1) * cw])
        b = _dot(xn, w_ref[:, c0 + D_CONV + j * cw:c0 + D_CONV + (j + 1) * cw])
        u_ref[:, j * cw:(j + 1) * cw] = a * jax.nn.sigmoid(b)


def _in_proj(x, g, w_bf):
    n = x.shape[0]
    tm = min(TM_PROJ, n)
    row = lambda w: pl.BlockSpec((tm, w), lambda i: (i, 0))
    return pl.pallas_call(
        _in_proj_body,
        grid=(n // tm,),
        in_specs=[row(D_MODEL), _resident((1, D_MODEL)), _resident((D_MODEL, D_IN))],
        out_specs=[row(D_ATTN), row(KV_DIM), row(KV_DIM), row(D_CONV)],
        out_shape=[jax.ShapeDtypeStruct((n, D_ATTN), BF16), jax.ShapeDtypeStruct((n, KV_DIM), F32),
                   jax.ShapeDtypeStruct((n, KV_DIM), F32), jax.ShapeDtypeStruct((n, D_CONV), F32)],
        compiler_params=_params(),
        name="in_proj",
    )(x, g, w_bf)


def _attend(q, parts, sink_ref, kv_head, keys_on_lanes=False):
    t = q.shape[0]
    score, weigh = (_dot, _dot_nt) if keys_on_lanes else (_dot_nt, _dot)
    qs = jnp.concatenate([q[:, g * HEAD_DIM:(g + 1) * HEAD_DIM] for g in range(GQA_GROUP)], axis=0)
    scores = [score(qs, kk) for kk, _, _ in parts]
    ps = [[] for _ in parts]
    inv = []
    for g in range(GQA_GROUP):
        sg = [jnp.where(mask, s[g * t:(g + 1) * t], -jnp.inf) for s, (_, _, mask) in zip(scores, parts)]
        sk = sink_ref[kv_head * GQA_GROUP + g]
        m = sk
        for x in sg:
            m = jnp.maximum(jnp.max(x, axis=-1, keepdims=True), m)
        den = jnp.exp(sk - m)
        for k, x in enumerate(sg):
            p = jnp.exp(x - m)
            den = den + jnp.sum(p, axis=-1, keepdims=True)
            ps[k].append(p.astype(BF16))
        inv.append(1.0 / den)
    o = None
    for k, (_, vv, _) in enumerate(parts):
        ok = weigh(jnp.concatenate(ps[k], axis=0), vv)
        o = ok if o is None else o + ok
    return jnp.concatenate([o[g * t:(g + 1) * t] * inv[g] for g in range(GQA_GROUP)], axis=1)


def _conv_tail(y, dwb_ref, lng_ref, lnb_ref, pw_ref, pwb_ref):
    y = y + dwb_ref[...]
    mu = jnp.mean(y, axis=-1, keepdims=True)
    yc = y - mu
    yn = yc * lax.rsqrt(jnp.mean(yc * yc, axis=-1, keepdims=True) + LN_EPS)
    yn = yn * lng_ref[...] + lnb_ref[...]
    act = yn * jax.nn.sigmoid(yn)
    return _dot(act.astype(BF16), pw_ref[...]) + pwb_ref[...]


CONV_PAD = 32
CONV_ROWS = 128
CONV_COLS = 128


def _mixer_prompt_body(sink_ref, q_ref, k_ref, v_ref, u_ref, dww_ref, dwb_ref, lng_ref, lnb_ref, pw_ref, pwb_ref,
                       mix_ref, kprev, vprev, uext, ushift, ybuf):
    i = pl.program_id(0)
    tm = q_ref.shape[0]

    @pl.when(i == 0)
    def _():
        kprev[...] = jnp.zeros_like(kprev)
        vprev[...] = jnp.zeros_like(vprev)
        uext[0:CONV_PAD, :] = jnp.zeros((CONV_PAD, D_CONV), F32)

    qi = lax.broadcasted_iota(jnp.int32, (WINDOW, 2 * WINDOW), 0)
    kj = lax.broadcasted_iota(jnp.int32, (WINDOW, 2 * WINDOW), 1)
    diff = qi + WINDOW - kj
    band = (diff >= 0) & (diff <= WINDOW)
    kp, vp = kprev[...], vprev[...]
    for b in range(tm // WINDOW):
        rows = slice(b * WINDOW, (b + 1) * WINDOW)
        kb = k_ref[rows, :].astype(BF16)
        vb = v_ref[rows, :].astype(BF16)
        kk = jnp.concatenate([kp, kb], axis=0)
        vv = jnp.concatenate([vp, vb], axis=0)
        if b == 0:
            mask = band & (kj >= jnp.where(i > 0, 0, WINDOW))
        else:
            mask = band
        for h in range(N_KV_HEADS):
            cols = slice(h * HEAD_DIM, (h + 1) * HEAD_DIM)
            hq = slice(h * GQA_GROUP * HEAD_DIM, (h + 1) * GQA_GROUP * HEAD_DIM)
            o = _attend(q_ref[rows, hq], [(kk[:, cols], vv[:, cols], mask)], sink_ref, h)
            mix_ref[rows, hq] = o.astype(BF16)
        kp, vp = kb, vb
    kprev[...] = kp
    vprev[...] = vp

    uext[CONV_PAD:CONV_PAD + tm, :] = u_ref[...]
    n_shift_rows = ushift.shape[1]
    for s in range(1, SUBLANES):
        ushift[s - 1] = uext[s:s + n_shift_rows, :]
    off = CONV_PAD - (CONV_W - 1)
    for r in range(tm // CONV_ROWS):
        for c in range(D_CONV // CONV_COLS):
            cs = slice(c * CONV_COLS, (c + 1) * CONV_COLS)
            acc = jnp.zeros((CONV_ROWS, CONV_COLS), F32)
            for j in range(CONV_W):
                a, s = divmod(off + j, SUBLANES)
                r0 = r * CONV_ROWS + a * SUBLANES
                src = uext[r0:r0 + CONV_ROWS, cs] if s == 0 else ushift[s - 1, r0:r0 + CONV_ROWS, cs]
                acc = acc + dww_ref[j:j + 1, cs] * src
            ybuf[r * CONV_ROWS:(r + 1) * CONV_ROWS, cs] = acc
    uext[0:CONV_PAD, :] = uext[tm:tm + CONV_PAD, :]
    conv = _conv_tail(ybuf[...], dwb_ref, lng_ref, lnb_ref, pw_ref, pwb_ref)
    mix_ref[:, D_ATTN:] = conv.astype(BF16)


def _mixer_prompt(sink, q, k, v, u, dww, dwb, lng, lnb, pw_bf, pwb):
    n = q.shape[0]
    tm = TM_MIX
    row = lambda w: pl.BlockSpec((tm, w), lambda i, s: (i, 0))
    res = lambda shape: pl.BlockSpec(shape, lambda i, s: (0,) * len(shape), pipeline_mode=pl.Buffered(1))
    return pl.pallas_call(
        _mixer_prompt_body,
        grid_spec=pltpu.PrefetchScalarGridSpec(
            num_scalar_prefetch=1,
            grid=(n // tm,),
            in_specs=[row(D_ATTN), row(KV_DIM), row(KV_DIM), row(D_CONV), res((CONV_W, D_CONV)), res((1, D_CONV)),
                      res((1, D_CONV)), res((1, D_CONV)), res((D_CONV, D_CONV)), res((1, D_CONV))],
            out_specs=row(D_MODEL),
            scratch_shapes=[pltpu.VMEM((WINDOW, KV_DIM), BF16), pltpu.VMEM((WINDOW, KV_DIM), BF16),
                            pltpu.VMEM((CONV_PAD + tm, D_CONV), F32),
                            pltpu.VMEM((SUBLANES - 1, CONV_PAD - SUBLANES + tm, D_CONV), F32),
                            pltpu.VMEM((tm, D_CONV), F32)],
        ),
        out_shape=jax.ShapeDtypeStruct((n, D_MODEL), BF16),
        compiler_params=_params(),
        name="mixer_prompt",
    )(sink, q, k, v, u, dww, dwb, lng, lnb, pw_bf, pwb)


def _mixer_sample_body(sink_ref, q_ref, k_ref, v_ref, u_ref, ck_ref, cv_ref, cs_ref, dww_ref, dwb_ref, lng_ref,
                       lnb_ref, pw_ref, pwb_ref, mix_ref, wk_ref, wv_ref, wc_ref, usel, ybuf):
    nt = q_ref.shape[0]
    sb = ck_ref.shape[0]
    t_new = nt // sb
    n_cache = sb * WINDOW
    n_state = CONV_W - 1

    pad = jnp.zeros((LANES - nt, KV_DIM), F32)
    kn_t = jnp.concatenate([k_ref[...], pad], axis=0).T
    vn_t = jnp.concatenate([v_ref[...], pad], axis=0).T

    lt, lw = t_new.bit_length() - 1, WINDOW.bit_length() - 1
    qr = lax.broadcasted_iota(jnp.int32, (nt, n_cache), 0)
    cc = lax.broadcasted_iota(jnp.int32, (nt, n_cache), 1)
    mask_c = ((cc >> lw) == (qr >> lt)) & ((cc & (WINDOW - 1)) >= (qr & (t_new - 1)))
    qr = lax.broadcasted_iota(jnp.int32, (nt, LANES), 0)
    cn = lax.broadcasted_iota(jnp.int32, (nt, LANES), 1)
    mask_n = ((cn >> lt) == (qr >> lt)) & ((cn & (t_new - 1)) <= (qr & (t_new - 1)))
    for h in range(N_KV_HEADS):
        dims = slice(h * HEAD_DIM, (h + 1) * HEAD_DIM)
        hq = slice(h * GQA_GROUP * HEAD_DIM, (h + 1) * GQA_GROUP * HEAD_DIM)
        kc = jnp.concatenate([ck_ref[s, h] for s in range(sb)], axis=1).astype(BF16)
        vc = jnp.concatenate([cv_ref[s, h] for s in range(sb)], axis=1).astype(BF16)
        parts = [(kc, vc, mask_c), (kn_t[dims, :].astype(BF16), vn_t[dims, :].astype(BF16), mask_n)]
        o = _attend(q_ref[:, hq], parts, sink_ref, h, keys_on_lanes=True)
        mix_ref[:, hq] = o.astype(BF16)

    is_new = lax.broadcasted_iota(jnp.int32, (HEAD_DIM, WINDOW), 1) >= WINDOW - t_new
    for s in range(sb):
        shift = (WINDOW - t_new - s * t_new) % LANES
        kn_s = pltpu.roll(kn_t, shift=shift, axis=1)
        vn_s = pltpu.roll(vn_t, shift=shift, axis=1)
        for h in range(N_KV_HEADS):
            dims = slice(h * HEAD_DIM, (h + 1) * HEAD_DIM)
            old_k = pltpu.roll(ck_ref[s, h], shift=WINDOW - t_new, axis=1)
            old_v = pltpu.roll(cv_ref[s, h], shift=WINDOW - t_new, axis=1)
            wk_ref[s, h] = jnp.where(is_new, kn_s[dims, :], old_k)
            wv_ref[s, h] = jnp.where(is_new, vn_s[dims, :], old_v)

    for s in range(sb):
        for t in range(t_new):
            usel[t, s:s + 1, :] = u_ref[s * t_new + t:s * t_new + t + 1, :]
    u_new = [usel[t] for t in range(t_new)]
    for t in range(t_new):
        acc = jnp.zeros((sb, D_CONV), F32)
        for r in range(t, n_state):
            acc = acc + dww_ref[r - t:r - t + 1, :] * cs_ref[r]
        for t2 in range(t + 1):
            j = n_state - t + t2
            acc = acc + dww_ref[j:j + 1, :] * u_new[t2]
        for s in range(sb):
            ybuf[s * t_new + t:s * t_new + t + 1, :] = acc[s:s + 1, :]
        wc_ref[n_state - t_new + t] = u_new[t]
    wc_ref[0:n_state - t_new] = cs_ref[t_new:n_state]
    conv = _conv_tail(ybuf[...], dwb_ref, lng_ref, lnb_ref, pw_ref, pwb_ref)
    mix_ref[:, D_ATTN:] = conv.astype(BF16)


def _mixer_sample(sink, q, k, v, u, ck, cv, cs, dww, dwb, lng, lnb, pw_bf, pwb):
    n = q.shape[0]
    n_seq = ck.shape[1]
    t_new = n // n_seq
    sb = SEQ_TILE
    nt = sb * t_new
    n_state = CONV_W - 1
    row = lambda w: pl.BlockSpec((nt, w), lambda i, s: (i, 0))
    cache = pl.BlockSpec((None, sb, N_KV_HEADS, HEAD_DIM, WINDOW), lambda i, s: (0, i, 0, 0, 0))
    state = pl.BlockSpec((None, n_state, sb, D_CONV), lambda i, s: (0, 0, i, 0))
    res = lambda shape: pl.BlockSpec(shape, lambda i, s: (0,) * len(shape), pipeline_mode=pl.Buffered(1))
    return pl.pallas_call(
        _mixer_sample_body,
        grid_spec=pltpu.PrefetchScalarGridSpec(
            num_scalar_prefetch=1,
            grid=(n_seq // sb,),
            in_specs=[row(D_ATTN), row(KV_DIM), row(KV_DIM), row(D_CONV), cache, cache, state,
                      res((CONV_W, D_CONV)), res((1, D_CONV)), res((1, D_CONV)), res((1, D_CONV)),
                      res((D_CONV, D_CONV)), res((1, D_CONV))],
            out_specs=[row(D_MODEL), cache, cache, state],
            scratch_shapes=[pltpu.VMEM((t_new, sb, D_CONV), F32), pltpu.VMEM((nt, D_CONV), F32)],
        ),
        out_shape=[jax.ShapeDtypeStruct((n, D_MODEL), BF16), jax.ShapeDtypeStruct(ck.shape, F32),
                   jax.ShapeDtypeStruct(cv.shape, F32), jax.ShapeDtypeStruct(cs.shape, F32)],
        compiler_params=_params(),
        name="mixer_sample",
    )(sink, q, k, v, u, ck, cv, cs, dww, dwb, lng, lnb, pw_bf, pwb)


META_E0, META_E1, META_G0, META_G1, META_R0, META_R1 = range(6)


def _out_router_body(mixp_ref, xp_ref, mixs_ref, xs_ref, wout_ref, g2_ref, rw_ref, rb_ref, h_ref, meta_ref, cnt_ref,
                     carry, lower, *, prompt_tiles):
    i = pl.program_id(0)
    tm = xp_ref.shape[0]

    @pl.when(i == 0)
    def _():
        carry[...] = jnp.zeros_like(carry)
        ti = lax.broadcasted_iota(jnp.int32, (tm, tm), 0)
        tj = lax.broadcasted_iota(jnp.int32, (tm, tm), 1)
        lower[...] = jnp.where(tj < ti, 1.0, 0.0).astype(BF16)

    is_prompt = i < prompt_tiles
    x = jnp.where(is_prompt, xp_ref[...], xs_ref[...])
    mix = jnp.where(is_prompt, mixp_ref[...], mixs_ref[...])
    h = x + _dot(mix, wout_ref[...])
    h_ref[...] = h
    xn = _rmsnorm(h, g2_ref[...]).astype(BF16)
    logits = _dot(xn, rw_ref[...]) + rb_ref[...]
    lane = lax.broadcasted_iota(jnp.int32, logits.shape, 1).astype(F32)
    first = lambda cond: jnp.min(jnp.where(cond, lane, float(LANES)), axis=-1, keepdims=True)

    gl = jnp.where(lane < N_GROUPS, logits, -jnp.inf)
    gmax = jnp.max(gl, axis=-1, keepdims=True)
    gidx = first(gl == gmax)
    gval = 1.0 / jnp.sum(jnp.exp(gl - gmax), axis=-1, keepdims=True)

    lo = ROUTER_LANE0 + EXPERTS_PER_GROUP * gidx
    el = jnp.where((lane >= lo) & (lane < lo + EXPERTS_PER_GROUP), logits, -jnp.inf)
    m1 = jnp.max(el, axis=-1, keepdims=True)
    i1 = first(el == m1)
    el2 = jnp.where(lane == i1, -jnp.inf, el)
    m2 = jnp.max(el2, axis=-1, keepdims=True)
    i2 = first(el2 == m2)
    r = jnp.exp(m2 - m1)
    g0 = gval * (1.0 / (1.0 + r))
    g1 = gval * (r / (1.0 + r))

    sel0, sel1 = lane == i1, lane == i2
    onehot = jnp.where(sel0 | sel1, 1.0, 0.0)
    before = _dot(lower[...], onehot.astype(BF16)) + carry[0:1, :]
    r0 = jnp.sum(jnp.where(sel0, before, 0.0), axis=-1, keepdims=True)
    r1 = jnp.sum(jnp.where(sel1, before, 0.0), axis=-1, keepdims=True)
    carry[...] = carry[...] + jnp.sum(onehot, axis=0, keepdims=True)
    cnt_ref[...] = carry[...]

    lane_i = lax.broadcasted_iota(jnp.int32, logits.shape, 1)
    meta = jnp.zeros(logits.shape, F32)
    for slot, val in ((META_E0, i1 - ROUTER_LANE0), (META_E1, i2 - ROUTER_LANE0), (META_G0, g0), (META_G1, g1),
                      (META_R0, r0), (META_R1, r1)):
        meta = jnp.where(lane_i == slot, val, meta)
    meta_ref[...] = meta


def _out_router(mix_p, x_p, mix_s, x_s, wout_bf, g2, rw_bf, rb):
    n_p, n_s = x_p.shape[0], x_s.shape[0]
    tm = TM_PROJ
    assert n_p % tm == 0 and n_s % tm == 0
    tp, ts = n_p // tm, n_s // tm
    n = n_p + n_s
    prow = lambda w: pl.BlockSpec((tm, w), lambda i: (jnp.minimum(i, tp - 1), 0))
    srow = lambda w: pl.BlockSpec((tm, w), lambda i: (jnp.maximum(i - tp, 0), 0))
    row = lambda w: pl.BlockSpec((tm, w), lambda i: (i, 0))
    return pl.pallas_call(
        functools.partial(_out_router_body, prompt_tiles=tp),
        grid=(tp + ts,),
        in_specs=[prow(D_MODEL), prow(D_MODEL), srow(D_MODEL), srow(D_MODEL), _resident((D_MODEL, D_MODEL)),
                  _resident((1, D_MODEL)), _resident((D_MODEL, LANES)), _resident((1, LANES))],
        out_specs=[row(D_MODEL), row(LANES), pl.BlockSpec((8, LANES), lambda i: (0, 0))],
        out_shape=[jax.ShapeDtypeStruct((n, D_MODEL), F32), jax.ShapeDtypeStruct((n, LANES), F32),
                   jax.ShapeDtypeStruct((8, LANES), F32)],
        scratch_shapes=[pltpu.VMEM((8, LANES), F32), pltpu.VMEM((tm, tm), BF16)],
        compiler_params=_params(),
        name="out_router",
    )(mix_p, x_p, mix_s, x_s, wout_bf, g2, rw_bf, rb)


def _positions_body(meta_ref, off_ref, pos_ref):
    meta = meta_ref[...]
    lane = lax.broadcasted_iota(jnp.int32, meta.shape, 1)
    lane_f = lane.astype(F32)
    off = off_ref[...]
    pos = jnp.zeros(meta.shape, F32)
    for slot, (e_lane, r_lane) in enumerate(((META_E0, META_R0), (META_E1, META_R1))):
        e = meta[:, e_lane:e_lane + 1] + ROUTER_LANE0
        seg = jnp.sum(jnp.where(lane_f == e, off, 0.0), axis=-1, keepdims=True)
        pos = jnp.where(lane == slot, seg + meta[:, r_lane:r_lane + 1], pos)
    pos_ref[...] = pos.astype(jnp.int32)


def _positions(meta_all, off_lanes):
    n = meta_all.shape[0]
    tm = TM_PROJ
    assert n % tm == 0
    return pl.pallas_call(
        _positions_body,
        grid=(n // tm,),
        in_specs=[pl.BlockSpec((tm, LANES), lambda i: (i, 0)), _resident((1, LANES))],
        out_specs=pl.BlockSpec((tm, LANES), lambda i: (i, 0)),
        out_shape=jax.ShapeDtypeStruct((n, LANES), jnp.int32),
        compiler_params=_params(),
        name="positions",
    )(meta_all, off_lanes)


N_LOAD_SLOTS = 3


def _pow2_sizes(n):
    return [1 << b for b in range(n.bit_length() - 1, -1, -1)]


def _dispatch_body(pos_ref, nv_ref, h_ref, hs_ref, hbuf, zbuf, lsem, ssem, zsem, *, n_tok, tm):
    i = pl.program_id(0)
    n_steps = pl.num_programs(0)
    n_tiles = nv_ref.shape[0]

    groups = tm // SUBLANES

    def load(t, slot, sem=lsem):
        return pltpu.make_async_copy(h_ref.at[pl.ds(t * groups, groups)], hbuf.at[slot], sem.at[slot])

    def clear_unowned(start):
        def per_tile(t, c):
            nv = nv_ref[t]
            z = TR - nv
            row = t * TR + nv
            head = z & (SUBLANES - 1)
            for k in range(SUBLANES - 1):
                @pl.when(k < head)
                def _(k=k):
                    cp = pltpu.make_async_copy(zbuf.at[pl.ds(0, 1), :], hs_ref.at[pl.ds(row + k, 1), :], zsem)
                    cp.start() if start else cp.wait()
            row = row + head
            for size in _pow2_sizes(TR):
                if size < SUBLANES:
                    break
                @pl.when((z & size) != 0)
                def _(row=row, size=size):
                    dst = hs_ref.at[pl.ds(pl.multiple_of(row, SUBLANES), size), :]
                    cp = pltpu.make_async_copy(zbuf.at[pl.ds(0, size), :], dst, zsem)
                    cp.start() if start else cp.wait()
                row = row + (z & size)
            return c
        lax.fori_loop(0, n_tiles, per_tile, 0)

    def scatter_wait(slot):
        for _ in range(2):
            load(0, slot, ssem).wait()

    @pl.when(i == 0)
    def _():
        zbuf[...] = jnp.zeros_like(zbuf)
        clear_unowned(True)
        load(0, 0).start()

    @pl.when(i + 1 < n_steps)
    def _():
        load(i + 1, (i + 1) % N_LOAD_SLOTS).start()

    slot = i % N_LOAD_SLOTS
    load(i, slot).wait()

    def issue(jb, c):
        for u in range(SUBLANES):
            for s in range(2):
                p = pos_ref[s * n_tok + i * tm + jb * SUBLANES + u]
                pltpu.make_async_copy(hbuf.at[slot, jb, pl.ds(u, 1), :], hs_ref.at[pl.ds(p, 1), :],
                                      ssem.at[slot]).start()
        return c
    lax.fori_loop(0, groups, issue, 0)

    @pl.when(i > 0)
    def _():
        scatter_wait((i + N_LOAD_SLOTS - 1) % N_LOAD_SLOTS)

    @pl.when(i == n_steps - 1)
    def _():
        scatter_wait(slot)
        clear_unowned(False)


def _dispatch(pos, nv, rows):
    n_tok, width = rows.shape
    n_tiles = nv.shape[0]
    tm = TM_PROJ
    assert n_tok % tm == 0
    return pl.pallas_call(
        functools.partial(_dispatch_body, n_tok=n_tok, tm=tm),
        grid_spec=pltpu.PrefetchScalarGridSpec(
            num_scalar_prefetch=2,
            grid=(n_tok // tm,),
            in_specs=[pl.BlockSpec(memory_space=pl.ANY)],
            out_specs=pl.BlockSpec(memory_space=pl.ANY),
            scratch_shapes=[pltpu.VMEM((N_LOAD_SLOTS, tm // SUBLANES, SUBLANES, width), rows.dtype),
                            pltpu.VMEM((TR, width), rows.dtype),
                            pltpu.SemaphoreType.DMA((N_LOAD_SLOTS,)), pltpu.SemaphoreType.DMA((N_LOAD_SLOTS,)),
                            pltpu.SemaphoreType.DMA(())],
        ),
        out_shape=jax.ShapeDtypeStruct((n_tiles * TR, width), rows.dtype),
        compiler_params=_params(),
        name="dispatch",
    )(pos, nv, rows.reshape(n_tok // SUBLANES, SUBLANES, width))


N_MOE_LOADS = 4


def _moe_body(ts_ref, hs_ref, g2_ref, wg_ref, wu_ref, wd_ref, ys_ref, xbuf, obuf, wg_bf, wu_bf, wd_bf, lsem, ssem):
    e = pl.program_id(0)
    n_tiles = ys_ref.shape[0] // TR
    n_used = ts_ref[N_EXPERTS]
    t0, t1 = ts_ref[e], ts_ref[e + 1]

    def load(g, slot):
        return pltpu.make_async_copy(hs_ref.at[pl.ds(pl.multiple_of(g * TR, TR), TR), :], xbuf.at[slot], lsem.at[slot])

    def store(g, slot):
        return pltpu.make_async_copy(obuf.at[slot], ys_ref.at[pl.ds(pl.multiple_of(g * TR, TR), TR), :], ssem.at[slot])

    @pl.when(e == 0)
    def _():
        for g in range(N_MOE_LOADS - 1):
            @pl.when(g < n_used)
            def _(g=g):
                load(g, g).start()

    @pl.when(t1 > t0)
    def _():
        wg_bf[...] = wg_ref[0].astype(BF16)
        wu_bf[...] = wu_ref[0].astype(BF16)
        wd_bf[...] = wd_ref[0].astype(BF16)

    def tile(g, c):
        slot = g % 2
        xslot = g % N_MOE_LOADS
        ahead = g + N_MOE_LOADS - 1

        @pl.when(ahead < n_used)
        def _():
            load(ahead, ahead % N_MOE_LOADS).start()

        load(g, xslot).wait()
        xn = _rmsnorm(xbuf[xslot], g2_ref[...]).astype(BF16)
        hg = _dot(xn, wg_bf[...])
        hu = _dot(xn, wu_bf[...])
        hid = (hg * jax.nn.sigmoid(hg) * hu).astype(BF16)
        y = _dot(hid, wd_bf[...])

        @pl.when(g >= 2)
        def _():
            store(g - 2, slot).wait()

        obuf[slot] = y
        store(g, slot).start()
        return c

    lax.fori_loop(t0, t1, tile, 0)

    @pl.when(e == pl.num_programs(0) - 1)
    def _():
        @pl.when(n_used >= 2)
        def _():
            store(n_used - 2, n_used % 2).wait()
        store(n_used - 1, (n_used - 1) % 2).wait()
        obuf[0] = jnp.zeros(obuf.shape[1:], obuf.dtype)

        def clear(g, c):
            store(g, 0).start()
            return c
        lax.fori_loop(n_used, n_tiles, clear, 0)

        def drain(g, c):
            store(g, 0).wait()
            return c
        lax.fori_loop(n_used, n_tiles, drain, 0)


def _moe(ts, hs, g2, wg, wu, wd):
    wspec = lambda shape: pl.BlockSpec((1,) + shape, lambda e, ts: (e, 0, 0))
    tile_buf = lambda slots: pltpu.VMEM((slots, TR, hs.shape[1]), hs.dtype)
    return pl.pallas_call(
        _moe_body,
        grid_spec=pltpu.PrefetchScalarGridSpec(
            num_scalar_prefetch=1,
            grid=(N_EXPERTS,),
            in_specs=[pl.BlockSpec(memory_space=pl.ANY),
                      pl.BlockSpec((1, D_MODEL), lambda e, ts: (0, 0)),
                      wspec((D_MODEL, D_EXPERT)), wspec((D_MODEL, D_EXPERT)), wspec((D_EXPERT, D_MODEL))],
            out_specs=pl.BlockSpec(memory_space=pl.ANY),
            scratch_shapes=[tile_buf(N_MOE_LOADS), tile_buf(2),
                            pltpu.VMEM((D_MODEL, D_EXPERT), BF16), pltpu.VMEM((D_MODEL, D_EXPERT), BF16),
                            pltpu.VMEM((D_EXPERT, D_MODEL), BF16),
                            pltpu.SemaphoreType.DMA((N_MOE_LOADS,)), pltpu.SemaphoreType.DMA((2,))],
        ),
        out_shape=jax.ShapeDtypeStruct(hs.shape, hs.dtype),
        compiler_params=_params(),
        name="moe",
    )(ts, hs, g2, wg, wu, wd)


def _combine_body(pos_ref, h_ref, meta_ref, g_ref, ys_ref, ys_grouped_ref, o_ref, ybuf, sem, *, n_tok, row0):
    i = pl.program_id(0)
    n_steps = pl.num_programs(0)
    tm = h_ref.shape[0]

    groups = tm // SUBLANES

    def gather(t, slot):
        def issue(jb, c):
            for u in range(SUBLANES):
                for s in range(2):
                    p = pos_ref[s * n_tok + row0 + t * tm + jb * SUBLANES + u]
                    pltpu.make_async_copy(ys_ref.at[pl.ds(p, 1), :], ybuf.at[slot, s, jb, pl.ds(u, 1), :],
                                          sem.at[slot]).start()
            return c
        lax.fori_loop(0, groups, issue, 0)

    @pl.when(i == 0)
    def _():
        gather(0, 0)

    @pl.when(i + 1 < n_steps)
    def _():
        gather(i + 1, (i + 1) % 2)

    slot = i % 2
    for s in range(2):
        pltpu.make_async_copy(ys_grouped_ref.at[pl.ds(0, groups)], ybuf.at[slot, s], sem.at[slot]).wait()
    meta = meta_ref[...]
    g0 = meta[:, META_G0:META_G0 + 1]
    g1 = meta[:, META_G1:META_G1 + 1]
    y0 = ybuf[slot, 0].reshape(tm, D_MODEL)
    y1 = ybuf[slot, 1].reshape(tm, D_MODEL)
    h = h_ref[...] + (g0 * y0 + g1 * y1)
    o_ref[...] = _rmsnorm(h, g_ref[...])


def _combine(pos, h_all, meta_all, ys, gf, row0, n):
    n_tok = h_all.shape[0]
    tm = min(TM_OUT, n)
    assert row0 % tm == 0 and n % tm == 0
    b0 = row0 // tm
    return pl.pallas_call(
        functools.partial(_combine_body, n_tok=n_tok, row0=row0),
        grid_spec=pltpu.PrefetchScalarGridSpec(
            num_scalar_prefetch=1,
            grid=(n // tm,),
            in_specs=[pl.BlockSpec((tm, D_MODEL), lambda i, pos: (b0 + i, 0)),
                      pl.BlockSpec((tm, LANES), lambda i, pos: (b0 + i, 0)),
                      pl.BlockSpec((1, D_MODEL), lambda i, pos: (0, 0)),
                      pl.BlockSpec(memory_space=pl.ANY), pl.BlockSpec(memory_space=pl.ANY)],
            out_specs=pl.BlockSpec((tm, D_MODEL), lambda i, pos: (i, 0)),
            scratch_shapes=[pltpu.VMEM((2, 2, tm // SUBLANES, SUBLANES, ys.shape[1]), ys.dtype),
                            pltpu.SemaphoreType.DMA((2,))],
        ),
        out_shape=jax.ShapeDtypeStruct((n, D_MODEL), F32),
        compiler_params=_params(),
        name="combine",
    )(pos, h_all, meta_all, gf, ys, ys.reshape(ys.shape[0] // SUBLANES, SUBLANES, ys.shape[1]))


def _routing_tables(counts, n_tok):
    n_tiles = (2 * n_tok) // TR + N_EXPERTS
    cnt = counts[ROUTER_LANE0:ROUTER_LANE0 + N_EXPERTS].astype(jnp.int32)
    tiles_e = (cnt + TR - 1) // TR
    tile_end = jnp.cumsum(tiles_e)
    tile_start = tile_end - tiles_e
    off_lanes = jnp.pad((tile_start * TR).astype(F32), (ROUTER_LANE0, LANES - ROUTER_LANE0 - N_EXPERTS))
    ts = jnp.concatenate([tile_start, tile_end[-1:]]).astype(jnp.int32)
    tile = jnp.arange(n_tiles, dtype=jnp.int32)
    owner = (tile[:, None] >= tile_start[None, :]) & (tile[:, None] < tile_end[None, :])
    rows_left = jnp.sum(jnp.where(owner, cnt[None, :] - (tile[:, None] - tile_start[None, :]) * TR, 0), axis=1)
    nv = jnp.clip(rows_left, 0, TR).astype(jnp.int32)
    return off_lanes.reshape(1, LANES), ts, nv


def kernel(x_prompt, x_sample, cache_win_k, cache_win_v, state_conv, norm1_g, w_in, attn_sink, conv_dw_w, conv_dw_b,
           conv_ln_g, conv_ln_b, conv_pw_w, conv_pw_b, w_out, norm2_g, router_group_w, router_group_b,
           router_expert_w, router_expert_b, expert_w_gate, expert_w_up, expert_w_down, final_norm_g):
    depth = w_in.shape[0]
    assert depth == 1, "single-layer step"
    bp, sp, _ = x_prompt.shape
    assert bp == 1, "one prompt sequence"
    n_seq, t_new, _ = x_sample.shape
    n_p, n_s = bp * sp, n_seq * t_new
    n_tok = n_p + n_s
    l = 0

    row = lambda a: a.reshape(1, -1)
    w_in_bf = w_in[l].astype(BF16)
    w_out_bf = w_out[l].astype(BF16)
    pw_bf = conv_pw_w[l].astype(BF16)
    rw = jnp.concatenate([router_group_w[l], router_expert_w[l]], axis=1)
    rw_bf = jnp.pad(rw, ((0, 0), (0, LANES - rw.shape[1]))).astype(BF16)
    rb = jnp.pad(jnp.concatenate([router_group_b[l], router_expert_b[l]]), (0, LANES - rw.shape[1])).reshape(1, LANES)
    g1, g2, gf = row(norm1_g[l]), row(norm2_g[l]), row(final_norm_g)
    sink = attn_sink[l]
    conv_w = (conv_dw_w[l], row(conv_dw_b[l]), row(conv_ln_g[l]), row(conv_ln_b[l]), pw_bf, row(conv_pw_b[l]))

    xp = x_prompt.reshape(n_p, D_MODEL)
    xs = x_sample.reshape(n_s, D_MODEL)
    cache_axes, cache_axes_back = (0, 1, 3, 4, 2), (0, 1, 4, 2, 3)
    ck = jnp.transpose(cache_win_k, cache_axes)
    cv = jnp.transpose(cache_win_v, cache_axes)
    cs = jnp.transpose(state_conv, (0, 2, 1, 3))

    qp, kp, vp, up = _in_proj(xp, g1, w_in_bf)
    qs, ks, vs, us = _in_proj(xs, g1, w_in_bf)
    mix_p = _mixer_prompt(sink, qp, kp, vp, up, *conv_w)
    mix_s, wk_s, wv_s, wc_s = _mixer_sample(sink, qs, ks, vs, us, ck, cv, cs, *conv_w)

    h_all, meta_all, cnt = _out_router(mix_p, xp, mix_s, xs, w_out_bf, g2, rw_bf, rb)

    off_lanes, ts, nv = _routing_tables(cnt[0], n_tok)
    pos_lanes = _positions(meta_all, off_lanes)
    pos = jnp.concatenate([pos_lanes[:, 0], pos_lanes[:, 1]])
    hs = _dispatch(pos, nv, h_all)
    ys = _moe(ts, hs, g2, expert_w_gate[l], expert_w_up[l], expert_w_down[l])
    y_p = _combine(pos, h_all, meta_all, ys, gf, 0, n_p)
    y_s = _combine(pos, h_all, meta_all, ys, gf, n_p, n_s)

    kv_shape = (depth, bp, WINDOW, N_KV_HEADS, HEAD_DIM)
    return (y_p.reshape(bp, sp, D_MODEL), y_s.reshape(n_seq, t_new, D_MODEL),
            kp[n_p - WINDOW:].reshape(kv_shape), vp[n_p - WINDOW:].reshape(kv_shape),
            up[n_p - (CONV_W - 1):].reshape(depth, bp, CONV_W - 1, D_CONV),
            jnp.transpose(wk_s, cache_axes_back), jnp.transpose(wv_s, cache_axes_back),
            jnp.transpose(wc_s, (0, 2, 1, 3)))
```

```python
import functools

import jax
import jax.numpy as jnp
from jax import lax
from jax.experimental import pallas as pl
from jax.experimental.pallas import tpu as pltpu

F32 = jnp.float32
BF16 = jnp.bfloat16

D_MODEL = 2048
HEAD_DIM = 64
N_HEADS = 16
N_KV_HEADS = 2
GQA_GROUP = 8
KV_DIM = N_KV_HEADS * HEAD_DIM
D_ATTN = N_HEADS * HEAD_DIM
D_CONV = D_MODEL - D_ATTN
WINDOW = 128
CONV_W = 31
D_IN = D_ATTN + 2 * KV_DIM + 2 * D_CONV
N_GROUPS = 4
EXPERTS_PER_GROUP = 8
N_EXPERTS = N_GROUPS * EXPERTS_PER_GROUP
D_EXPERT = 256
RMS_EPS = 1e-6
LN_EPS = 1e-5

LANES = 128
SUBLANES = 8
ROUTER_LANE0 = N_GROUPS
VMEM_LIMIT = 56 * 1024 * 1024

TM_PROJ = 512
TM_MIX = 256
SEQ_TILE = 8
TR = 128
TM_OUT = 256

_NT = (((1,), (1,)), ((), ()))


def _params(n_axes=1):
    return pltpu.CompilerParams(dimension_semantics=("arbitrary",) * n_axes, vmem_limit_bytes=VMEM_LIMIT)


def _resident(shape):
    return pl.BlockSpec(shape, lambda *_: (0,) * len(shape), pipeline_mode=pl.Buffered(1))


def _rmsnorm(x, g):
    ms = jnp.mean(x * x, axis=-1, keepdims=True)
    return x * lax.rsqrt(ms + RMS_EPS) * g


def _dot(a, b):
    return jnp.dot(a, b, preferred_element_type=F32)


def _dot_nt(a, b):
    return lax.dot_general(a, b, _NT, preferred_element_type=F32)


def _in_proj_body(x_ref, g_ref, w_ref, q_ref, k_ref, v_ref, u_ref):
    xn = _rmsnorm(x_ref[...], g_ref[...]).astype(BF16)
    q_ref[...] = (_dot(xn, w_ref[:, :D_ATTN]) * (HEAD_DIM ** -0.5)).astype(BF16)
    kv = _dot(xn, w_ref[:, D_ATTN:D_ATTN + 2 * KV_DIM])
    k_ref[...] = kv[:, :KV_DIM]
    v_ref[...] = kv[:, KV_DIM:]
    c0 = D_ATTN + 2 * KV_DIM
    cw = 256
    for j in range(D_CONV // cw):
        a = _dot(xn, w_ref[:, c0 + j * cw:c0 + (j + 1) * cw])
        b = _dot(xn, w_ref[:, c0 + D_CONV + j * cw:c0 + D_CONV + (j + 1) * cw])
        u_ref[:, j * cw:(j + 1) * cw] = a * jax.nn.sigmoid(b)


def _in_proj(x, g, w_bf):
    n = x.shape[0]
    tm = min(TM_PROJ, n)
    row = lambda w: pl.BlockSpec((tm, w), lambda i: (i, 0))
    return pl.pallas_call(
        _in_proj_body,
        grid=(n // tm,),
        in_specs=[row(D_MODEL), _resident((1, D_MODEL)), _resident((D_MODEL, D_IN))],
        out_specs=[row(D_ATTN), row(KV_DIM), row(KV_DIM), row(D_CONV)],
        out_shape=[jax.ShapeDtypeStruct((n, D_ATTN), BF16), jax.ShapeDtypeStruct((n, KV_DIM), F32),
                   jax.ShapeDtypeStruct((n, KV_DIM), F32), jax.ShapeDtypeStruct((n, D_CONV), F32)],
        compiler_params=_params(),
        name="in_proj",
    )(x, g, w_bf)


def _attend(q, parts, sink_ref, kv_head, keys_on_lanes=False):
    t = q.shape[0]
    score, weigh = (_dot, _dot_nt) if keys_on_lanes else (_dot_nt, _dot)
    qs = jnp.concatenate([q[:, g * HEAD_DIM:(g + 1) * HEAD_DIM] for g in range(GQA_GROUP)], axis=0)
    scores = [score(qs, kk) for kk, _, _ in parts]
    hidden = [jnp.where(mask, 0.0, -jnp.inf) for _, _, mask in parts]
    ps = [[] for _ in parts]
    inv = []
    for g in range(GQA_GROUP):
        sg = [s[g * t:(g + 1) * t] + neg for s, neg in zip(scores, hidden)]
        sk = sink_ref[kv_head * GQA_GROUP + g]
        m = sk
        for x in sg:
            m = jnp.maximum(jnp.max(x, axis=-1, keepdims=True), m)
        den = jnp.exp(sk - m)
        for k, x in enumerate(sg):
            p = jnp.exp(x - m)
            den = den + jnp.sum(p, axis=-1, keepdims=True)
            ps[k].append(p.astype(BF16))
        inv.append(1.0 / den)
    o = None
    for k, (_, vv, _) in enumerate(parts):
        ok = weigh(jnp.concatenate(ps[k], axis=0), vv)
        o = ok if o is None else o + ok
    return jnp.concatenate([o[g * t:(g + 1) * t] * inv[g] for g in range(GQA_GROUP)], axis=1)


def _conv_tail(y, dwb_ref, lng_ref, lnb_ref, pw_ref, pwb_ref):
    y = y + dwb_ref[...]
    mu = jnp.mean(y, axis=-1, keepdims=True)
    yc = y - mu
    yn = yc * lax.rsqrt(jnp.mean(yc * yc, axis=-1, keepdims=True) + LN_EPS)
    yn = yn * lng_ref[...] + lnb_ref[...]
    act = yn * jax.nn.sigmoid(yn)
    return _dot(act.astype(BF16), pw_ref[...]) + pwb_ref[...]


CONV_PAD = 32
CONV_ROWS = 128
CONV_COLS = 128


def _mixer_prompt_body(sink_ref, q_ref, k_ref, v_ref, u_ref, dww_ref, dwb_ref, lng_ref, lnb_ref, pw_ref, pwb_ref,
                       mix_ref, kprev, vprev, uext, ushift, ybuf):
    i = pl.program_id(0)
    tm = q_ref.shape[0]

    @pl.when(i == 0)
    def _():
        kprev[...] = jnp.zeros_like(kprev)
        vprev[...] = jnp.zeros_like(vprev)
        uext[0:CONV_PAD, :] = jnp.zeros((CONV_PAD, D_CONV), F32)

    qi = lax.broadcasted_iota(jnp.int32, (WINDOW, 2 * WINDOW), 0)
    kj = lax.broadcasted_iota(jnp.int32, (WINDOW, 2 * WINDOW), 1)
    diff = qi + WINDOW - kj
    band = (diff >= 0) & (diff <= WINDOW)
    kp, vp = kprev[...], vprev[...]
    for b in range(tm // WINDOW):
        rows = slice(b * WINDOW, (b + 1) * WINDOW)
        kb = k_ref[rows, :].astype(BF16)
        vb = v_ref[rows, :].astype(BF16)
        kk = jnp.concatenate([kp, kb], axis=0)
        vv = jnp.concatenate([vp, vb], axis=0)
        if b == 0:
            mask = band & (kj >= jnp.where(i > 0, 0, WINDOW))
        else:
            mask = band
        for h in range(N_KV_HEADS):
            cols = slice(h * HEAD_DIM, (h + 1) * HEAD_DIM)
            hq = slice(h * GQA_GROUP * HEAD_DIM, (h + 1) * GQA_GROUP * HEAD_DIM)
            o = _attend(q_ref[rows, hq], [(kk[:, cols], vv[:, cols], mask)], sink_ref, h)
            mix_ref[rows, hq] = o.astype(BF16)
        kp, vp = kb, vb
    kprev[...] = kp
    vprev[...] = vp

    uext[CONV_PAD:CONV_PAD + tm, :] = u_ref[...]
    n_shift_rows = ushift.shape[1]
    for s in range(1, SUBLANES):
        ushift[s - 1] = uext[s:s + n_shift_rows, :]
    off = CONV_PAD - (CONV_W - 1)
    for r in range(tm // CONV_ROWS):
        for c in range(D_CONV // CONV_COLS):
            cs = slice(c * CONV_COLS, (c + 1) * CONV_COLS)
            acc = jnp.zeros((CONV_ROWS, CONV_COLS), F32)
            for j in range(CONV_W):
                a, s = divmod(off + j, SUBLANES)
                r0 = r * CONV_ROWS + a * SUBLANES
                src = uext[r0:r0 + CONV_ROWS, cs] if s == 0 else ushift[s - 1, r0:r0 + CONV_ROWS, cs]
                acc = acc + dww_ref[j:j + 1, cs] * src
            ybuf[r * CONV_ROWS:(r + 1) * CONV_ROWS, cs] = acc
    uext[0:CONV_PAD, :] = uext[tm:tm + CONV_PAD, :]
    conv = _conv_tail(ybuf[...], dwb_ref, lng_ref, lnb_ref, pw_ref, pwb_ref)
    mix_ref[:, D_ATTN:] = conv.astype(BF16)


def _mixer_prompt(sink, q, k, v, u, dww, dwb, lng, lnb, pw_bf, pwb):
    n = q.shape[0]
    tm = TM_MIX
    row = lambda w: pl.BlockSpec((tm, w), lambda i, s: (i, 0))
    res = lambda shape: pl.BlockSpec(shape, lambda i, s: (0,) * len(shape), pipeline_mode=pl.Buffered(1))
    return pl.pallas_call(
        _mixer_prompt_body,
        grid_spec=pltpu.PrefetchScalarGridSpec(
            num_scalar_prefetch=1,
            grid=(n // tm,),
            in_specs=[row(D_ATTN), row(KV_DIM), row(KV_DIM), row(D_CONV), res((CONV_W, D_CONV)), res((1, D_CONV)),
                      res((1, D_CONV)), res((1, D_CONV)), res((D_CONV, D_CONV)), res((1, D_CONV))],
            out_specs=row(D_MODEL),
            scratch_shapes=[pltpu.VMEM((WINDOW, KV_DIM), BF16), pltpu.VMEM((WINDOW, KV_DIM), BF16),
                            pltpu.VMEM((CONV_PAD + tm, D_CONV), F32),
                            pltpu.VMEM((SUBLANES - 1, CONV_PAD - SUBLANES + tm, D_CONV), F32),
                            pltpu.VMEM((tm, D_CONV), F32)],
        ),
        out_shape=jax.ShapeDtypeStruct((n, D_MODEL), BF16),
        compiler_params=_params(),
        name="mixer_prompt",
    )(sink, q, k, v, u, dww, dwb, lng, lnb, pw_bf, pwb)


def _mixer_sample_body(sink_ref, q_ref, k_ref, v_ref, u_ref, ck_ref, cv_ref, cs_ref, dww_ref, dwb_ref, lng_ref,
                       lnb_ref, pw_ref, pwb_ref, mix_ref, wk_ref, wv_ref, wc_ref, usel, ybuf):
    nt = q_ref.shape[0]
    sb = ck_ref.shape[0]
    t_new = nt // sb
    n_cache = sb * WINDOW
    n_state = CONV_W - 1

    pad = jnp.zeros((LANES - nt, KV_DIM), F32)
    kn_t = jnp.concatenate([k_ref[...], pad], axis=0).T
    vn_t = jnp.concatenate([v_ref[...], pad], axis=0).T

    lt, lw = t_new.bit_length() - 1, WINDOW.bit_length() - 1
    qr = lax.broadcasted_iota(jnp.int32, (nt, n_cache), 0)
    cc = lax.broadcasted_iota(jnp.int32, (nt, n_cache), 1)
    mask_c = ((cc >> lw) == (qr >> lt)) & ((cc & (WINDOW - 1)) >= (qr & (t_new - 1)))
    qr = lax.broadcasted_iota(jnp.int32, (nt, LANES), 0)
    cn = lax.broadcasted_iota(jnp.int32, (nt, LANES), 1)
    mask_n = ((cn >> lt) == (qr >> lt)) & ((cn & (t_new - 1)) <= (qr & (t_new - 1)))
    for h in range(N_KV_HEADS):
        dims = slice(h * HEAD_DIM, (h + 1) * HEAD_DIM)
        hq = slice(h * GQA_GROUP * HEAD_DIM, (h + 1) * GQA_GROUP * HEAD_DIM)
        kc = jnp.concatenate([ck_ref[s, h] for s in range(sb)], axis=1).astype(BF16)
        vc = jnp.concatenate([cv_ref[s, h] for s in range(sb)], axis=1).astype(BF16)
        parts = [(kc, vc, mask_c), (kn_t[dims, :].astype(BF16), vn_t[dims, :].astype(BF16), mask_n)]
        o = _attend(q_ref[:, hq], parts, sink_ref, h, keys_on_lanes=True)
        mix_ref[:, hq] = o.astype(BF16)

    is_new = lax.broadcasted_iota(jnp.int32, (HEAD_DIM, WINDOW), 1) >= WINDOW - t_new
    for s in range(sb):
        shift = (WINDOW - t_new - s * t_new) % LANES
        kn_s = pltpu.roll(kn_t, shift=shift, axis=1)
        vn_s = pltpu.roll(vn_t, shift=shift, axis=1)
        for h in range(N_KV_HEADS):
            dims = slice(h * HEAD_DIM, (h + 1) * HEAD_DIM)
            old_k = pltpu.roll(ck_ref[s, h], shift=WINDOW - t_new, axis=1)
            old_v = pltpu.roll(cv_ref[s, h], shift=WINDOW - t_new, axis=1)
            wk_ref[s, h] = jnp.where(is_new, kn_s[dims, :], old_k)
            wv_ref[s, h] = jnp.where(is_new, vn_s[dims, :], old_v)

    for s in range(sb):
        for t in range(t_new):
            usel[t, s:s + 1, :] = u_ref[s * t_new + t:s * t_new + t + 1, :]
    u_new = [usel[t] for t in range(t_new)]
    for t in range(t_new):
        acc = jnp.zeros((sb, D_CONV), F32)
        for r in range(t, n_state):
            acc = acc + dww_ref[r - t:r - t + 1, :] * cs_ref[r]
        for t2 in range(t + 1):
            j = n_state - t + t2
            acc = acc + dww_ref[j:j + 1, :] * u_new[t2]
        for s in range(sb):
            ybuf[s * t_new + t:s * t_new + t + 1, :] = acc[s:s + 1, :]
        wc_ref[n_state - t_new + t] = u_new[t]
    wc_ref[0:n_state - t_new] = cs_ref[t_new:n_state]
    conv = _conv_tail(ybuf[...], dwb_ref, lng_ref, lnb_ref, pw_ref, pwb_ref)
    mix_ref[:, D_ATTN:] = conv.astype(BF16)


def _mixer_sample(sink, q, k, v, u, ck, cv, cs, dww, dwb, lng, lnb, pw_bf, pwb):
    n = q.shape[0]
    n_seq = ck.shape[1]
    t_new = n // n_seq
    sb = SEQ_TILE
    nt = sb * t_new
    n_state = CONV_W - 1
    row = lambda w: pl.BlockSpec((nt, w), lambda i, s: (i, 0))
    cache = pl.BlockSpec((None, sb, N_KV_HEADS, HEAD_DIM, WINDOW), lambda i, s: (0, i, 0, 0, 0))
    state = pl.BlockSpec((None, n_state, sb, D_CONV), lambda i, s: (0, 0, i, 0))
    res = lambda shape: pl.BlockSpec(shape, lambda i, s: (0,) * len(shape), pipeline_mode=pl.Buffered(1))
    return pl.pallas_call(
        _mixer_sample_body,
        grid_spec=pltpu.PrefetchScalarGridSpec(
            num_scalar_prefetch=1,
            grid=(n_seq // sb,),
            in_specs=[row(D_ATTN), row(KV_DIM), row(KV_DIM), row(D_CONV), cache, cache, state,
                      res((CONV_W, D_CONV)), res((1, D_CONV)), res((1, D_CONV)), res((1, D_CONV)),
                      res((D_CONV, D_CONV)), res((1, D_CONV))],
            out_specs=[row(D_MODEL), cache, cache, state],
            scratch_shapes=[pltpu.VMEM((t_new, sb, D_CONV), F32), pltpu.VMEM((nt, D_CONV), F32)],
        ),
        out_shape=[jax.ShapeDtypeStruct((n, D_MODEL), BF16), jax.ShapeDtypeStruct(ck.shape, F32),
                   jax.ShapeDtypeStruct(cv.shape, F32), jax.ShapeDtypeStruct(cs.shape, F32)],
        compiler_params=_params(),
        name="mixer_sample",
    )(sink, q, k, v, u, ck, cv, cs, dww, dwb, lng, lnb, pw_bf, pwb)


META_E0, META_E1, META_G0, META_G1, META_R0, META_R1 = range(6)


def _out_router_body(mixp_ref, xp_ref, mixs_ref, xs_ref, wout_ref, g2_ref, rw_ref, rb_ref, h_ref, meta_ref, cnt_ref,
                     carry, lower, *, prompt_tiles):
    i = pl.program_id(0)
    tm = xp_ref.shape[0]

    @pl.when(i == 0)
    def _():
        carry[...] = jnp.zeros_like(carry)
        ti = lax.broadcasted_iota(jnp.int32, (tm, tm), 0)
        tj = lax.broadcasted_iota(jnp.int32, (tm, tm), 1)
        lower[...] = jnp.where(tj < ti, 1.0, 0.0).astype(BF16)

    is_prompt = i < prompt_tiles
    x = jnp.where(is_prompt, xp_ref[...], xs_ref[...])
    mix = jnp.where(is_prompt, mixp_ref[...], mixs_ref[...])
    h = x + _dot(mix, wout_ref[...])
    h_ref[...] = h
    xn = _rmsnorm(h, g2_ref[...]).astype(BF16)
    logits = _dot(xn, rw_ref[...]) + rb_ref[...]
    lane = lax.broadcasted_iota(jnp.int32, logits.shape, 1).astype(F32)
    first = lambda cond: jnp.min(jnp.where(cond, lane, float(LANES)), axis=-1, keepdims=True)

    gl = jnp.where(lane < N_GROUPS, logits, -jnp.inf)
    gmax = jnp.max(gl, axis=-1, keepdims=True)
    gidx = first(gl == gmax)
    gval = 1.0 / jnp.sum(jnp.exp(gl - gmax), axis=-1, keepdims=True)

    lo = ROUTER_LANE0 + EXPERTS_PER_GROUP * gidx
    el = jnp.where((lane >= lo) & (lane < lo + EXPERTS_PER_GROUP), logits, -jnp.inf)
    m1 = jnp.max(el, axis=-1, keepdims=True)
    i1 = first(el == m1)
    el2 = jnp.where(lane == i1, -jnp.inf, el)
    m2 = jnp.max(el2, axis=-1, keepdims=True)
    i2 = first(el2 == m2)
    r = jnp.exp(m2 - m1)
    g0 = gval * (1.0 / (1.0 + r))
    g1 = gval * (r / (1.0 + r))

    sel0, sel1 = lane == i1, lane == i2
    onehot = jnp.where(sel0 | sel1, 1.0, 0.0)
    before = _dot(lower[...], onehot.astype(BF16)) + carry[0:1, :]
    r0 = jnp.sum(jnp.where(sel0, before, 0.0), axis=-1, keepdims=True)
    r1 = jnp.sum(jnp.where(sel1, before, 0.0), axis=-1, keepdims=True)
    carry[...] = carry[...] + jnp.sum(onehot, axis=0, keepdims=True)
    cnt_ref[...] = carry[...]

    lane_i = lax.broadcasted_iota(jnp.int32, logits.shape, 1)
    meta = jnp.zeros(logits.shape, F32)
    for slot, val in ((META_E0, i1 - ROUTER_LANE0), (META_E1, i2 - ROUTER_LANE0), (META_G0, g0), (META_G1, g1),
                      (META_R0, r0), (META_R1, r1)):
        meta = jnp.where(lane_i == slot, val, meta)
    meta_ref[...] = meta


def _out_router(mix_p, x_p, mix_s, x_s, wout_bf, g2, rw_bf, rb):
    n_p, n_s = x_p.shape[0], x_s.shape[0]
    tm = TM_PROJ
    assert n_p % tm == 0 and n_s % tm == 0
    tp, ts = n_p // tm, n_s // tm
    n = n_p + n_s
    prow = lambda w: pl.BlockSpec((tm, w), lambda i: (jnp.minimum(i, tp - 1), 0))
    srow = lambda w: pl.BlockSpec((tm, w), lambda i: (jnp.maximum(i - tp, 0), 0))
    row = lambda w: pl.BlockSpec((tm, w), lambda i: (i, 0))
    return pl.pallas_call(
        functools.partial(_out_router_body, prompt_tiles=tp),
        grid=(tp + ts,),
        in_specs=[prow(D_MODEL), prow(D_MODEL), srow(D_MODEL), srow(D_MODEL), _resident((D_MODEL, D_MODEL)),
                  _resident((1, D_MODEL)), _resident((D_MODEL, LANES)), _resident((1, LANES))],
        out_specs=[row(D_MODEL), row(LANES), pl.BlockSpec((8, LANES), lambda i: (0, 0))],
        out_shape=[jax.ShapeDtypeStruct((n, D_MODEL), F32), jax.ShapeDtypeStruct((n, LANES), F32),
                   jax.ShapeDtypeStruct((8, LANES), F32)],
        scratch_shapes=[pltpu.VMEM((8, LANES), F32), pltpu.VMEM((tm, tm), BF16)],
        compiler_params=_params(),
        name="out_router",
    )(mix_p, x_p, mix_s, x_s, wout_bf, g2, rw_bf, rb)


def _positions_body(meta_ref, off_ref, pos_ref):
    meta = meta_ref[...]
    lane = lax.broadcasted_iota(jnp.int32, meta.shape, 1)
    lane_f = lane.astype(F32)
    off = off_ref[...]
    pos = jnp.zeros(meta.shape, F32)
    for slot, (e_lane, r_lane) in enumerate(((META_E0, META_R0), (META_E1, META_R1))):
        e = meta[:, e_lane:e_lane + 1] + ROUTER_LANE0
        seg = jnp.sum(jnp.where(lane_f == e, off, 0.0), axis=-1, keepdims=True)
        pos = jnp.where(lane == slot, seg + meta[:, r_lane:r_lane + 1], pos)
    pos_ref[...] = pos.astype(jnp.int32)


def _positions(meta_all, off_lanes):
    n = meta_all.shape[0]
    tm = TM_PROJ
    assert n % tm == 0
    return pl.pallas_call(
        _positions_body,
        grid=(n // tm,),
        in_specs=[pl.BlockSpec((tm, LANES), lambda i: (i, 0)), _resident((1, LANES))],
        out_specs=pl.BlockSpec((tm, LANES), lambda i: (i, 0)),
        out_shape=jax.ShapeDtypeStruct((n, LANES), jnp.int32),
        compiler_params=_params(),
        name="positions",
    )(meta_all, off_lanes)


N_LOAD_SLOTS = 3


def _pow2_sizes(n):
    return [1 << b for b in range(n.bit_length() - 1, -1, -1)]


def _dispatch_body(pos_ref, nv_ref, h_ref, hs_ref, hbuf, zbuf, lsem, ssem, zsem, *, n_tok, tm):
    i = pl.program_id(0)
    n_steps = pl.num_programs(0)
    n_tiles = nv_ref.shape[0]

    groups = tm // SUBLANES

    def load(t, slot, sem=lsem):
        return pltpu.make_async_copy(h_ref.at[pl.ds(t * groups, groups)], hbuf.at[slot], sem.at[slot])

    def clear_unowned(start):
        def per_tile(t, c):
            nv = nv_ref[t]
            z = TR - nv
            row = t * TR + nv
            head = z & (SUBLANES - 1)
            for k in range(SUBLANES - 1):
                @pl.when(k < head)
                def _(k=k):
                    cp = pltpu.make_async_copy(zbuf.at[pl.ds(0, 1), :], hs_ref.at[pl.ds(row + k, 1), :], zsem)
                    cp.start() if start else cp.wait()
            row = row + head
            for size in _pow2_sizes(TR):
                if size < SUBLANES:
                    break
                @pl.when((z & size) != 0)
                def _(row=row, size=size):
                    dst = hs_ref.at[pl.ds(pl.multiple_of(row, SUBLANES), size), :]
                    cp = pltpu.make_async_copy(zbuf.at[pl.ds(0, size), :], dst, zsem)
                    cp.start() if start else cp.wait()
                row = row + (z & size)
            return c
        lax.fori_loop(0, n_tiles, per_tile, 0)

    def scatter_wait(slot):
        for _ in range(2):
            load(0, slot, ssem).wait()

    @pl.when(i == 0)
    def _():
        zbuf[...] = jnp.zeros_like(zbuf)
        clear_unowned(True)
        load(0, 0).start()

    @pl.when(i + 1 < n_steps)
    def _():
        load(i + 1, (i + 1) % N_LOAD_SLOTS).start()

    slot = i % N_LOAD_SLOTS
    load(i, slot).wait()

    def issue(jb, c):
        for u in range(SUBLANES):
            for s in range(2):
                p = pos_ref[s * n_tok + i * tm + jb * SUBLANES + u]
                pltpu.make_async_copy(hbuf.at[slot, jb, pl.ds(u, 1), :], hs_ref.at[pl.ds(p, 1), :],
                                      ssem.at[slot]).start()
        return c
    lax.fori_loop(0, groups, issue, 0)

    @pl.when(i > 0)
    def _():
        scatter_wait((i + N_LOAD_SLOTS - 1) % N_LOAD_SLOTS)

    @pl.when(i == n_steps - 1)
    def _():
        scatter_wait(slot)
        clear_unowned(False)


def _dispatch(pos, nv, rows):
    n_tok, width = rows.shape
    n_tiles = nv.shape[0]
    tm = TM_PROJ
    assert n_tok % tm == 0
    return pl.pallas_call(
        functools.partial(_dispatch_body, n_tok=n_tok, tm=tm),
        grid_spec=pltpu.PrefetchScalarGridSpec(
            num_scalar_prefetch=2,
            grid=(n_tok // tm,),
            in_specs=[pl.BlockSpec(memory_space=pl.ANY)],
            out_specs=pl.BlockSpec(memory_space=pl.ANY),
            scratch_shapes=[pltpu.VMEM((N_LOAD_SLOTS, tm // SUBLANES, SUBLANES, width), rows.dtype),
                            pltpu.VMEM((TR, width), rows.dtype),
                            pltpu.SemaphoreType.DMA((N_LOAD_SLOTS,)), pltpu.SemaphoreType.DMA((N_LOAD_SLOTS,)),
                            pltpu.SemaphoreType.DMA(())],
        ),
        out_shape=jax.ShapeDtypeStruct((n_tiles * TR, width), rows.dtype),
        compiler_params=_params(),
        name="dispatch",
    )(pos, nv, rows.reshape(n_tok // SUBLANES, SUBLANES, width))


N_MOE_LOADS = 4


def _moe_body(ts_ref, hs_ref, g2_ref, wg_ref, wu_ref, wd_ref, ys_ref, xbuf, obuf, wg_bf, wu_bf, wd_bf, lsem, ssem):
    e = pl.program_id(0)
    n_tiles = ys_ref.shape[0] // TR
    n_used = ts_ref[N_EXPERTS]
    t0, t1 = ts_ref[e], ts_ref[e + 1]

    def load(g, slot):
        return pltpu.make_async_copy(hs_ref.at[pl.ds(pl.multiple_of(g * TR, TR), TR), :], xbuf.at[slot], lsem.at[slot])

    def store(g, slot):
        return pltpu.make_async_copy(obuf.at[slot], ys_ref.at[pl.ds(pl.multiple_of(g * TR, TR), TR), :], ssem.at[slot])

    @pl.when(e == 0)
    def _():
        for g in range(N_MOE_LOADS - 1):
            @pl.when(g < n_used)
            def _(g=g):
                load(g, g).start()

    @pl.when(t1 > t0)
    def _():
        wg_bf[...] = wg_ref[0].astype(BF16)
        wu_bf[...] = wu_ref[0].astype(BF16)
        wd_bf[...] = wd_ref[0].astype(BF16)

    def tile(g, c):
        slot = g % 2
        xslot = g % N_MOE_LOADS
        ahead = g + N_MOE_LOADS - 1

        @pl.when(ahead < n_used)
        def _():
            load(ahead, ahead % N_MOE_LOADS).start()

        load(g, xslot).wait()
        xn = _rmsnorm(xbuf[xslot], g2_ref[...]).astype(BF16)
        hg = _dot(xn, wg_bf[...])
        hu = _dot(xn, wu_bf[...])
        hid = (hg * jax.nn.sigmoid(hg) * hu).astype(BF16)
        y = _dot(hid, wd_bf[...])

        @pl.when(g >= 2)
        def _():
            store(g - 2, slot).wait()

        obuf[slot] = y
        store(g, slot).start()
        return c

    lax.fori_loop(t0, t1, tile, 0)

    @pl.when(e == pl.num_programs(0) - 1)
    def _():
        @pl.when(n_used >= 2)
        def _():
            store(n_used - 2, n_used % 2).wait()
        store(n_used - 1, (n_used - 1) % 2).wait()
        obuf[0] = jnp.zeros(obuf.shape[1:], obuf.dtype)

        def clear(g, c):
            store(g, 0).start()
            return c
        lax.fori_loop(n_used, n_tiles, clear, 0)

        def drain(g, c):
            store(g, 0).wait()
            return c
        lax.fori_loop(n_used, n_tiles, drain, 0)


def _moe(ts, hs, g2, wg, wu, wd):
    wspec = lambda shape: pl.BlockSpec((1,) + shape, lambda e, ts: (e, 0, 0))
    tile_buf = lambda slots: pltpu.VMEM((slots, TR, hs.shape[1]), hs.dtype)
    return pl.pallas_call(
        _moe_body,
        grid_spec=pltpu.PrefetchScalarGridSpec(
            num_scalar_prefetch=1,
            grid=(N_EXPERTS,),
            in_specs=[pl.BlockSpec(memory_space=pl.ANY),
                      pl.BlockSpec((1, D_MODEL), lambda e, ts: (0, 0)),
                      wspec((D_MODEL, D_EXPERT)), wspec((D_MODEL, D_EXPERT)), wspec((D_EXPERT, D_MODEL))],
            out_specs=pl.BlockSpec(memory_space=pl.ANY),
            scratch_shapes=[tile_buf(N_MOE_LOADS), tile_buf(2),
                            pltpu.VMEM((D_MODEL, D_EXPERT), BF16), pltpu.VMEM((D_MODEL, D_EXPERT), BF16),
                            pltpu.VMEM((D_EXPERT, D_MODEL), BF16),
                            pltpu.SemaphoreType.DMA((N_MOE_LOADS,)), pltpu.SemaphoreType.DMA((2,))],
        ),
        out_shape=jax.ShapeDtypeStruct(hs.shape, hs.dtype),
        compiler_params=_params(),
        name="moe",
    )(ts, hs, g2, wg, wu, wd)


def _combine_body(pos_ref, h_ref, meta_ref, g_ref, ys_ref, ys_grouped_ref, o_ref, ybuf, sem, *, n_tok, row0):
    i = pl.program_id(0)
    n_steps = pl.num_programs(0)
    tm = h_ref.shape[0]

    groups = tm // SUBLANES

    def gather(t, slot):
        def issue(jb, c):
            for u in range(SUBLANES):
                for s in range(2):
                    p = pos_ref[s * n_tok + row0 + t * tm + jb * SUBLANES + u]
                    pltpu.make_async_copy(ys_ref.at[pl.ds(p, 1), :], ybuf.at[slot, s, jb, pl.ds(u, 1), :],
                                          sem.at[slot]).start()
            return c
        lax.fori_loop(0, groups, issue, 0)

    @pl.when(i == 0)
    def _():
        gather(0, 0)

    @pl.when(i + 1 < n_steps)
    def _():
        gather(i + 1, (i + 1) % 2)

    slot = i % 2
    for s in range(2):
        pltpu.make_async_copy(ys_grouped_ref.at[pl.ds(0, groups)], ybuf.at[slot, s], sem.at[slot]).wait()
    meta = meta_ref[...]
    g0 = meta[:, META_G0:META_G0 + 1]
    g1 = meta[:, META_G1:META_G1 + 1]
    y0 = ybuf[slot, 0].reshape(tm, D_MODEL)
    y1 = ybuf[slot, 1].reshape(tm, D_MODEL)
    h = h_ref[...] + (g0 * y0 + g1 * y1)
    o_ref[...] = _rmsnorm(h, g_ref[...])


def _combine(pos, h_all, meta_all, ys, gf, row0, n):
    n_tok = h_all.shape[0]
    tm = min(TM_OUT, n)
    assert row0 % tm == 0 and n % tm == 0
    b0 = row0 // tm
    return pl.pallas_call(
        functools.partial(_combine_body, n_tok=n_tok, row0=row0),
        grid_spec=pltpu.PrefetchScalarGridSpec(
            num_scalar_prefetch=1,
            grid=(n // tm,),
            in_specs=[pl.BlockSpec((tm, D_MODEL), lambda i, pos: (b0 + i, 0)),
                      pl.BlockSpec((tm, LANES), lambda i, pos: (b0 + i, 0)),
                      pl.BlockSpec((1, D_MODEL), lambda i, pos: (0, 0)),
                      pl.BlockSpec(memory_space=pl.ANY), pl.BlockSpec(memory_space=pl.ANY)],
            out_specs=pl.BlockSpec((tm, D_MODEL), lambda i, pos: (i, 0)),
            scratch_shapes=[pltpu.VMEM((2, 2, tm // SUBLANES, SUBLANES, ys.shape[1]), ys.dtype),
                            pltpu.SemaphoreType.DMA((2,))],
        ),
        out_shape=jax.ShapeDtypeStruct((n, D_MODEL), F32),
        compiler_params=_params(),
        name="combine",
    )(pos, h_all, meta_all, gf, ys, ys.reshape(ys.shape[0] // SUBLANES, SUBLANES, ys.shape[1]))


def _routing_tables(counts, n_tok):
    n_tiles = (2 * n_tok) // TR + N_EXPERTS
    cnt = counts[ROUTER_LANE0:ROUTER_LANE0 + N_EXPERTS].astype(jnp.int32)
    tiles_e = (cnt + TR - 1) // TR
    tile_end = jnp.cumsum(tiles_e)
    tile_start = tile_end - tiles_e
    off_lanes = jnp.pad((tile_start * TR).astype(F32), (ROUTER_LANE0, LANES - ROUTER_LANE0 - N_EXPERTS))
    ts = jnp.concatenate([tile_start, tile_end[-1:]]).astype(jnp.int32)
    tile = jnp.arange(n_tiles, dtype=jnp.int32)
    owner = (tile[:, None] >= tile_start[None, :]) & (tile[:, None] < tile_end[None, :])
    rows_left = jnp.sum(jnp.where(owner, cnt[None, :] - (tile[:, None] - tile_start[None, :]) * TR, 0), axis=1)
    nv = jnp.clip(rows_left, 0, TR).astype(jnp.int32)
    return off_lanes.reshape(1, LANES), ts, nv


def kernel(x_prompt, x_sample, cache_win_k, cache_win_v, state_conv, norm1_g, w_in, attn_sink, conv_dw_w, conv_dw_b,
           conv_ln_g, conv_ln_b, conv_pw_w, conv_pw_b, w_out, norm2_g, router_group_w, router_group_b,
           router_expert_w, router_expert_b, expert_w_gate, expert_w_up, expert_w_down, final_norm_g):
    depth = w_in.shape[0]
    assert depth == 1, "single-layer step"
    bp, sp, _ = x_prompt.shape
    assert bp == 1, "one prompt sequence"
    n_seq, t_new, _ = x_sample.shape
    n_p, n_s = bp * sp, n_seq * t_new
    n_tok = n_p + n_s
    l = 0

    row = lambda a: a.reshape(1, -1)
    w_in_bf = w_in[l].astype(BF16)
    w_out_bf = w_out[l].astype(BF16)
    pw_bf = conv_pw_w[l].astype(BF16)
    rw = jnp.concatenate([router_group_w[l], router_expert_w[l]], axis=1)
    rw_bf = jnp.pad(rw, ((0, 0), (0, LANES - rw.shape[1]))).astype(BF16)
    rb = jnp.pad(jnp.concatenate([router_group_b[l], router_expert_b[l]]), (0, LANES - rw.shape[1])).reshape(1, LANES)
    g1, g2, gf = row(norm1_g[l]), row(norm2_g[l]), row(final_norm_g)
    sink = attn_sink[l]
    conv_w = (conv_dw_w[l], row(conv_dw_b[l]), row(conv_ln_g[l]), row(conv_ln_b[l]), pw_bf, row(conv_pw_b[l]))

    xp = x_prompt.reshape(n_p, D_MODEL)
    xs = x_sample.reshape(n_s, D_MODEL)
    cache_axes, cache_axes_back = (0, 1, 3, 4, 2), (0, 1, 4, 2, 3)
    ck = jnp.transpose(cache_win_k, cache_axes)
    cv = jnp.transpose(cache_win_v, cache_axes)
    cs = jnp.transpose(state_conv, (0, 2, 1, 3))

    qp, kp, vp, up = _in_proj(xp, g1, w_in_bf)
    qs, ks, vs, us = _in_proj(xs, g1, w_in_bf)
    mix_p = _mixer_prompt(sink, qp, kp, vp, up, *conv_w)
    mix_s, wk_s, wv_s, wc_s = _mixer_sample(sink, qs, ks, vs, us, ck, cv, cs, *conv_w)

    h_all, meta_all, cnt = _out_router(mix_p, xp, mix_s, xs, w_out_bf, g2, rw_bf, rb)

    off_lanes, ts, nv = _routing_tables(cnt[0], n_tok)
    pos_lanes = _positions(meta_all, off_lanes)
    pos = jnp.concatenate([pos_lanes[:, 0], pos_lanes[:, 1]])
    hs = _dispatch(pos, nv, h_all)
    ys = _moe(ts, hs, g2, expert_w_gate[l], expert_w_up[l], expert_w_down[l])
    y_p = _combine(pos, h_all, meta_all, ys, gf, 0, n_p)
    y_s = _combine(pos, h_all, meta_all, ys, gf, n_p, n_s)

    kv_shape = (depth, bp, WINDOW, N_KV_HEADS, HEAD_DIM)
    return (y_p.reshape(bp, sp, D_MODEL), y_s.reshape(n_seq, t_new, D_MODEL),
            kp[n_p - WINDOW:].reshape(kv_shape), vp[n_p - WINDOW:].reshape(kv_shape),
            up[n_p - (CONV_W - 1):].reshape(depth, bp, CONV_W - 1, D_CONV),
            jnp.transpose(wk_s, cache_axes_back), jnp.transpose(wv_s, cache_axes_back),
            jnp.transpose(wc_s, (0, 2, 1, 3)))
```

```python
import functools

import jax
import jax.numpy as jnp
from jax import lax
from jax.experimental import pallas as pl
from jax.experimental.pallas import tpu as pltpu

F32 = jnp.float32
BF16 = jnp.bfloat16

D_MODEL = 2048
HEAD_DIM = 64
N_HEADS = 16
N_KV_HEADS = 2
GQA_GROUP = 8
KV_DIM = N_KV_HEADS * HEAD_DIM
D_ATTN = N_HEADS * HEAD_DIM
D_CONV = D_MODEL - D_ATTN
WINDOW = 128
CONV_W = 31
D_IN = D_ATTN + 2 * KV_DIM + 2 * D_CONV
N_GROUPS = 4
EXPERTS_PER_GROUP = 8
N_EXPERTS = N_GROUPS * EXPERTS_PER_GROUP
D_EXPERT = 256
RMS_EPS = 1e-6
LN_EPS = 1e-5

LANES = 128
SUBLANES = 8
ROUTER_LANE0 = N_GROUPS
VMEM_LIMIT = 56 * 1024 * 1024

TM_PROJ = 512
TM_MIX = 256
SEQ_TILE = 8
TR = 128
TM_OUT = 256

_NT = (((1,), (1,)), ((), ()))


def _params(n_axes=1):
    return pltpu.CompilerParams(dimension_semantics=("arbitrary",) * n_axes, vmem_limit_bytes=VMEM_LIMIT)


def _resident(shape):
    return pl.BlockSpec(shape, lambda *_: (0,) * len(shape), pipeline_mode=pl.Buffered(1))


def _rmsnorm(x, g):
    ms = jnp.mean(x * x, axis=-1, keepdims=True)
    return x * lax.rsqrt(ms + RMS_EPS) * g


def _dot(a, b):
    return jnp.dot(a, b, preferred_element_type=F32)


def _dot_nt(a, b):
    return lax.dot_general(a, b, _NT, preferred_element_type=F32)


def _in_proj_body(x_ref, g_ref, w_ref, q_ref, k_ref, v_ref, u_ref):
    xn = _rmsnorm(x_ref[...], g_ref[...]).astype(BF16)
    q_ref[...] = (_dot(xn, w_ref[:, :D_ATTN]) * (HEAD_DIM ** -0.5)).astype(BF16)
    kv = _dot(xn, w_ref[:, D_ATTN:D_ATTN + 2 * KV_DIM])
    k_ref[...] = kv[:, :KV_DIM]
    v_ref[...] = kv[:, KV_DIM:]
    c0 = D_ATTN + 2 * KV_DIM
    cw = 256
    for j in range(D_CONV // cw):
        a = _dot(xn, w_ref[:, c0 + j * cw:c0 + (j + 1) * cw])
        b = _dot(xn, w_ref[:, c0 + D_CONV + j * cw:c0 + D_CONV + (j + 1) * cw])
        u_ref[:, j * cw:(j + 1) * cw] = a * jax.nn.sigmoid(b)


def _in_proj(x, g, w_bf):
    n = x.shape[0]
    tm = min(TM_PROJ, n)
    row = lambda w: pl.BlockSpec((tm, w), lambda i: (i, 0))
    return pl.pallas_call(
        _in_proj_body,
        grid=(n // tm,),
        in_specs=[row(D_MODEL), _resident((1, D_MODEL)), _resident((D_MODEL, D_IN))],
        out_specs=[row(D_ATTN), row(KV_DIM), row(KV_DIM), row(D_CONV)],
        out_shape=[jax.ShapeDtypeStruct((n, D_ATTN), BF16), jax.ShapeDtypeStruct((n, KV_DIM), F32),
                   jax.ShapeDtypeStruct((n, KV_DIM), F32), jax.ShapeDtypeStruct((n, D_CONV), F32)],
        compiler_params=_params(),
        name="in_proj",
    )(x, g, w_bf)


def _attend(q, parts, sink_ref, kv_head, keys_on_lanes=False):
    t = q.shape[0]
    score, weigh = (_dot, _dot_nt) if keys_on_lanes else (_dot_nt, _dot)
    qs = jnp.concatenate([q[:, g * HEAD_DIM:(g + 1) * HEAD_DIM] for g in range(GQA_GROUP)], axis=0)
    scores = [score(qs, kk) for kk, _, _ in parts]
    hidden = [jnp.where(mask, 0.0, -jnp.inf) for _, _, mask in parts]
    ps = [[] for _ in parts]
    inv = []
    for g in range(GQA_GROUP):
        sg = [s[g * t:(g + 1) * t] + neg for s, neg in zip(scores, hidden)]
        sk = sink_ref[kv_head * GQA_GROUP + g]
        m = sk
        for x in sg:
            m = jnp.maximum(jnp.max(x, axis=-1, keepdims=True), m)
        den = jnp.exp(sk - m)
        for k, x in enumerate(sg):
            p = jnp.exp(x - m)
            den = den + jnp.sum(p, axis=-1, keepdims=True)
            ps[k].append(p.astype(BF16))
        inv.append(1.0 / den)
    o = None
    for k, (_, vv, _) in enumerate(parts):
        ok = weigh(jnp.concatenate(ps[k], axis=0), vv)
        o = ok if o is None else o + ok
    return jnp.concatenate([o[g * t:(g + 1) * t] * inv[g] for g in range(GQA_GROUP)], axis=1)


def _conv_tail(y, dwb_ref, lng_ref, lnb_ref, pw_ref, pwb_ref):
    y = y + dwb_ref[...]
    mu = jnp.mean(y, axis=-1, keepdims=True)
    yc = y - mu
    yn = yc * lax.rsqrt(jnp.mean(yc * yc, axis=-1, keepdims=True) + LN_EPS)
    yn = yn * lng_ref[...] + lnb_ref[...]
    act = yn * jax.nn.sigmoid(yn)
    return _dot(act.astype(BF16), pw_ref[...]) + pwb_ref[...]


CONV_PAD = 32
CONV_ROWS = 128
CONV_COLS = 128


def _mixer_prompt_body(sink_ref, q_ref, k_ref, v_ref, u_ref, dww_ref, dwb_ref, lng_ref, lnb_ref, pw_ref, pwb_ref,
                       mix_ref, kprev, vprev, uext, ushift, ybuf):
    i = pl.program_id(0)
    tm = q_ref.shape[0]

    @pl.when(i == 0)
    def _():
        kprev[...] = jnp.zeros_like(kprev)
        vprev[...] = jnp.zeros_like(vprev)
        uext[0:CONV_PAD, :] = jnp.zeros((CONV_PAD, D_CONV), F32)

    qi = lax.broadcasted_iota(jnp.int32, (WINDOW, 2 * WINDOW), 0)
    kj = lax.broadcasted_iota(jnp.int32, (WINDOW, 2 * WINDOW), 1)
    diff = qi + WINDOW - kj
    band = (diff >= 0) & (diff <= WINDOW)
    kp, vp = kprev[...], vprev[...]
    for b in range(tm // WINDOW):
        rows = slice(b * WINDOW, (b + 1) * WINDOW)
        kb = k_ref[rows, :].astype(BF16)
        vb = v_ref[rows, :].astype(BF16)
        kk = jnp.concatenate([kp, kb], axis=0)
        vv = jnp.concatenate([vp, vb], axis=0)
        if b == 0:
            mask = band & (kj >= jnp.where(i > 0, 0, WINDOW))
        else:
            mask = band
        for h in range(N_KV_HEADS):
            cols = slice(h * HEAD_DIM, (h + 1) * HEAD_DIM)
            hq = slice(h * GQA_GROUP * HEAD_DIM, (h + 1) * GQA_GROUP * HEAD_DIM)
            o = _attend(q_ref[rows, hq], [(kk[:, cols], vv[:, cols], mask)], sink_ref, h)
            mix_ref[rows, hq] = o.astype(BF16)
        kp, vp = kb, vb
    kprev[...] = kp
    vprev[...] = vp

    uext[CONV_PAD:CONV_PAD + tm, :] = u_ref[...]
    n_shift_rows = ushift.shape[1]
    for s in range(1, SUBLANES):
        ushift[s - 1] = uext[s:s + n_shift_rows, :]
    off = CONV_PAD - (CONV_W - 1)
    for r in range(tm // CONV_ROWS):
        for c in range(D_CONV // CONV_COLS):
            cs = slice(c * CONV_COLS, (c + 1) * CONV_COLS)
            acc = jnp.zeros((CONV_ROWS, CONV_COLS), F32)
            for j in range(CONV_W):
                a, s = divmod(off + j, SUBLANES)
                r0 = r * CONV_ROWS + a * SUBLANES
                src = uext[r0:r0 + CONV_ROWS, cs] if s == 0 else ushift[s - 1, r0:r0 + CONV_ROWS, cs]
                acc = acc + dww_ref[j:j + 1, cs] * src
            ybuf[r * CONV_ROWS:(r + 1) * CONV_ROWS, cs] = acc
    uext[0:CONV_PAD, :] = uext[tm:tm + CONV_PAD, :]
    conv = _conv_tail(ybuf[...], dwb_ref, lng_ref, lnb_ref, pw_ref, pwb_ref)
    mix_ref[:, D_ATTN:] = conv.astype(BF16)


def _mixer_prompt(sink, q, k, v, u, dww, dwb, lng, lnb, pw_bf, pwb):
    n = q.shape[0]
    tm = TM_MIX
    row = lambda w: pl.BlockSpec((tm, w), lambda i, s: (i, 0))
    res = lambda shape: pl.BlockSpec(shape, lambda i, s: (0,) * len(shape), pipeline_mode=pl.Buffered(1))
    return pl.pallas_call(
        _mixer_prompt_body,
        grid_spec=pltpu.PrefetchScalarGridSpec(
            num_scalar_prefetch=1,
            grid=(n // tm,),
            in_specs=[row(D_ATTN), row(KV_DIM), row(KV_DIM), row(D_CONV), res((CONV_W, D_CONV)), res((1, D_CONV)),
                      res((1, D_CONV)), res((1, D_CONV)), res((D_CONV, D_CONV)), res((1, D_CONV))],
            out_specs=row(D_MODEL),
            scratch_shapes=[pltpu.VMEM((WINDOW, KV_DIM), BF16), pltpu.VMEM((WINDOW, KV_DIM), BF16),
                            pltpu.VMEM((CONV_PAD + tm, D_CONV), F32),
                            pltpu.VMEM((SUBLANES - 1, CONV_PAD - SUBLANES + tm, D_CONV), F32),
                            pltpu.VMEM((tm, D_CONV), F32)],
        ),
        out_shape=jax.ShapeDtypeStruct((n, D_MODEL), BF16),
        compiler_params=_params(),
        name="mixer_prompt",
    )(sink, q, k, v, u, dww, dwb, lng, lnb, pw_bf, pwb)


def _mixer_sample_body(sink_ref, q_ref, k_ref, v_ref, u_ref, ck_ref, cv_ref, cs_ref, dww_ref, dwb_ref, lng_ref,
                       lnb_ref, pw_ref, pwb_ref, mix_ref, wk_ref, wv_ref, wc_ref, usel, ybuf):
    nt = q_ref.shape[0]
    sb = ck_ref.shape[0]
    t_new = nt // sb
    n_cache = sb * WINDOW
    n_state = CONV_W - 1

    pad = jnp.zeros((LANES - nt, KV_DIM), F32)
    kn_t = jnp.concatenate([k_ref[...], pad], axis=0).T
    vn_t = jnp.concatenate([v_ref[...], pad], axis=0).T

    lt, lw = t_new.bit_length() - 1, WINDOW.bit_length() - 1
    qr = lax.broadcasted_iota(jnp.int32, (nt, n_cache), 0)
    cc = lax.broadcasted_iota(jnp.int32, (nt, n_cache), 1)
    mask_c = ((cc >> lw) == (qr >> lt)) & ((cc & (WINDOW - 1)) >= (qr & (t_new - 1)))
    qr = lax.broadcasted_iota(jnp.int32, (nt, LANES), 0)
    cn = lax.broadcasted_iota(jnp.int32, (nt, LANES), 1)
    mask_n = ((cn >> lt) == (qr >> lt)) & ((cn & (t_new - 1)) <= (qr & (t_new - 1)))
    for h in range(N_KV_HEADS):
        dims = slice(h * HEAD_DIM, (h + 1) * HEAD_DIM)
        hq = slice(h * GQA_GROUP * HEAD_DIM, (h + 1) * GQA_GROUP * HEAD_DIM)
        kc = jnp.concatenate([ck_ref[s, h] for s in range(sb)], axis=1).astype(BF16)
        vc = jnp.concatenate([cv_ref[s, h] for s in range(sb)], axis=1).astype(BF16)
        parts = [(kc, vc, mask_c), (kn_t[dims, :].astype(BF16), vn_t[dims, :].astype(BF16), mask_n)]
        o = _attend(q_ref[:, hq], parts, sink_ref, h, keys_on_lanes=True)
        mix_ref[:, hq] = o.astype(BF16)

    is_new = lax.broadcasted_iota(jnp.int32, (HEAD_DIM, WINDOW), 1) >= WINDOW - t_new
    for s in range(sb):
        shift = (WINDOW - t_new - s * t_new) % LANES
        kn_s = pltpu.roll(kn_t, shift=shift, axis=1)
        vn_s = pltpu.roll(vn_t, shift=shift, axis=1)
        for h in range(N_KV_HEADS):
            dims = slice(h * HEAD_DIM, (h + 1) * HEAD_DIM)
            old_k = pltpu.roll(ck_ref[s, h], shift=WINDOW - t_new, axis=1)
            old_v = pltpu.roll(cv_ref[s, h], shift=WINDOW - t_new, axis=1)
            wk_ref[s, h] = jnp.where(is_new, kn_s[dims, :], old_k)
            wv_ref[s, h] = jnp.where(is_new, vn_s[dims, :], old_v)

    for s in range(sb):
        for t in range(t_new):
            usel[t, s:s + 1, :] = u_ref[s * t_new + t:s * t_new + t + 1, :]
    u_new = [usel[t] for t in range(t_new)]
    for t in range(t_new):
        acc = jnp.zeros((sb, D_CONV), F32)
        for r in range(t, n_state):
            acc = acc + dww_ref[r - t:r - t + 1, :] * cs_ref[r]
        for t2 in range(t + 1):
            j = n_state - t + t2
            acc = acc + dww_ref[j:j + 1, :] * u_new[t2]
        for s in range(sb):
            ybuf[s * t_new + t:s * t_new + t + 1, :] = acc[s:s + 1, :]
        wc_ref[n_state - t_new + t] = u_new[t]
    wc_ref[0:n_state - t_new] = cs_ref[t_new:n_state]
    conv = _conv_tail(ybuf[...], dwb_ref, lng_ref, lnb_ref, pw_ref, pwb_ref)
    mix_ref[:, D_ATTN:] = conv.astype(BF16)


def _mixer_sample(sink, q, k, v, u, ck, cv, cs, dww, dwb, lng, lnb, pw_bf, pwb):
    n = q.shape[0]
    n_seq = ck.shape[1]
    t_new = n // n_seq
    sb = SEQ_TILE
    nt = sb * t_new
    n_state = CONV_W - 1
    row = lambda w: pl.BlockSpec((nt, w), lambda i, s: (i, 0))
    cache = pl.BlockSpec((None, sb, N_KV_HEADS, HEAD_DIM, WINDOW), lambda i, s: (0, i, 0, 0, 0))
    state = pl.BlockSpec((None, n_state, sb, D_CONV), lambda i, s: (0, 0, i, 0))
    res = lambda shape: pl.BlockSpec(shape, lambda i, s: (0,) * len(shape), pipeline_mode=pl.Buffered(1))
    return pl.pallas_call(
        _mixer_sample_body,
        grid_spec=pltpu.PrefetchScalarGridSpec(
            num_scalar_prefetch=1,
            grid=(n_seq // sb,),
            in_specs=[row(D_ATTN), row(KV_DIM), row(KV_DIM), row(D_CONV), cache, cache, state,
                      res((CONV_W, D_CONV)), res((1, D_CONV)), res((1, D_CONV)), res((1, D_CONV)),
                      res((D_CONV, D_CONV)), res((1, D_CONV))],
            out_specs=[row(D_MODEL), cache, cache, state],
            scratch_shapes=[pltpu.VMEM((t_new, sb, D_CONV), F32), pltpu.VMEM((nt, D_CONV), F32)],
        ),
        out_shape=[jax.ShapeDtypeStruct((n, D_MODEL), BF16), jax.ShapeDtypeStruct(ck.shape, F32),
                   jax.ShapeDtypeStruct(cv.shape, F32), jax.ShapeDtypeStruct(cs.shape, F32)],
        compiler_params=_params(),
        name="mixer_sample",
    )(sink, q, k, v, u, ck, cv, cs, dww, dwb, lng, lnb, pw_bf, pwb)


META_E0, META_E1, META_G0, META_G1, META_R0, META_R1 = range(6)


ROUTER_ROWS = 40


def _out_router_body(mixp_ref, xp_ref, mixs_ref, xs_ref, wout_ref, g2_ref, rw_ref, rb_ref, h_ref, meta_ref, cnt_ref,
                     carry, earlier, *, prompt_tiles):
    i = pl.program_id(0)
    tm = xp_ref.shape[0]

    @pl.when(i == 0)
    def _():
        carry[...] = jnp.zeros_like(carry)
        ti = lax.broadcasted_iota(jnp.int32, (tm, tm), 0)
        tj = lax.broadcasted_iota(jnp.int32, (tm, tm), 1)
        earlier[...] = jnp.where(ti < tj, 1.0, 0.0).astype(BF16)

    is_prompt = i < prompt_tiles
    x = jnp.where(is_prompt, xp_ref[...], xs_ref[...])
    mix = jnp.where(is_prompt, mixp_ref[...], mixs_ref[...])
    h = x + _dot(mix, wout_ref[...])
    h_ref[...] = h
    xn = _rmsnorm(h, g2_ref[...]).astype(BF16)
    logits = _dot(xn, rw_ref[...]) + rb_ref[...]
    lt = logits.T[:ROUTER_ROWS, :]
    row = lax.broadcasted_iota(jnp.int32, lt.shape, 0).astype(F32)
    first = lambda cond: jnp.min(jnp.where(cond, row, float(LANES)), axis=0, keepdims=True)

    gl = jnp.where(row < N_GROUPS, lt, -jnp.inf)
    gmax = jnp.max(gl, axis=0, keepdims=True)
    gidx = first(gl == gmax)
    gval = 1.0 / jnp.sum(jnp.exp(gl - gmax), axis=0, keepdims=True)

    lo = ROUTER_LANE0 + EXPERTS_PER_GROUP * gidx
    el = jnp.where((row >= lo) & (row < lo + EXPERTS_PER_GROUP), lt, -jnp.inf)
    m1 = jnp.max(el, axis=0, keepdims=True)
    i1 = first(el == m1)
    el2 = jnp.where(row == i1, -jnp.inf, el)
    m2 = jnp.max(el2, axis=0, keepdims=True)
    i2 = first(el2 == m2)
    r = jnp.exp(m2 - m1)
    g0 = gval * (1.0 / (1.0 + r))
    g1 = gval * (r / (1.0 + r))

    sel0, sel1 = row == i1, row == i2
    onehot = jnp.where(sel0 | sel1, 1.0, 0.0)
    before = _dot(onehot.astype(BF16), earlier[...]) + carry[0:ROUTER_ROWS, 0:1]
    r0 = jnp.sum(jnp.where(sel0, before, 0.0), axis=0, keepdims=True)
    r1 = jnp.sum(jnp.where(sel1, before, 0.0), axis=0, keepdims=True)
    carry[0:ROUTER_ROWS, :] = carry[0:ROUTER_ROWS, :] + jnp.sum(onehot, axis=1, keepdims=True)
    cnt_ref[...] = carry[...]

    row8 = lax.broadcasted_iota(jnp.int32, (SUBLANES, tm), 0)
    rec = jnp.zeros((SUBLANES, tm), F32)
    for slot, val in ((META_E0, i1 - ROUTER_LANE0), (META_E1, i2 - ROUTER_LANE0), (META_G0, g0), (META_G1, g1),
                      (META_R0, r0), (META_R1, r1)):
        rec = jnp.where(row8 == slot, val, rec)
    meta_ref[...] = jnp.concatenate([rec, jnp.zeros((LANES - SUBLANES, tm), F32)], axis=0).T


def _out_router(mix_p, x_p, mix_s, x_s, wout_bf, g2, rw_bf, rb):
    n_p, n_s = x_p.shape[0], x_s.shape[0]
    tm = TM_PROJ
    assert n_p % tm == 0 and n_s % tm == 0
    tp, ts = n_p // tm, n_s // tm
    n = n_p + n_s
    prow = lambda w: pl.BlockSpec((tm, w), lambda i: (jnp.minimum(i, tp - 1), 0))
    srow = lambda w: pl.BlockSpec((tm, w), lambda i: (jnp.maximum(i - tp, 0), 0))
    row = lambda w: pl.BlockSpec((tm, w), lambda i: (i, 0))
    return pl.pallas_call(
        functools.partial(_out_router_body, prompt_tiles=tp),
        grid=(tp + ts,),
        in_specs=[prow(D_MODEL), prow(D_MODEL), srow(D_MODEL), srow(D_MODEL), _resident((D_MODEL, D_MODEL)),
                  _resident((1, D_MODEL)), _resident((D_MODEL, LANES)), _resident((1, LANES))],
        out_specs=[row(D_MODEL), row(LANES), pl.BlockSpec((LANES, LANES), lambda i: (0, 0))],
        out_shape=[jax.ShapeDtypeStruct((n, D_MODEL), F32), jax.ShapeDtypeStruct((n, LANES), F32),
                   jax.ShapeDtypeStruct((LANES, LANES), F32)],
        scratch_shapes=[pltpu.VMEM((LANES, LANES), F32), pltpu.VMEM((tm, tm), BF16)],
        compiler_params=_params(),
        name="out_router",
    )(mix_p, x_p, mix_s, x_s, wout_bf, g2, rw_bf, rb)


def _positions_body(meta_ref, off_ref, pos_ref):
    meta = meta_ref[...]
    lane = lax.broadcasted_iota(jnp.int32, meta.shape, 1)
    lane_f = lane.astype(F32)
    off = off_ref[...]
    pos = jnp.zeros(meta.shape, F32)
    for slot, (e_lane, r_lane) in enumerate(((META_E0, META_R0), (META_E1, META_R1))):
        e = meta[:, e_lane:e_lane + 1] + ROUTER_LANE0
        seg = jnp.sum(jnp.where(lane_f == e, off, 0.0), axis=-1, keepdims=True)
        pos = jnp.where(lane == slot, seg + meta[:, r_lane:r_lane + 1], pos)
    pos_ref[...] = pos.astype(jnp.int32)


def _positions(meta_all, off_lanes):
    n = meta_all.shape[0]
    tm = TM_PROJ
    assert n % tm == 0
    return pl.pallas_call(
        _positions_body,
        grid=(n // tm,),
        in_specs=[pl.BlockSpec((tm, LANES), lambda i: (i, 0)), _resident((1, LANES))],
        out_specs=pl.BlockSpec((tm, LANES), lambda i: (i, 0)),
        out_shape=jax.ShapeDtypeStruct((n, LANES), jnp.int32),
        compiler_params=_params(),
        name="positions",
    )(meta_all, off_lanes)


N_LOAD_SLOTS = 3


def _pow2_sizes(n):
    return [1 << b for b in range(n.bit_length() - 1, -1, -1)]


def _dispatch_body(pos_ref, nv_ref, h_ref, hs_ref, hbuf, zbuf, lsem, ssem, zsem, *, n_tok, tm):
    i = pl.program_id(0)
    n_steps = pl.num_programs(0)
    n_tiles = nv_ref.shape[0]

    groups = tm // SUBLANES

    def load(t, slot, sem=lsem):
        return pltpu.make_async_copy(h_ref.at[pl.ds(t * groups, groups)], hbuf.at[slot], sem.at[slot])

    def clear_unowned(start):
        def per_tile(t, c):
            nv = nv_ref[t]
            z = TR - nv
            row = t * TR + nv
            head = z & (SUBLANES - 1)
            for k in range(SUBLANES - 1):
                @pl.when(k < head)
                def _(k=k):
                    cp = pltpu.make_async_copy(zbuf.at[pl.ds(0, 1), :], hs_ref.at[pl.ds(row + k, 1), :], zsem)
                    cp.start() if start else cp.wait()
            row = row + head
            for size in _pow2_sizes(TR):
                if size < SUBLANES:
                    break
                @pl.when((z & size) != 0)
                def _(row=row, size=size):
                    dst = hs_ref.at[pl.ds(pl.multiple_of(row, SUBLANES), size), :]
                    cp = pltpu.make_async_copy(zbuf.at[pl.ds(0, size), :], dst, zsem)
                    cp.start() if start else cp.wait()
                row = row + (z & size)
            return c
        lax.fori_loop(0, n_tiles, per_tile, 0)

    def scatter_wait(slot):
        for _ in range(2):
            load(0, slot, ssem).wait()

    @pl.when(i == 0)
    def _():
        zbuf[...] = jnp.zeros_like(zbuf)
        clear_unowned(True)
        load(0, 0).start()

    @pl.when(i + 1 < n_steps)
    def _():
        load(i + 1, (i + 1) % N_LOAD_SLOTS).start()

    slot = i % N_LOAD_SLOTS
    load(i, slot).wait()

    def issue(jb, c):
        for u in range(SUBLANES):
            for s in range(2):
                p = pos_ref[s * n_tok + i * tm + jb * SUBLANES + u]
                pltpu.make_async_copy(hbuf.at[slot, jb, pl.ds(u, 1), :], hs_ref.at[pl.ds(p, 1), :],
                                      ssem.at[slot]).start()
        return c
    lax.fori_loop(0, groups, issue, 0)

    @pl.when(i > 0)
    def _():
        scatter_wait((i + N_LOAD_SLOTS - 1) % N_LOAD_SLOTS)

    @pl.when(i == n_steps - 1)
    def _():
        scatter_wait(slot)
        clear_unowned(False)


def _dispatch(pos, nv, rows):
    n_tok, width = rows.shape
    n_tiles = nv.shape[0]
    tm = TM_PROJ
    assert n_tok % tm == 0
    return pl.pallas_call(
        functools.partial(_dispatch_body, n_tok=n_tok, tm=tm),
        grid_spec=pltpu.PrefetchScalarGridSpec(
            num_scalar_prefetch=2,
            grid=(n_tok // tm,),
            in_specs=[pl.BlockSpec(memory_space=pl.ANY)],
            out_specs=pl.BlockSpec(memory_space=pl.ANY),
            scratch_shapes=[pltpu.VMEM((N_LOAD_SLOTS, tm // SUBLANES, SUBLANES, width), rows.dtype),
                            pltpu.VMEM((TR, width), rows.dtype),
                            pltpu.SemaphoreType.DMA((N_LOAD_SLOTS,)), pltpu.SemaphoreType.DMA((N_LOAD_SLOTS,)),
                            pltpu.SemaphoreType.DMA(())],
        ),
        out_shape=jax.ShapeDtypeStruct((n_tiles * TR, width), rows.dtype),
        compiler_params=_params(),
        name="dispatch",
    )(pos, nv, rows.reshape(n_tok // SUBLANES, SUBLANES, width))


N_MOE_LOADS = 4


def _moe_body(ts_ref, hs_ref, g2_ref, wg_ref, wu_ref, wd_ref, ys_ref, xbuf, obuf, wg_bf, wu_bf, wd_bf, lsem, ssem):
    e = pl.program_id(0)
    n_tiles = ys_ref.shape[0] // TR
    n_used = ts_ref[N_EXPERTS]
    t0, t1 = ts_ref[e], ts_ref[e + 1]

    def load(g, slot):
        return pltpu.make_async_copy(hs_ref.at[pl.ds(pl.multiple_of(g * TR, TR), TR), :], xbuf.at[slot], lsem.at[slot])

    def store(g, slot):
        return pltpu.make_async_copy(obuf.at[slot], ys_ref.at[pl.ds(pl.multiple_of(g * TR, TR), TR), :], ssem.at[slot])

    @pl.when(e == 0)
    def _():
        for g in range(N_MOE_LOADS - 1):
            @pl.when(g < n_used)
            def _(g=g):
                load(g, g).start()

    @pl.when(t1 > t0)
    def _():
        wg_bf[...] = wg_ref[0].astype(BF16)
        wu_bf[...] = wu_ref[0].astype(BF16)
        wd_bf[...] = wd_ref[0].astype(BF16)

    def tile(g, c):
        slot = g % 2
        xslot = g % N_MOE_LOADS
        ahead = g + N_MOE_LOADS - 1

        @pl.when(ahead < n_used)
        def _():
            load(ahead, ahead % N_MOE_LOADS).start()

        load(g, xslot).wait()
        xn = _rmsnorm(xbuf[xslot], g2_ref[...]).astype(BF16)
        hg = _dot(xn, wg_bf[...])
        hu = _dot(xn, wu_bf[...])
        hid = (hg * jax.nn.sigmoid(hg) * hu).astype(BF16)
        y = _dot(hid, wd_bf[...])

        @pl.when(g >= 2)
        def _():
            store(g - 2, slot).wait()

        obuf[slot] = y
        store(g, slot).start()
        return c

    lax.fori_loop(t0, t1, tile, 0)

    @pl.when(e == pl.num_programs(0) - 1)
    def _():
        @pl.when(n_used >= 2)
        def _():
            store(n_used - 2, n_used % 2).wait()
        store(n_used - 1, (n_used - 1) % 2).wait()
        obuf[0] = jnp.zeros(obuf.shape[1:], obuf.dtype)

        def clear(g, c):
            store(g, 0).start()
            return c
        lax.fori_loop(n_used, n_tiles, clear, 0)

        def drain(g, c):
            store(g, 0).wait()
            return c
        lax.fori_loop(n_used, n_tiles, drain, 0)


def _moe(ts, hs, g2, wg, wu, wd):
    wspec = lambda shape: pl.BlockSpec((1,) + shape, lambda e, ts: (e, 0, 0))
    tile_buf = lambda slots: pltpu.VMEM((slots, TR, hs.shape[1]), hs.dtype)
    return pl.pallas_call(
        _moe_body,
        grid_spec=pltpu.PrefetchScalarGridSpec(
            num_scalar_prefetch=1,
            grid=(N_EXPERTS,),
            in_specs=[pl.BlockSpec(memory_space=pl.ANY),
                      pl.BlockSpec((1, D_MODEL), lambda e, ts: (0, 0)),
                      wspec((D_MODEL, D_EXPERT)), wspec((D_MODEL, D_EXPERT)), wspec((D_EXPERT, D_MODEL))],
            out_specs=pl.BlockSpec(memory_space=pl.ANY),
            scratch_shapes=[tile_buf(N_MOE_LOADS), tile_buf(2),
                            pltpu.VMEM((D_MODEL, D_EXPERT), BF16), pltpu.VMEM((D_MODEL, D_EXPERT), BF16),
                            pltpu.VMEM((D_EXPERT, D_MODEL), BF16),
                            pltpu.SemaphoreType.DMA((N_MOE_LOADS,)), pltpu.SemaphoreType.DMA((2,))],
        ),
        out_shape=jax.ShapeDtypeStruct(hs.shape, hs.dtype),
        compiler_params=_params(),
        name="moe",
    )(ts, hs, g2, wg, wu, wd)


def _combine_body(pos_ref, h_ref, meta_ref, g_ref, ys_ref, ys_grouped_ref, o_ref, ybuf, sem, *, n_tok, row0):
    i = pl.program_id(0)
    n_steps = pl.num_programs(0)
    tm = h_ref.shape[0]

    groups = tm // SUBLANES

    def gather(t, slot):
        def issue(jb, c):
            for u in range(SUBLANES):
                for s in range(2):
                    p = pos_ref[s * n_tok + row0 + t * tm + jb * SUBLANES + u]
                    pltpu.make_async_copy(ys_ref.at[pl.ds(p, 1), :], ybuf.at[slot, s, jb, pl.ds(u, 1), :],
                                          sem.at[slot]).start()
            return c
        lax.fori_loop(0, groups, issue, 0)

    @pl.when(i == 0)
    def _():
        gather(0, 0)

    @pl.when(i + 1 < n_steps)
    def _():
        gather(i + 1, (i + 1) % 2)

    slot = i % 2
    for s in range(2):
        pltpu.make_async_copy(ys_grouped_ref.at[pl.ds(0, groups)], ybuf.at[slot, s], sem.at[slot]).wait()
    meta = meta_ref[...]
    g0 = meta[:, META_G0:META_G0 + 1]
    g1 = meta[:, META_G1:META_G1 + 1]
    y0 = ybuf[slot, 0].reshape(tm, D_MODEL)
    y1 = ybuf[slot, 1].reshape(tm, D_MODEL)
    h = h_ref[...] + (g0 * y0 + g1 * y1)
    o_ref[...] = _rmsnorm(h, g_ref[...])


def _combine(pos, h_all, meta_all, ys, gf, row0, n):
    n_tok = h_all.shape[0]
    tm = min(TM_OUT, n)
    assert row0 % tm == 0 and n % tm == 0
    b0 = row0 // tm
    return pl.pallas_call(
        functools.partial(_combine_body, n_tok=n_tok, row0=row0),
        grid_spec=pltpu.PrefetchScalarGridSpec(
            num_scalar_prefetch=1,
            grid=(n // tm,),
            in_specs=[pl.BlockSpec((tm, D_MODEL), lambda i, pos: (b0 + i, 0)),
                      pl.BlockSpec((tm, LANES), lambda i, pos: (b0 + i, 0)),
                      pl.BlockSpec((1, D_MODEL), lambda i, pos: (0, 0)),
                      pl.BlockSpec(memory_space=pl.ANY), pl.BlockSpec(memory_space=pl.ANY)],
            out_specs=pl.BlockSpec((tm, D_MODEL), lambda i, pos: (i, 0)),
            scratch_shapes=[pltpu.VMEM((2, 2, tm // SUBLANES, SUBLANES, ys.shape[1]), ys.dtype),
                            pltpu.SemaphoreType.DMA((2,))],
        ),
        out_shape=jax.ShapeDtypeStruct((n, D_MODEL), F32),
        compiler_params=_params(),
        name="combine",
    )(pos, h_all, meta_all, gf, ys, ys.reshape(ys.shape[0] // SUBLANES, SUBLANES, ys.shape[1]))


def _routing_tables(counts, n_tok):
    n_tiles = (2 * n_tok) // TR + N_EXPERTS
    cnt = counts[ROUTER_LANE0:ROUTER_LANE0 + N_EXPERTS].astype(jnp.int32)
    tiles_e = (cnt + TR - 1) // TR
    tile_end = jnp.cumsum(tiles_e)
    tile_start = tile_end - tiles_e
    off_lanes = jnp.pad((tile_start * TR).astype(F32), (ROUTER_LANE0, LANES - ROUTER_LANE0 - N_EXPERTS))
    ts = jnp.concatenate([tile_start, tile_end[-1:]]).astype(jnp.int32)
    tile = jnp.arange(n_tiles, dtype=jnp.int32)
    owner = (tile[:, None] >= tile_start[None, :]) & (tile[:, None] < tile_end[None, :])
    rows_left = jnp.sum(jnp.where(owner, cnt[None, :] - (tile[:, None] - tile_start[None, :]) * TR, 0), axis=1)
    nv = jnp.clip(rows_left, 0, TR).astype(jnp.int32)
    return off_lanes.reshape(1, LANES), ts, nv


def kernel(x_prompt, x_sample, cache_win_k, cache_win_v, state_conv, norm1_g, w_in, attn_sink, conv_dw_w, conv_dw_b,
           conv_ln_g, conv_ln_b, conv_pw_w, conv_pw_b, w_out, norm2_g, router_group_w, router_group_b,
           router_expert_w, router_expert_b, expert_w_gate, expert_w_up, expert_w_down, final_norm_g):
    depth = w_in.shape[0]
    assert depth == 1, "single-layer step"
    bp, sp, _ = x_prompt.shape
    assert bp == 1, "one prompt sequence"
    n_seq, t_new, _ = x_sample.shape
    n_p, n_s = bp * sp, n_seq * t_new
    n_tok = n_p + n_s
    l = 0

    row = lambda a: a.reshape(1, -1)
    w_in_bf = w_in[l].astype(BF16)
    w_out_bf = w_out[l].astype(BF16)
    pw_bf = conv_pw_w[l].astype(BF16)
    rw = jnp.concatenate([router_group_w[l], router_expert_w[l]], axis=1)
    rw_bf = jnp.pad(rw, ((0, 0), (0, LANES - rw.shape[1]))).astype(BF16)
    rb = jnp.pad(jnp.concatenate([router_group_b[l], router_expert_b[l]]), (0, LANES - rw.shape[1])).reshape(1, LANES)
    g1, g2, gf = row(norm1_g[l]), row(norm2_g[l]), row(final_norm_g)
    sink = attn_sink[l]
    conv_w = (conv_dw_w[l], row(conv_dw_b[l]), row(conv_ln_g[l]), row(conv_ln_b[l]), pw_bf, row(conv_pw_b[l]))

    xp = x_prompt.reshape(n_p, D_MODEL)
    xs = x_sample.reshape(n_s, D_MODEL)
    cache_axes, cache_axes_back = (0, 1, 3, 4, 2), (0, 1, 4, 2, 3)
    ck = jnp.transpose(cache_win_k, cache_axes)
    cv = jnp.transpose(cache_win_v, cache_axes)
    cs = jnp.transpose(state_conv, (0, 2, 1, 3))

    qp, kp, vp, up = _in_proj(xp, g1, w_in_bf)
    qs, ks, vs, us = _in_proj(xs, g1, w_in_bf)
    mix_p = _mixer_prompt(sink, qp, kp, vp, up, *conv_w)
    mix_s, wk_s, wv_s, wc_s = _mixer_sample(sink, qs, ks, vs, us, ck, cv, cs, *conv_w)

    h_all, meta_all, cnt = _out_router(mix_p, xp, mix_s, xs, w_out_bf, g2, rw_bf, rb)

    off_lanes, ts, nv = _routing_tables(cnt[:, 0], n_tok)
    pos_lanes = _positions(meta_all, off_lanes)
    pos = jnp.concatenate([pos_lanes[:, 0], pos_lanes[:, 1]])
    hs = _dispatch(pos, nv, h_all)
    ys = _moe(ts, hs, g2, expert_w_gate[l], expert_w_up[l], expert_w_down[l])
    y_p = _combine(pos, h_all, meta_all, ys, gf, 0, n_p)
    y_s = _combine(pos, h_all, meta_all, ys, gf, n_p, n_s)

    kv_shape = (depth, bp, WINDOW, N_KV_HEADS, HEAD_DIM)
    return (y_p.reshape(bp, sp, D_MODEL), y_s.reshape(n_seq, t_new, D_MODEL),
            kp[n_p - WINDOW:].reshape(kv_shape), vp[n_p - WINDOW:].reshape(kv_shape),
            up[n_p - (CONV_W - 1):].reshape(depth, bp, CONV_W - 1, D_CONV),
            jnp.transpose(wk_s, cache_axes_back), jnp.transpose(wv_s, cache_axes_back),
            jnp.transpose(wc_s, (0, 2, 1, 3)))
```

```python
import functools

import jax
import jax.numpy as jnp
from jax import lax
from jax.experimental import pallas as pl
from jax.experimental.pallas import tpu as pltpu

F32 = jnp.float32
BF16 = jnp.bfloat16

D_MODEL = 2048
HEAD_DIM = 64
N_HEADS = 16
N_KV_HEADS = 2
GQA_GROUP = 8
KV_DIM = N_KV_HEADS * HEAD_DIM
D_ATTN = N_HEADS * HEAD_DIM
D_CONV = D_MODEL - D_ATTN
WINDOW = 128
CONV_W = 31
D_IN = D_ATTN + 2 * KV_DIM + 2 * D_CONV
N_GROUPS = 4
EXPERTS_PER_GROUP = 8
N_EXPERTS = N_GROUPS * EXPERTS_PER_GROUP
D_EXPERT = 256
RMS_EPS = 1e-6
LN_EPS = 1e-5

LANES = 128
SUBLANES = 8
ROUTER_LANE0 = N_GROUPS
VMEM_LIMIT = 56 * 1024 * 1024

TM_PROJ = 512
TM_MIX = 256
SEQ_TILE = 8
TR = 128
TM_OUT = 256

_NT = (((1,), (1,)), ((), ()))


def _params(n_axes=1):
    return pltpu.CompilerParams(dimension_semantics=("arbitrary",) * n_axes, vmem_limit_bytes=VMEM_LIMIT)


def _resident(shape):
    return pl.BlockSpec(shape, lambda *_: (0,) * len(shape), pipeline_mode=pl.Buffered(1))


def _rmsnorm(x, g):
    ms = jnp.mean(x * x, axis=-1, keepdims=True)
    return x * lax.rsqrt(ms + RMS_EPS) * g


def _dot(a, b):
    return jnp.dot(a, b, preferred_element_type=F32)


def _dot_nt(a, b):
    return lax.dot_general(a, b, _NT, preferred_element_type=F32)


def _in_proj_body(x_ref, g_ref, w_ref, q_ref, k_ref, v_ref, u_ref):
    xn = _rmsnorm(x_ref[...], g_ref[...]).astype(BF16)
    q_ref[...] = (_dot(xn, w_ref[:, :D_ATTN]) * (HEAD_DIM ** -0.5)).astype(BF16)
    kv = _dot(xn, w_ref[:, D_ATTN:D_ATTN + 2 * KV_DIM])
    k_ref[...] = kv[:, :KV_DIM]
    v_ref[...] = kv[:, KV_DIM:]
    c0 = D_ATTN + 2 * KV_DIM
    cw = 256
    for j in range(D_CONV // cw):
        a = _dot(xn, w_ref[:, c0 + j * cw:c0 + (j + 1) * cw])
        b = _dot(xn, w_ref[:, c0 + D_CONV + j * cw:c0 + D_CONV + (j + 1) * cw])
        u_ref[:, j * cw:(j + 1) * cw] = a * jax.nn.sigmoid(b)


def _in_proj(x, g, w_bf):
    n = x.shape[0]
    tm = min(TM_PROJ, n)
    row = lambda w: pl.BlockSpec((tm, w), lambda i: (i, 0))
    return pl.pallas_call(
        _in_proj_body,
        grid=(n // tm,),
        in_specs=[row(D_MODEL), _resident((1, D_MODEL)), _resident((D_MODEL, D_IN))],
        out_specs=[row(D_ATTN), row(KV_DIM), row(KV_DIM), row(D_CONV)],
        out_shape=[jax.ShapeDtypeStruct((n, D_ATTN), BF16), jax.ShapeDtypeStruct((n, KV_DIM), F32),
                   jax.ShapeDtypeStruct((n, KV_DIM), F32), jax.ShapeDtypeStruct((n, D_CONV), F32)],
        compiler_params=_params(),
        name="in_proj",
    )(x, g, w_bf)


def _attend(q, parts, sink_ref, kv_head, keys_on_lanes=False):
    t = q.shape[0]
    score, weigh = (_dot, _dot_nt) if keys_on_lanes else (_dot_nt, _dot)
    qs = jnp.concatenate([q[:, g * HEAD_DIM:(g + 1) * HEAD_DIM] for g in range(GQA_GROUP)], axis=0)
    scores = [score(qs, kk) for kk, _, _ in parts]
    hidden = [jnp.where(mask, 0.0, -jnp.inf) for _, _, mask in parts]
    ps = [[] for _ in parts]
    inv = []
    for g in range(GQA_GROUP):
        sg = [s[g * t:(g + 1) * t] + neg for s, neg in zip(scores, hidden)]
        sk = sink_ref[kv_head * GQA_GROUP + g]
        m = sk
        for x in sg:
            m = jnp.maximum(jnp.max(x, axis=-1, keepdims=True), m)
        den = jnp.exp(sk - m)
        for k, x in enumerate(sg):
            p = jnp.exp(x - m)
            den = den + jnp.sum(p, axis=-1, keepdims=True)
            ps[k].append(p.astype(BF16))
        inv.append(1.0 / den)
    o = None
    for k, (_, vv, _) in enumerate(parts):
        ok = weigh(jnp.concatenate(ps[k], axis=0), vv)
        o = ok if o is None else o + ok
    return jnp.concatenate([o[g * t:(g + 1) * t] * inv[g] for g in range(GQA_GROUP)], axis=1)


def _conv_tail(y, dwb_ref, lng_ref, lnb_ref, pw_ref, pwb_ref):
    y = y + dwb_ref[...]
    mu = jnp.mean(y, axis=-1, keepdims=True)
    yc = y - mu
    yn = yc * lax.rsqrt(jnp.mean(yc * yc, axis=-1, keepdims=True) + LN_EPS)
    yn = yn * lng_ref[...] + lnb_ref[...]
    act = yn * jax.nn.sigmoid(yn)
    return _dot(act.astype(BF16), pw_ref[...]) + pwb_ref[...]


CONV_PAD = 32
CONV_ROWS = 128
CONV_COLS = 128


def _mixer_prompt_body(sink_ref, q_ref, k_ref, v_ref, u_ref, dww_ref, dwb_ref, lng_ref, lnb_ref, pw_ref, pwb_ref,
                       mix_ref, kprev, vprev, uext, ushift, ybuf):
    i = pl.program_id(0)
    tm = q_ref.shape[0]

    @pl.when(i == 0)
    def _():
        kprev[...] = jnp.zeros_like(kprev)
        vprev[...] = jnp.zeros_like(vprev)
        uext[0:CONV_PAD, :] = jnp.zeros((CONV_PAD, D_CONV), F32)

    qi = lax.broadcasted_iota(jnp.int32, (WINDOW, 2 * WINDOW), 0)
    kj = lax.broadcasted_iota(jnp.int32, (WINDOW, 2 * WINDOW), 1)
    diff = qi + WINDOW - kj
    band = (diff >= 0) & (diff <= WINDOW)
    kp, vp = kprev[...], vprev[...]
    for b in range(tm // WINDOW):
        rows = slice(b * WINDOW, (b + 1) * WINDOW)
        kb = k_ref[rows, :].astype(BF16)
        vb = v_ref[rows, :].astype(BF16)
        kk = jnp.concatenate([kp, kb], axis=0)
        vv = jnp.concatenate([vp, vb], axis=0)
        if b == 0:
            mask = band & (kj >= jnp.where(i > 0, 0, WINDOW))
        else:
            mask = band
        for h in range(N_KV_HEADS):
            cols = slice(h * HEAD_DIM, (h + 1) * HEAD_DIM)
            hq = slice(h * GQA_GROUP * HEAD_DIM, (h + 1) * GQA_GROUP * HEAD_DIM)
            o = _attend(q_ref[rows, hq], [(kk[:, cols], vv[:, cols], mask)], sink_ref, h)
            mix_ref[rows, hq] = o.astype(BF16)
        kp, vp = kb, vb
    kprev[...] = kp
    vprev[...] = vp

    uext[CONV_PAD:CONV_PAD + tm, :] = u_ref[...]
    n_shift_rows = ushift.shape[1]
    for s in range(1, SUBLANES):
        ushift[s - 1] = uext[s:s + n_shift_rows, :]
    off = CONV_PAD - (CONV_W - 1)
    for r in range(tm // CONV_ROWS):
        for c in range(D_CONV // CONV_COLS):
            cs = slice(c * CONV_COLS, (c + 1) * CONV_COLS)
            acc = jnp.zeros((CONV_ROWS, CONV_COLS), F32)
            for j in range(CONV_W):
                a, s = divmod(off + j, SUBLANES)
                r0 = r * CONV_ROWS + a * SUBLANES
                src = uext[r0:r0 + CONV_ROWS, cs] if s == 0 else ushift[s - 1, r0:r0 + CONV_ROWS, cs]
                acc = acc + dww_ref[j:j + 1, cs] * src
            ybuf[r * CONV_ROWS:(r + 1) * CONV_ROWS, cs] = acc
    uext[0:CONV_PAD, :] = uext[tm:tm + CONV_PAD, :]
    conv = _conv_tail(ybuf[...], dwb_ref, lng_ref, lnb_ref, pw_ref, pwb_ref)
    mix_ref[:, D_ATTN:] = conv.astype(BF16)


def _mixer_prompt(sink, q, k, v, u, dww, dwb, lng, lnb, pw_bf, pwb):
    n = q.shape[0]
    tm = TM_MIX
    row = lambda w: pl.BlockSpec((tm, w), lambda i, s: (i, 0))
    res = lambda shape: pl.BlockSpec(shape, lambda i, s: (0,) * len(shape), pipeline_mode=pl.Buffered(1))
    return pl.pallas_call(
        _mixer_prompt_body,
        grid_spec=pltpu.PrefetchScalarGridSpec(
            num_scalar_prefetch=1,
            grid=(n // tm,),
            in_specs=[row(D_ATTN), row(KV_DIM), row(KV_DIM), row(D_CONV), res((CONV_W, D_CONV)), res((1, D_CONV)),
                      res((1, D_CONV)), res((1, D_CONV)), res((D_CONV, D_CONV)), res((1, D_CONV))],
            out_specs=row(D_MODEL),
            scratch_shapes=[pltpu.VMEM((WINDOW, KV_DIM), BF16), pltpu.VMEM((WINDOW, KV_DIM), BF16),
                            pltpu.VMEM((CONV_PAD + tm, D_CONV), F32),
                            pltpu.VMEM((SUBLANES - 1, CONV_PAD - SUBLANES + tm, D_CONV), F32),
                            pltpu.VMEM((tm, D_CONV), F32)],
        ),
        out_shape=jax.ShapeDtypeStruct((n, D_MODEL), BF16),
        compiler_params=_params(),
        name="mixer_prompt",
    )(sink, q, k, v, u, dww, dwb, lng, lnb, pw_bf, pwb)


def _mixer_sample_body(sink_ref, q_ref, k_ref, v_ref, u_ref, ck_ref, cv_ref, cs_ref, dww_ref, dwb_ref, lng_ref,
                       lnb_ref, pw_ref, pwb_ref, mix_ref, wk_ref, wv_ref, wc_ref, usel, ybuf):
    nt = q_ref.shape[0]
    sb = ck_ref.shape[0]
    t_new = nt // sb
    n_cache = sb * WINDOW
    n_state = CONV_W - 1

    pad = jnp.zeros((LANES - nt, KV_DIM), F32)
    kn_t = jnp.concatenate([k_ref[...], pad], axis=0).T
    vn_t = jnp.concatenate([v_ref[...], pad], axis=0).T

    lt, lw = t_new.bit_length() - 1, WINDOW.bit_length() - 1
    qr = lax.broadcasted_iota(jnp.int32, (nt, n_cache), 0)
    cc = lax.broadcasted_iota(jnp.int32, (nt, n_cache), 1)
    mask_c = ((cc >> lw) == (qr >> lt)) & ((cc & (WINDOW - 1)) >= (qr & (t_new - 1)))
    qr = lax.broadcasted_iota(jnp.int32, (nt, LANES), 0)
    cn = lax.broadcasted_iota(jnp.int32, (nt, LANES), 1)
    mask_n = ((cn >> lt) == (qr >> lt)) & ((cn & (t_new - 1)) <= (qr & (t_new - 1)))
    for h in range(N_KV_HEADS):
        dims = slice(h * HEAD_DIM, (h + 1) * HEAD_DIM)
        hq = slice(h * GQA_GROUP * HEAD_DIM, (h + 1) * GQA_GROUP * HEAD_DIM)
        kc = jnp.concatenate([ck_ref[s, h] for s in range(sb)], axis=1).astype(BF16)
        vc = jnp.concatenate([cv_ref[s, h] for s in range(sb)], axis=1).astype(BF16)
        parts = [(kc, vc, mask_c), (kn_t[dims, :].astype(BF16), vn_t[dims, :].astype(BF16), mask_n)]
        o = _attend(q_ref[:, hq], parts, sink_ref, h, keys_on_lanes=True)
        mix_ref[:, hq] = o.astype(BF16)

    is_new = lax.broadcasted_iota(jnp.int32, (HEAD_DIM, WINDOW), 1) >= WINDOW - t_new
    for s in range(sb):
        shift = (WINDOW - t_new - s * t_new) % LANES
        kn_s = pltpu.roll(kn_t, shift=shift, axis=1)
        vn_s = pltpu.roll(vn_t, shift=shift, axis=1)
        for h in range(N_KV_HEADS):
            dims = slice(h * HEAD_DIM, (h + 1) * HEAD_DIM)
            old_k = pltpu.roll(ck_ref[s, h], shift=WINDOW - t_new, axis=1)
            old_v = pltpu.roll(cv_ref[s, h], shift=WINDOW - t_new, axis=1)
            wk_ref[s, h] = jnp.where(is_new, kn_s[dims, :], old_k)
            wv_ref[s, h] = jnp.where(is_new, vn_s[dims, :], old_v)

    for s in range(sb):
        for t in range(t_new):
            usel[t, s:s + 1, :] = u_ref[s * t_new + t:s * t_new + t + 1, :]
    u_new = [usel[t] for t in range(t_new)]
    for t in range(t_new):
        acc = jnp.zeros((sb, D_CONV), F32)
        for r in range(t, n_state):
            acc = acc + dww_ref[r - t:r - t + 1, :] * cs_ref[r]
        for t2 in range(t + 1):
            j = n_state - t + t2
            acc = acc + dww_ref[j:j + 1, :] * u_new[t2]
        for s in range(sb):
            ybuf[s * t_new + t:s * t_new + t + 1, :] = acc[s:s + 1, :]
        wc_ref[n_state - t_new + t] = u_new[t]
    wc_ref[0:n_state - t_new] = cs_ref[t_new:n_state]
    conv = _conv_tail(ybuf[...], dwb_ref, lng_ref, lnb_ref, pw_ref, pwb_ref)
    mix_ref[:, D_ATTN:] = conv.astype(BF16)


def _mixer_sample(sink, q, k, v, u, ck, cv, cs, dww, dwb, lng, lnb, pw_bf, pwb):
    n = q.shape[0]
    n_seq = ck.shape[1]
    t_new = n // n_seq
    sb = SEQ_TILE
    nt = sb * t_new
    n_state = CONV_W - 1
    row = lambda w: pl.BlockSpec((nt, w), lambda i, s: (i, 0))
    cache = pl.BlockSpec((None, sb, N_KV_HEADS, HEAD_DIM, WINDOW), lambda i, s: (0, i, 0, 0, 0))
    state = pl.BlockSpec((None, n_state, sb, D_CONV), lambda i, s: (0, 0, i, 0))
    res = lambda shape: pl.BlockSpec(shape, lambda i, s: (0,) * len(shape), pipeline_mode=pl.Buffered(1))
    return pl.pallas_call(
        _mixer_sample_body,
        grid_spec=pltpu.PrefetchScalarGridSpec(
            num_scalar_prefetch=1,
            grid=(n_seq // sb,),
            in_specs=[row(D_ATTN), row(KV_DIM), row(KV_DIM), row(D_CONV), cache, cache, state,
                      res((CONV_W, D_CONV)), res((1, D_CONV)), res((1, D_CONV)), res((1, D_CONV)),
                      res((D_CONV, D_CONV)), res((1, D_CONV))],
            out_specs=[row(D_MODEL), cache, cache, state],
            scratch_shapes=[pltpu.VMEM((t_new, sb, D_CONV), F32), pltpu.VMEM((nt, D_CONV), F32)],
        ),
        out_shape=[jax.ShapeDtypeStruct((n, D_MODEL), BF16), jax.ShapeDtypeStruct(ck.shape, F32),
                   jax.ShapeDtypeStruct(cv.shape, F32), jax.ShapeDtypeStruct(cs.shape, F32)],
        compiler_params=_params(),
        name="mixer_sample",
    )(sink, q, k, v, u, ck, cv, cs, dww, dwb, lng, lnb, pw_bf, pwb)


META_E0, META_E1, META_G0, META_G1, META_R0, META_R1 = range(6)


ROUTER_ROWS = 40


def _out_router_body(mixp_ref, xp_ref, mixs_ref, xs_ref, wout_ref, g2_ref, rw_ref, rb_ref, h_ref, meta_ref, rec_ref,
                     cnt_ref, carry, earlier, *, prompt_tiles):
    i = pl.program_id(0)
    tm = xp_ref.shape[0]

    @pl.when(i == 0)
    def _():
        carry[...] = jnp.zeros_like(carry)
        ti = lax.broadcasted_iota(jnp.int32, (tm, tm), 0)
        tj = lax.broadcasted_iota(jnp.int32, (tm, tm), 1)
        earlier[...] = jnp.where(ti < tj, 1.0, 0.0).astype(BF16)

    is_prompt = i < prompt_tiles
    x = jnp.where(is_prompt, xp_ref[...], xs_ref[...])
    mix = jnp.where(is_prompt, mixp_ref[...], mixs_ref[...])
    h = x + _dot(mix, wout_ref[...])
    h_ref[...] = h
    xn = _rmsnorm(h, g2_ref[...]).astype(BF16)
    logits = _dot(xn, rw_ref[...]) + rb_ref[...]
    lt = logits.T[:ROUTER_ROWS, :]
    row = lax.broadcasted_iota(jnp.int32, lt.shape, 0).astype(F32)
    first = lambda cond: jnp.min(jnp.where(cond, row, float(LANES)), axis=0, keepdims=True)

    gl = jnp.where(row < N_GROUPS, lt, -jnp.inf)
    gmax = jnp.max(gl, axis=0, keepdims=True)
    gidx = first(gl == gmax)
    gval = 1.0 / jnp.sum(jnp.exp(gl - gmax), axis=0, keepdims=True)

    lo = ROUTER_LANE0 + EXPERTS_PER_GROUP * gidx
    el = jnp.where((row >= lo) & (row < lo + EXPERTS_PER_GROUP), lt, -jnp.inf)
    m1 = jnp.max(el, axis=0, keepdims=True)
    i1 = first(el == m1)
    el2 = jnp.where(row == i1, -jnp.inf, el)
    m2 = jnp.max(el2, axis=0, keepdims=True)
    i2 = first(el2 == m2)
    r = jnp.exp(m2 - m1)
    g0 = gval * (1.0 / (1.0 + r))
    g1 = gval * (r / (1.0 + r))

    sel0, sel1 = row == i1, row == i2
    onehot = jnp.where(sel0 | sel1, 1.0, 0.0)
    before = _dot(onehot.astype(BF16), earlier[...]) + carry[0:ROUTER_ROWS, 0:1]
    r0 = jnp.sum(jnp.where(sel0, before, 0.0), axis=0, keepdims=True)
    r1 = jnp.sum(jnp.where(sel1, before, 0.0), axis=0, keepdims=True)
    carry[0:ROUTER_ROWS, :] = carry[0:ROUTER_ROWS, :] + jnp.sum(onehot, axis=1, keepdims=True)
    cnt_ref[...] = carry[...]

    row8 = lax.broadcasted_iota(jnp.int32, (SUBLANES, tm), 0)
    rec = jnp.zeros((SUBLANES, tm), F32)
    for slot, val in ((META_E0, i1 - ROUTER_LANE0), (META_E1, i2 - ROUTER_LANE0), (META_G0, g0), (META_G1, g1),
                      (META_R0, r0), (META_R1, r1)):
        rec = jnp.where(row8 == slot, val, rec)
    rec_ref[...] = rec
    meta_ref[...] = jnp.concatenate([rec, jnp.zeros((LANES - SUBLANES, tm), F32)], axis=0).T


def _out_router(mix_p, x_p, mix_s, x_s, wout_bf, g2, rw_bf, rb):
    n_p, n_s = x_p.shape[0], x_s.shape[0]
    tm = TM_PROJ
    assert n_p % tm == 0 and n_s % tm == 0
    tp, ts = n_p // tm, n_s // tm
    n = n_p + n_s
    prow = lambda w: pl.BlockSpec((tm, w), lambda i: (jnp.minimum(i, tp - 1), 0))
    srow = lambda w: pl.BlockSpec((tm, w), lambda i: (jnp.maximum(i - tp, 0), 0))
    row = lambda w: pl.BlockSpec((tm, w), lambda i: (i, 0))
    return pl.pallas_call(
        functools.partial(_out_router_body, prompt_tiles=tp),
        grid=(tp + ts,),
        in_specs=[prow(D_MODEL), prow(D_MODEL), srow(D_MODEL), srow(D_MODEL), _resident((D_MODEL, D_MODEL)),
                  _resident((1, D_MODEL)), _resident((D_MODEL, LANES)), _resident((1, LANES))],
        out_specs=[row(D_MODEL), row(LANES), pl.BlockSpec((SUBLANES, tm), lambda i: (0, i)),
                   pl.BlockSpec((LANES, LANES), lambda i: (0, 0))],
        out_shape=[jax.ShapeDtypeStruct((n, D_MODEL), F32), jax.ShapeDtypeStruct((n, LANES), F32),
                   jax.ShapeDtypeStruct((SUBLANES, n), F32), jax.ShapeDtypeStruct((LANES, LANES), F32)],
        scratch_shapes=[pltpu.VMEM((LANES, LANES), F32), pltpu.VMEM((tm, tm), BF16)],
        compiler_params=_params(),
        name="out_router",
    )(mix_p, x_p, mix_s, x_s, wout_bf, g2, rw_bf, rb)


def _positions_body(rec_ref, off_ref, pos_ref):
    rec = rec_ref[...]
    tm = rec.shape[1]
    row = lax.broadcasted_iota(jnp.int32, (ROUTER_ROWS, tm), 0).astype(F32)
    row8 = lax.broadcasted_iota(jnp.int32, rec.shape, 0)
    off = off_ref[:, 0:1]
    pos = jnp.zeros(rec.shape, F32)
    for slot, (e_row, r_row) in enumerate(((META_E0, META_R0), (META_E1, META_R1))):
        e = rec[e_row:e_row + 1, :] + ROUTER_LANE0
        seg = jnp.sum(jnp.where(row == e, off, 0.0), axis=0, keepdims=True)
        pos = jnp.where(row8 == slot, seg + rec[r_row:r_row + 1, :], pos)
    pos_ref[...] = pos.astype(jnp.int32)


def _positions(rec_all, off_rows):
    n = rec_all.shape[1]
    tm = TM_PROJ
    assert n % tm == 0
    return pl.pallas_call(
        _positions_body,
        grid=(n // tm,),
        in_specs=[pl.BlockSpec((SUBLANES, tm), lambda i: (0, i)), _resident((ROUTER_ROWS, LANES))],
        out_specs=pl.BlockSpec((SUBLANES, tm), lambda i: (0, i)),
        out_shape=jax.ShapeDtypeStruct((SUBLANES, n), jnp.int32),
        compiler_params=_params(),
        name="positions",
    )(rec_all, off_rows)


N_LOAD_SLOTS = 3


def _pow2_sizes(n):
    return [1 << b for b in range(n.bit_length() - 1, -1, -1)]


def _dispatch_body(pos_ref, nv_ref, h_ref, hs_ref, hbuf, zbuf, lsem, ssem, zsem, *, n_tok, tm):
    i = pl.program_id(0)
    n_steps = pl.num_programs(0)
    n_tiles = nv_ref.shape[0]

    groups = tm // SUBLANES

    def load(t, slot, sem=lsem):
        return pltpu.make_async_copy(h_ref.at[pl.ds(t * groups, groups)], hbuf.at[slot], sem.at[slot])

    def clear_unowned(start):
        def per_tile(t, c):
            nv = nv_ref[t]
            z = TR - nv
            row = t * TR + nv
            head = z & (SUBLANES - 1)
            for k in range(SUBLANES - 1):
                @pl.when(k < head)
                def _(k=k):
                    cp = pltpu.make_async_copy(zbuf.at[pl.ds(0, 1), :], hs_ref.at[pl.ds(row + k, 1), :], zsem)
                    cp.start() if start else cp.wait()
            row = row + head
            for size in _pow2_sizes(TR):
                if size < SUBLANES:
                    break
                @pl.when((z & size) != 0)
                def _(row=row, size=size):
                    dst = hs_ref.at[pl.ds(pl.multiple_of(row, SUBLANES), size), :]
                    cp = pltpu.make_async_copy(zbuf.at[pl.ds(0, size), :], dst, zsem)
                    cp.start() if start else cp.wait()
                row = row + (z & size)
            return c
        lax.fori_loop(0, n_tiles, per_tile, 0)

    def scatter_wait(slot):
        for _ in range(2):
            load(0, slot, ssem).wait()

    @pl.when(i == 0)
    def _():
        zbuf[...] = jnp.zeros_like(zbuf)
        clear_unowned(True)
        load(0, 0).start()

    @pl.when(i + 1 < n_steps)
    def _():
        load(i + 1, (i + 1) % N_LOAD_SLOTS).start()

    slot = i % N_LOAD_SLOTS
    load(i, slot).wait()

    def issue(jb, c):
        for u in range(SUBLANES):
            for s in range(2):
                p = pos_ref[s * n_tok + i * tm + jb * SUBLANES + u]
                pltpu.make_async_copy(hbuf.at[slot, jb, pl.ds(u, 1), :], hs_ref.at[pl.ds(p, 1), :],
                                      ssem.at[slot]).start()
        return c
    lax.fori_loop(0, groups, issue, 0)

    @pl.when(i > 0)
    def _():
        scatter_wait((i + N_LOAD_SLOTS - 1) % N_LOAD_SLOTS)

    @pl.when(i == n_steps - 1)
    def _():
        scatter_wait(slot)
        clear_unowned(False)


def _dispatch(pos, nv, rows):
    n_tok, width = rows.shape
    n_tiles = nv.shape[0]
    tm = TM_PROJ
    assert n_tok % tm == 0
    return pl.pallas_call(
        functools.partial(_dispatch_body, n_tok=n_tok, tm=tm),
        grid_spec=pltpu.PrefetchScalarGridSpec(
            num_scalar_prefetch=2,
            grid=(n_tok // tm,),
            in_specs=[pl.BlockSpec(memory_space=pl.ANY)],
            out_specs=pl.BlockSpec(memory_space=pl.ANY),
            scratch_shapes=[pltpu.VMEM((N_LOAD_SLOTS, tm // SUBLANES, SUBLANES, width), rows.dtype),
                            pltpu.VMEM((TR, width), rows.dtype),
                            pltpu.SemaphoreType.DMA((N_LOAD_SLOTS,)), pltpu.SemaphoreType.DMA((N_LOAD_SLOTS,)),
                            pltpu.SemaphoreType.DMA(())],
        ),
        out_shape=jax.ShapeDtypeStruct((n_tiles * TR, width), rows.dtype),
        compiler_params=_params(),
        name="dispatch",
    )(pos, nv, rows.reshape(n_tok // SUBLANES, SUBLANES, width))


N_MOE_LOADS = 4


def _moe_body(ts_ref, hs_ref, g2_ref, wg_ref, wu_ref, wd_ref, ys_ref, xbuf, obuf, wg_bf, wu_bf, wd_bf, lsem, ssem):
    e = pl.program_id(0)
    n_tiles = ys_ref.shape[0] // TR
    n_used = ts_ref[N_EXPERTS]
    t0, t1 = ts_ref[e], ts_ref[e + 1]

    def load(g, slot):
        return pltpu.make_async_copy(hs_ref.at[pl.ds(pl.multiple_of(g * TR, TR), TR), :], xbuf.at[slot], lsem.at[slot])

    def store(g, slot):
        return pltpu.make_async_copy(obuf.at[slot], ys_ref.at[pl.ds(pl.multiple_of(g * TR, TR), TR), :], ssem.at[slot])

    @pl.when(e == 0)
    def _():
        for g in range(N_MOE_LOADS - 1):
            @pl.when(g < n_used)
            def _(g=g):
                load(g, g).start()

    @pl.when(t1 > t0)
    def _():
        wg_bf[...] = wg_ref[0].astype(BF16)
        wu_bf[...] = wu_ref[0].astype(BF16)
        wd_bf[...] = wd_ref[0].astype(BF16)

    def tile(g, c):
        slot = g % 2
        xslot = g % N_MOE_LOADS
        ahead = g + N_MOE_LOADS - 1

        @pl.when(ahead < n_used)
        def _():
            load(ahead, ahead % N_MOE_LOADS).start()

        load(g, xslot).wait()
        xn = _rmsnorm(xbuf[xslot], g2_ref[...]).astype(BF16)
        hg = _dot(xn, wg_bf[...])
        hu = _dot(xn, wu_bf[...])
        hid = (hg * jax.nn.sigmoid(hg) * hu).astype(BF16)
        y = _dot(hid, wd_bf[...])

        @pl.when(g >= 2)
        def _():
            store(g - 2, slot).wait()

        obuf[slot] = y
        store(g, slot).start()
        return c

    lax.fori_loop(t0, t1, tile, 0)

    @pl.when(e == pl.num_programs(0) - 1)
    def _():
        @pl.when(n_used >= 2)
        def _():
            store(n_used - 2, n_used % 2).wait()
        store(n_used - 1, (n_used - 1) % 2).wait()
        obuf[0] = jnp.zeros(obuf.shape[1:], obuf.dtype)

        def clear(g, c):
            store(g, 0).start()
            return c
        lax.fori_loop(n_used, n_tiles, clear, 0)

        def drain(g, c):
            store(g, 0).wait()
            return c
        lax.fori_loop(n_used, n_tiles, drain, 0)


def _moe(ts, hs, g2, wg, wu, wd):
    wspec = lambda shape: pl.BlockSpec((1,) + shape, lambda e, ts: (e, 0, 0))
    tile_buf = lambda slots: pltpu.VMEM((slots, TR, hs.shape[1]), hs.dtype)
    return pl.pallas_call(
        _moe_body,
        grid_spec=pltpu.PrefetchScalarGridSpec(
            num_scalar_prefetch=1,
            grid=(N_EXPERTS,),
            in_specs=[pl.BlockSpec(memory_space=pl.ANY),
                      pl.BlockSpec((1, D_MODEL), lambda e, ts: (0, 0)),
                      wspec((D_MODEL, D_EXPERT)), wspec((D_MODEL, D_EXPERT)), wspec((D_EXPERT, D_MODEL))],
            out_specs=pl.BlockSpec(memory_space=pl.ANY),
            scratch_shapes=[tile_buf(N_MOE_LOADS), tile_buf(2),
                            pltpu.VMEM((D_MODEL, D_EXPERT), BF16), pltpu.VMEM((D_MODEL, D_EXPERT), BF16),
                            pltpu.VMEM((D_EXPERT, D_MODEL), BF16),
                            pltpu.SemaphoreType.DMA((N_MOE_LOADS,)), pltpu.SemaphoreType.DMA((2,))],
        ),
        out_shape=jax.ShapeDtypeStruct(hs.shape, hs.dtype),
        compiler_params=_params(),
        name="moe",
    )(ts, hs, g2, wg, wu, wd)


def _combine_body(pos_ref, h_ref, meta_ref, g_ref, ys_ref, ys_grouped_ref, o_ref, ybuf, sem, *, n_tok, row0):
    i = pl.program_id(0)
    n_steps = pl.num_programs(0)
    tm = h_ref.shape[0]

    groups = tm // SUBLANES

    def gather(t, slot):
        def issue(jb, c):
            for u in range(SUBLANES):
                for s in range(2):
                    p = pos_ref[s * n_tok + row0 + t * tm + jb * SUBLANES + u]
                    pltpu.make_async_copy(ys_ref.at[pl.ds(p, 1), :], ybuf.at[slot, s, jb, pl.ds(u, 1), :],
                                          sem.at[slot]).start()
            return c
        lax.fori_loop(0, groups, issue, 0)

    @pl.when(i == 0)
    def _():
        gather(0, 0)

    @pl.when(i + 1 < n_steps)
    def _():
        gather(i + 1, (i + 1) % 2)

    slot = i % 2
    for s in range(2):
        pltpu.make_async_copy(ys_grouped_ref.at[pl.ds(0, groups)], ybuf.at[slot, s], sem.at[slot]).wait()
    meta = meta_ref[...]
    g0 = meta[:, META_G0:META_G0 + 1]
    g1 = meta[:, META_G1:META_G1 + 1]
    y0 = ybuf[slot, 0].reshape(tm, D_MODEL)
    y1 = ybuf[slot, 1].reshape(tm, D_MODEL)
    h = h_ref[...] + (g0 * y0 + g1 * y1)
    o_ref[...] = _rmsnorm(h, g_ref[...])


def _combine(pos, h_all, meta_all, ys, gf, row0, n):
    n_tok = h_all.shape[0]
    tm = min(TM_OUT, n)
    assert row0 % tm == 0 and n % tm == 0
    b0 = row0 // tm
    return pl.pallas_call(
        functools.partial(_combine_body, n_tok=n_tok, row0=row0),
        grid_spec=pltpu.PrefetchScalarGridSpec(
            num_scalar_prefetch=1,
            grid=(n // tm,),
            in_specs=[pl.BlockSpec((tm, D_MODEL), lambda i, pos: (b0 + i, 0)),
                      pl.BlockSpec((tm, LANES), lambda i, pos: (b0 + i, 0)),
                      pl.BlockSpec((1, D_MODEL), lambda i, pos: (0, 0)),
                      pl.BlockSpec(memory_space=pl.ANY), pl.BlockSpec(memory_space=pl.ANY)],
            out_specs=pl.BlockSpec((tm, D_MODEL), lambda i, pos: (i, 0)),
            scratch_shapes=[pltpu.VMEM((2, 2, tm // SUBLANES, SUBLANES, ys.shape[1]), ys.dtype),
                            pltpu.SemaphoreType.DMA((2,))],
        ),
        out_shape=jax.ShapeDtypeStruct((n, D_MODEL), F32),
        compiler_params=_params(),
        name="combine",
    )(pos, h_all, meta_all, gf, ys, ys.reshape(ys.shape[0] // SUBLANES, SUBLANES, ys.shape[1]))


def _routing_tables(counts, n_tok):
    n_tiles = (2 * n_tok) // TR + N_EXPERTS
    cnt = counts[ROUTER_LANE0:ROUTER_LANE0 + N_EXPERTS].astype(jnp.int32)
    tiles_e = (cnt + TR - 1) // TR
    tile_end = jnp.cumsum(tiles_e)
    tile_start = tile_end - tiles_e
    off_rows = jnp.pad((tile_start * TR).astype(F32), (ROUTER_LANE0, ROUTER_ROWS - ROUTER_LANE0 - N_EXPERTS))
    off_rows = jnp.broadcast_to(off_rows[:, None], (ROUTER_ROWS, LANES))
    ts = jnp.concatenate([tile_start, tile_end[-1:]]).astype(jnp.int32)
    tile = jnp.arange(n_tiles, dtype=jnp.int32)
    owner = (tile[:, None] >= tile_start[None, :]) & (tile[:, None] < tile_end[None, :])
    rows_left = jnp.sum(jnp.where(owner, cnt[None, :] - (tile[:, None] - tile_start[None, :]) * TR, 0), axis=1)
    nv = jnp.clip(rows_left, 0, TR).astype(jnp.int32)
    return off_rows, ts, nv


def kernel(x_prompt, x_sample, cache_win_k, cache_win_v, state_conv, norm1_g, w_in, attn_sink, conv_dw_w, conv_dw_b,
           conv_ln_g, conv_ln_b, conv_pw_w, conv_pw_b, w_out, norm2_g, router_group_w, router_group_b,
           router_expert_w, router_expert_b, expert_w_gate, expert_w_up, expert_w_down, final_norm_g):
    depth = w_in.shape[0]
    assert depth == 1, "single-layer step"
    bp, sp, _ = x_prompt.shape
    assert bp == 1, "one prompt sequence"
    n_seq, t_new, _ = x_sample.shape
    n_p, n_s = bp * sp, n_seq * t_new
    n_tok = n_p + n_s
    l = 0

    row = lambda a: a.reshape(1, -1)
    w_in_bf = w_in[l].astype(BF16)
    w_out_bf = w_out[l].astype(BF16)
    pw_bf = conv_pw_w[l].astype(BF16)
    rw = jnp.concatenate([router_group_w[l], router_expert_w[l]], axis=1)
    rw_bf = jnp.pad(rw, ((0, 0), (0, LANES - rw.shape[1]))).astype(BF16)
    rb = jnp.pad(jnp.concatenate([router_group_b[l], router_expert_b[l]]), (0, LANES - rw.shape[1])).reshape(1, LANES)
    g1, g2, gf = row(norm1_g[l]), row(norm2_g[l]), row(final_norm_g)
    sink = attn_sink[l]
    conv_w = (conv_dw_w[l], row(conv_dw_b[l]), row(conv_ln_g[l]), row(conv_ln_b[l]), pw_bf, row(conv_pw_b[l]))

    xp = x_prompt.reshape(n_p, D_MODEL)
    xs = x_sample.reshape(n_s, D_MODEL)
    cache_axes, cache_axes_back = (0, 1, 3, 4, 2), (0, 1, 4, 2, 3)
    ck = jnp.transpose(cache_win_k, cache_axes)
    cv = jnp.transpose(cache_win_v, cache_axes)
    cs = jnp.transpose(state_conv, (0, 2, 1, 3))

    qp, kp, vp, up = _in_proj(xp, g1, w_in_bf)
    qs, ks, vs, us = _in_proj(xs, g1, w_in_bf)
    mix_p = _mixer_prompt(sink, qp, kp, vp, up, *conv_w)
    mix_s, wk_s, wv_s, wc_s = _mixer_sample(sink, qs, ks, vs, us, ck, cv, cs, *conv_w)

    h_all, meta_all, rec_all, cnt = _out_router(mix_p, xp, mix_s, xs, w_out_bf, g2, rw_bf, rb)

    off_rows, ts, nv = _routing_tables(cnt[:, 0], n_tok)
    pos_rows = _positions(rec_all, off_rows)
    pos = jnp.concatenate([pos_rows[0], pos_rows[1]])
    hs = _dispatch(pos, nv, h_all)
    ys = _moe(ts, hs, g2, expert_w_gate[l], expert_w_up[l], expert_w_down[l])
    y_p = _combine(pos, h_all, meta_all, ys, gf, 0, n_p)
    y_s = _combine(pos, h_all, meta_all, ys, gf, n_p, n_s)

    kv_shape = (depth, bp, WINDOW, N_KV_HEADS, HEAD_DIM)
    return (y_p.reshape(bp, sp, D_MODEL), y_s.reshape(n_seq, t_new, D_MODEL),
            kp[n_p - WINDOW:].reshape(kv_shape), vp[n_p - WINDOW:].reshape(kv_shape),
            up[n_p - (CONV_W - 1):].reshape(depth, bp, CONV_W - 1, D_CONV),
            jnp.transpose(wk_s, cache_axes_back), jnp.transpose(wv_s, cache_axes_back),
            jnp.transpose(wc_s, (0, 2, 1, 3)))
```

```python
import functools

import jax
import jax.numpy as jnp
from jax import lax
from jax.experimental import pallas as pl
from jax.experimental.pallas import tpu as pltpu

F32 = jnp.float32
BF16 = jnp.bfloat16

D_MODEL = 2048
HEAD_DIM = 64
N_HEADS = 16
N_KV_HEADS = 2
GQA_GROUP = 8
KV_DIM = N_KV_HEADS * HEAD_DIM
D_ATTN = N_HEADS * HEAD_DIM
D_CONV = D_MODEL - D_ATTN
WINDOW = 128
CONV_W = 31
D_IN = D_ATTN + 2 * KV_DIM + 2 * D_CONV
N_GROUPS = 4
EXPERTS_PER_GROUP = 8
N_EXPERTS = N_GROUPS * EXPERTS_PER_GROUP
D_EXPERT = 256
RMS_EPS = 1e-6
LN_EPS = 1e-5

LANES = 128
SUBLANES = 8
ROUTER_LANE0 = N_GROUPS
VMEM_LIMIT = 56 * 1024 * 1024

TM_PROJ = 512
TM_MIX = 256
SEQ_TILE = 8
TR = 128
TM_OUT = 256

_NT = (((1,), (1,)), ((), ()))


def _params(n_axes=1):
    return pltpu.CompilerParams(dimension_semantics=("arbitrary",) * n_axes, vmem_limit_bytes=VMEM_LIMIT)


def _resident(shape):
    return pl.BlockSpec(shape, lambda *_: (0,) * len(shape), pipeline_mode=pl.Buffered(1))


def _rmsnorm(x, g):
    ms = jnp.mean(x * x, axis=-1, keepdims=True)
    return x * lax.rsqrt(ms + RMS_EPS) * g


def _dot(a, b):
    return jnp.dot(a, b, preferred_element_type=F32)


def _dot_nt(a, b):
    return lax.dot_general(a, b, _NT, preferred_element_type=F32)


def _in_proj_body(x_ref, g_ref, w_ref, q_ref, k_ref, v_ref, u_ref):
    xn = _rmsnorm(x_ref[...], g_ref[...]).astype(BF16)
    q_ref[...] = (_dot(xn, w_ref[:, :D_ATTN]) * (HEAD_DIM ** -0.5)).astype(BF16)
    kv = _dot(xn, w_ref[:, D_ATTN:D_ATTN + 2 * KV_DIM])
    k_ref[...] = kv[:, :KV_DIM]
    v_ref[...] = kv[:, KV_DIM:]
    c0 = D_ATTN + 2 * KV_DIM
    cw = 256
    for j in range(D_CONV // cw):
        a = _dot(xn, w_ref[:, c0 + j * cw:c0 + (j + 1) * cw])
        b = _dot(xn, w_ref[:, c0 + D_CONV + j * cw:c0 + D_CONV + (j + 1) * cw])
        u_ref[:, j * cw:(j + 1) * cw] = a * jax.nn.sigmoid(b)


def _in_proj(x, g, w_bf):
    n = x.shape[0]
    tm = min(TM_PROJ, n)
    row = lambda w: pl.BlockSpec((tm, w), lambda i: (i, 0))
    return pl.pallas_call(
        _in_proj_body,
        grid=(n // tm,),
        in_specs=[row(D_MODEL), _resident((1, D_MODEL)), _resident((D_MODEL, D_IN))],
        out_specs=[row(D_ATTN), row(KV_DIM), row(KV_DIM), row(D_CONV)],
        out_shape=[jax.ShapeDtypeStruct((n, D_ATTN), BF16), jax.ShapeDtypeStruct((n, KV_DIM), F32),
                   jax.ShapeDtypeStruct((n, KV_DIM), F32), jax.ShapeDtypeStruct((n, D_CONV), F32)],
        compiler_params=_params(),
        name="in_proj",
    )(x, g, w_bf)


def _attend(q, parts, sink_ref, kv_head, keys_on_lanes=False):
    t = q.shape[0]
    score, weigh = (_dot, _dot_nt) if keys_on_lanes else (_dot_nt, _dot)
    qs = jnp.concatenate([q[:, g * HEAD_DIM:(g + 1) * HEAD_DIM] for g in range(GQA_GROUP)], axis=0)
    scores = [score(qs, kk) for kk, _, _ in parts]
    hidden = [jnp.where(mask, 0.0, -jnp.inf) for _, _, mask in parts]
    ps = [[] for _ in parts]
    inv = []
    for g in range(GQA_GROUP):
        sg = [s[g * t:(g + 1) * t] + neg for s, neg in zip(scores, hidden)]
        sk = sink_ref[kv_head * GQA_GROUP + g]
        m = sk
        for x in sg:
            m = jnp.maximum(jnp.max(x, axis=-1, keepdims=True), m)
        den = jnp.exp(sk - m)
        for k, x in enumerate(sg):
            p = jnp.exp(x - m)
            den = den + jnp.sum(p, axis=-1, keepdims=True)
            ps[k].append(p.astype(BF16))
        inv.append(1.0 / den)
    o = None
    for k, (_, vv, _) in enumerate(parts):
        ok = weigh(jnp.concatenate(ps[k], axis=0), vv)
        o = ok if o is None else o + ok
    return jnp.concatenate([o[g * t:(g + 1) * t] * inv[g] for g in range(GQA_GROUP)], axis=1)


def _conv_tail(y, dwb_ref, lng_ref, lnb_ref, pw_ref, pwb_ref):
    y = y + dwb_ref[...]
    mu = jnp.mean(y, axis=-1, keepdims=True)
    yc = y - mu
    yn = yc * lax.rsqrt(jnp.mean(yc * yc, axis=-1, keepdims=True) + LN_EPS)
    yn = yn * lng_ref[...] + lnb_ref[...]
    act = yn * jax.nn.sigmoid(yn)
    return _dot(act.astype(BF16), pw_ref[...]) + pwb_ref[...]


CONV_PAD = 32
CONV_ROWS = 128
CONV_COLS = 128


def _mixer_prompt_body(sink_ref, q_ref, k_ref, v_ref, u_ref, dww_ref, dwb_ref, lng_ref, lnb_ref, pw_ref, pwb_ref,
                       mix_ref, kprev, vprev, uext, ushift, ybuf):
    i = pl.program_id(0)
    tm = q_ref.shape[0]

    @pl.when(i == 0)
    def _():
        kprev[...] = jnp.zeros_like(kprev)
        vprev[...] = jnp.zeros_like(vprev)
        uext[0:CONV_PAD, :] = jnp.zeros((CONV_PAD, D_CONV), F32)

    qi = lax.broadcasted_iota(jnp.int32, (WINDOW, 2 * WINDOW), 0)
    kj = lax.broadcasted_iota(jnp.int32, (WINDOW, 2 * WINDOW), 1)
    diff = qi + WINDOW - kj
    band = (diff >= 0) & (diff <= WINDOW)
    kp, vp = kprev[...], vprev[...]
    for b in range(tm // WINDOW):
        rows = slice(b * WINDOW, (b + 1) * WINDOW)
        kb = k_ref[rows, :].astype(BF16)
        vb = v_ref[rows, :].astype(BF16)
        kk = jnp.concatenate([kp, kb], axis=0)
        vv = jnp.concatenate([vp, vb], axis=0)
        if b == 0:
            mask = band & (kj >= jnp.where(i > 0, 0, WINDOW))
        else:
            mask = band
        for h in range(N_KV_HEADS):
            cols = slice(h * HEAD_DIM, (h + 1) * HEAD_DIM)
            hq = slice(h * GQA_GROUP * HEAD_DIM, (h + 1) * GQA_GROUP * HEAD_DIM)
            o = _attend(q_ref[rows, hq], [(kk[:, cols], vv[:, cols], mask)], sink_ref, h)
            mix_ref[rows, hq] = o.astype(BF16)
        kp, vp = kb, vb
    kprev[...] = kp
    vprev[...] = vp

    uext[CONV_PAD:CONV_PAD + tm, :] = u_ref[...]
    n_shift_rows = ushift.shape[1]
    for s in range(1, SUBLANES):
        ushift[s - 1] = uext[s:s + n_shift_rows, :]
    off = CONV_PAD - (CONV_W - 1)
    for r in range(tm // CONV_ROWS):
        for c in range(D_CONV // CONV_COLS):
            cs = slice(c * CONV_COLS, (c + 1) * CONV_COLS)
            acc = jnp.zeros((CONV_ROWS, CONV_COLS), F32)
            for j in range(CONV_W):
                a, s = divmod(off + j, SUBLANES)
                r0 = r * CONV_ROWS + a * SUBLANES
                src = uext[r0:r0 + CONV_ROWS, cs] if s == 0 else ushift[s - 1, r0:r0 + CONV_ROWS, cs]
                acc = acc + dww_ref[j:j + 1, cs] * src
            ybuf[r * CONV_ROWS:(r + 1) * CONV_ROWS, cs] = acc
    uext[0:CONV_PAD, :] = uext[tm:tm + CONV_PAD, :]
    conv = _conv_tail(ybuf[...], dwb_ref, lng_ref, lnb_ref, pw_ref, pwb_ref)
    mix_ref[:, D_ATTN:] = conv.astype(BF16)


def _mixer_prompt(sink, q, k, v, u, dww, dwb, lng, lnb, pw_bf, pwb):
    n = q.shape[0]
    tm = TM_MIX
    row = lambda w: pl.BlockSpec((tm, w), lambda i, s: (i, 0))
    res = lambda shape: pl.BlockSpec(shape, lambda i, s: (0,) * len(shape), pipeline_mode=pl.Buffered(1))
    return pl.pallas_call(
        _mixer_prompt_body,
        grid_spec=pltpu.PrefetchScalarGridSpec(
            num_scalar_prefetch=1,
            grid=(n // tm,),
            in_specs=[row(D_ATTN), row(KV_DIM), row(KV_DIM), row(D_CONV), res((CONV_W, D_CONV)), res((1, D_CONV)),
                      res((1, D_CONV)), res((1, D_CONV)), res((D_CONV, D_CONV)), res((1, D_CONV))],
            out_specs=row(D_MODEL),
            scratch_shapes=[pltpu.VMEM((WINDOW, KV_DIM), BF16), pltpu.VMEM((WINDOW, KV_DIM), BF16),
                            pltpu.VMEM((CONV_PAD + tm, D_CONV), F32),
                            pltpu.VMEM((SUBLANES - 1, CONV_PAD - SUBLANES + tm, D_CONV), F32),
                            pltpu.VMEM((tm, D_CONV), F32)],
        ),
        out_shape=jax.ShapeDtypeStruct((n, D_MODEL), BF16),
        compiler_params=_params(),
        name="mixer_prompt",
    )(sink, q, k, v, u, dww, dwb, lng, lnb, pw_bf, pwb)


def _mixer_sample_body(sink_ref, q_ref, k_ref, v_ref, u_ref, ck_ref, cv_ref, cs_ref, dww_ref, dwb_ref, lng_ref,
                       lnb_ref, pw_ref, pwb_ref, mix_ref, wk_ref, wv_ref, wc_ref, usel, ybuf):
    nt = q_ref.shape[0]
    sb = ck_ref.shape[0]
    t_new = nt // sb
    n_cache = sb * WINDOW
    n_state = CONV_W - 1

    pad = jnp.zeros((LANES - nt, KV_DIM), F32)
    kn_t = jnp.concatenate([k_ref[...], pad], axis=0).T
    vn_t = jnp.concatenate([v_ref[...], pad], axis=0).T

    lt, lw = t_new.bit_length() - 1, WINDOW.bit_length() - 1
    qr = lax.broadcasted_iota(jnp.int32, (nt, n_cache), 0)
    cc = lax.broadcasted_iota(jnp.int32, (nt, n_cache), 1)
    mask_c = ((cc >> lw) == (qr >> lt)) & ((cc & (WINDOW - 1)) >= (qr & (t_new - 1)))
    qr = lax.broadcasted_iota(jnp.int32, (nt, LANES), 0)
    cn = lax.broadcasted_iota(jnp.int32, (nt, LANES), 1)
    mask_n = ((cn >> lt) == (qr >> lt)) & ((cn & (t_new - 1)) <= (qr & (t_new - 1)))
    for h in range(N_KV_HEADS):
        dims = slice(h * HEAD_DIM, (h + 1) * HEAD_DIM)
        hq = slice(h * GQA_GROUP * HEAD_DIM, (h + 1) * GQA_GROUP * HEAD_DIM)
        kc = jnp.concatenate([ck_ref[s, h] for s in range(sb)], axis=1).astype(BF16)
        vc = jnp.concatenate([cv_ref[s, h] for s in range(sb)], axis=1).astype(BF16)
        parts = [(kc, vc, mask_c), (kn_t[dims, :].astype(BF16), vn_t[dims, :].astype(BF16), mask_n)]
        o = _attend(q_ref[:, hq], parts, sink_ref, h, keys_on_lanes=True)
        mix_ref[:, hq] = o.astype(BF16)

    is_new = lax.broadcasted_iota(jnp.int32, (HEAD_DIM, WINDOW), 1) >= WINDOW - t_new
    for s in range(sb):
        shift = (WINDOW - t_new - s * t_new) % LANES
        kn_s = pltpu.roll(kn_t, shift=shift, axis=1)
        vn_s = pltpu.roll(vn_t, shift=shift, axis=1)
        for h in range(N_KV_HEADS):
            dims = slice(h * HEAD_DIM, (h + 1) * HEAD_DIM)
            old_k = pltpu.roll(ck_ref[s, h], shift=WINDOW - t_new, axis=1)
            old_v = pltpu.roll(cv_ref[s, h], shift=WINDOW - t_new, axis=1)
            wk_ref[s, h] = jnp.where(is_new, kn_s[dims, :], old_k)
            wv_ref[s, h] = jnp.where(is_new, vn_s[dims, :], old_v)

    for s in range(sb):
        for t in range(t_new):
            usel[t, s:s + 1, :] = u_ref[s * t_new + t:s * t_new + t + 1, :]
    u_new = [usel[t] for t in range(t_new)]
    for t in range(t_new):
        acc = jnp.zeros((sb, D_CONV), F32)
        for r in range(t, n_state):
            acc = acc + dww_ref[r - t:r - t + 1, :] * cs_ref[r]
        for t2 in range(t + 1):
            j = n_state - t + t2
            acc = acc + dww_ref[j:j + 1, :] * u_new[t2]
        for s in range(sb):
            ybuf[s * t_new + t:s * t_new + t + 1, :] = acc[s:s + 1, :]
        wc_ref[n_state - t_new + t] = u_new[t]
    wc_ref[0:n_state - t_new] = cs_ref[t_new:n_state]
    conv = _conv_tail(ybuf[...], dwb_ref, lng_ref, lnb_ref, pw_ref, pwb_ref)
    mix_ref[:, D_ATTN:] = conv.astype(BF16)


def _mixer_sample(sink, q, k, v, u, ck, cv, cs, dww, dwb, lng, lnb, pw_bf, pwb):
    n = q.shape[0]
    n_seq = ck.shape[1]
    t_new = n // n_seq
    sb = SEQ_TILE
    nt = sb * t_new
    n_state = CONV_W - 1
    row = lambda w: pl.BlockSpec((nt, w), lambda i, s: (i, 0))
    cache = pl.BlockSpec((None, sb, N_KV_HEADS, HEAD_DIM, WINDOW), lambda i, s: (0, i, 0, 0, 0))
    state = pl.BlockSpec((None, n_state, sb, D_CONV), lambda i, s: (0, 0, i, 0))
    res = lambda shape: pl.BlockSpec(shape, lambda i, s: (0,) * len(shape), pipeline_mode=pl.Buffered(1))
    return pl.pallas_call(
        _mixer_sample_body,
        grid_spec=pltpu.PrefetchScalarGridSpec(
            num_scalar_prefetch=1,
            grid=(n_seq // sb,),
            in_specs=[row(D_ATTN), row(KV_DIM), row(KV_DIM), row(D_CONV), cache, cache, state,
                      res((CONV_W, D_CONV)), res((1, D_CONV)), res((1, D_CONV)), res((1, D_CONV)),
                      res((D_CONV, D_CONV)), res((1, D_CONV))],
            out_specs=[row(D_MODEL), cache, cache, state],
            scratch_shapes=[pltpu.VMEM((t_new, sb, D_CONV), F32), pltpu.VMEM((nt, D_CONV), F32)],
        ),
        out_shape=[jax.ShapeDtypeStruct((n, D_MODEL), BF16), jax.ShapeDtypeStruct(ck.shape, F32),
                   jax.ShapeDtypeStruct(cv.shape, F32), jax.ShapeDtypeStruct(cs.shape, F32)],
        compiler_params=_params(),
        name="mixer_sample",
    )(sink, q, k, v, u, ck, cv, cs, dww, dwb, lng, lnb, pw_bf, pwb)


META_E0, META_E1, META_G0, META_G1, META_R0, META_R1 = range(6)


ROUTER_ROWS = 40


def _out_router_body(mixp_ref, xp_ref, mixs_ref, xs_ref, wout_ref, g2_ref, rw_ref, rb_ref, h_ref, meta_ref, rec_ref,
                     cnt_ref, carry, earlier, *, prompt_tiles):
    i = pl.program_id(0)
    tm = xp_ref.shape[0]

    @pl.when(i == 0)
    def _():
        carry[...] = jnp.zeros_like(carry)
        ti = lax.broadcasted_iota(jnp.int32, (tm, tm), 0)
        tj = lax.broadcasted_iota(jnp.int32, (tm, tm), 1)
        earlier[...] = jnp.where(ti < tj, 1.0, 0.0).astype(BF16)

    is_prompt = i < prompt_tiles
    x = jnp.where(is_prompt, xp_ref[...], xs_ref[...])
    mix = jnp.where(is_prompt, mixp_ref[...], mixs_ref[...])
    h = x + _dot(mix, wout_ref[...])
    h_ref[...] = h
    xn = _rmsnorm(h, g2_ref[...]).astype(BF16)
    logits = _dot(xn, rw_ref[...]) + rb_ref[...]
    lt = logits.T[:ROUTER_ROWS, :]
    row = lax.broadcasted_iota(jnp.int32, lt.shape, 0).astype(F32)
    first = lambda cond: jnp.min(jnp.where(cond, row, float(LANES)), axis=0, keepdims=True)

    gl = jnp.where(row < N_GROUPS, lt, -jnp.inf)
    gmax = jnp.max(gl, axis=0, keepdims=True)
    gidx = first(gl == gmax)
    gval = 1.0 / jnp.sum(jnp.exp(gl - gmax), axis=0, keepdims=True)

    lo = ROUTER_LANE0 + EXPERTS_PER_GROUP * gidx
    el = jnp.where((row >= lo) & (row < lo + EXPERTS_PER_GROUP), lt, -jnp.inf)
    m1 = jnp.max(el, axis=0, keepdims=True)
    i1 = first(el == m1)
    el2 = jnp.where(row == i1, -jnp.inf, el)
    m2 = jnp.max(el2, axis=0, keepdims=True)
    i2 = first(el2 == m2)
    r = jnp.exp(m2 - m1)
    g0 = gval * (1.0 / (1.0 + r))
    g1 = gval * (r / (1.0 + r))

    sel0, sel1 = row == i1, row == i2
    onehot = jnp.where(sel0 | sel1, 1.0, 0.0)
    before = _dot(onehot.astype(BF16), earlier[...]) + carry[0:ROUTER_ROWS, 0:1]
    r0 = jnp.sum(jnp.where(sel0, before, 0.0), axis=0, keepdims=True)
    r1 = jnp.sum(jnp.where(sel1, before, 0.0), axis=0, keepdims=True)
    carry[0:ROUTER_ROWS, :] = carry[0:ROUTER_ROWS, :] + jnp.sum(onehot, axis=1, keepdims=True)
    cnt_ref[...] = carry[...]

    row8 = lax.broadcasted_iota(jnp.int32, (SUBLANES, tm), 0)
    rec = jnp.zeros((SUBLANES, tm), F32)
    for slot, val in ((META_E0, i1 - ROUTER_LANE0), (META_E1, i2 - ROUTER_LANE0), (META_G0, g0), (META_G1, g1),
                      (META_R0, r0), (META_R1, r1)):
        rec = jnp.where(row8 == slot, val, rec)
    rec_ref[...] = rec
    meta_ref[...] = jnp.concatenate([rec, jnp.zeros((LANES - SUBLANES, tm), F32)], axis=0).T


def _out_router(mix_p, x_p, mix_s, x_s, wout_bf, g2, rw_bf, rb):
    n_p, n_s = x_p.shape[0], x_s.shape[0]
    tm = TM_PROJ
    assert n_p % tm == 0 and n_s % tm == 0
    tp, ts = n_p // tm, n_s // tm
    n = n_p + n_s
    prow = lambda w: pl.BlockSpec((tm, w), lambda i: (jnp.minimum(i, tp - 1), 0))
    srow = lambda w: pl.BlockSpec((tm, w), lambda i: (jnp.maximum(i - tp, 0), 0))
    row = lambda w: pl.BlockSpec((tm, w), lambda i: (i, 0))
    return pl.pallas_call(
        functools.partial(_out_router_body, prompt_tiles=tp),
        grid=(tp + ts,),
        in_specs=[prow(D_MODEL), prow(D_MODEL), srow(D_MODEL), srow(D_MODEL), _resident((D_MODEL, D_MODEL)),
                  _resident((1, D_MODEL)), _resident((D_MODEL, LANES)), _resident((1, LANES))],
        out_specs=[row(D_MODEL), row(LANES), pl.BlockSpec((SUBLANES, tm), lambda i: (0, i)),
                   pl.BlockSpec((LANES, LANES), lambda i: (0, 0))],
        out_shape=[jax.ShapeDtypeStruct((n, D_MODEL), F32), jax.ShapeDtypeStruct((n, LANES), F32),
                   jax.ShapeDtypeStruct((SUBLANES, n), F32), jax.ShapeDtypeStruct((LANES, LANES), F32)],
        scratch_shapes=[pltpu.VMEM((LANES, LANES), F32), pltpu.VMEM((tm, tm), BF16)],
        compiler_params=_params(),
        name="out_router",
    )(mix_p, x_p, mix_s, x_s, wout_bf, g2, rw_bf, rb)


def _positions_body(rec_ref, off_ref, pos_ref):
    rec = rec_ref[...]
    tm = rec.shape[1]
    row = lax.broadcasted_iota(jnp.int32, (ROUTER_ROWS, tm), 0).astype(F32)
    row8 = lax.broadcasted_iota(jnp.int32, rec.shape, 0)
    off = off_ref[:, 0:1]
    pos = jnp.zeros(rec.shape, F32)
    for slot, (e_row, r_row) in enumerate(((META_E0, META_R0), (META_E1, META_R1))):
        e = rec[e_row:e_row + 1, :] + ROUTER_LANE0
        seg = jnp.sum(jnp.where(row == e, off, 0.0), axis=0, keepdims=True)
        pos = jnp.where(row8 == slot, seg + rec[r_row:r_row + 1, :], pos)
    pos_ref[...] = pos.astype(jnp.int32)


def _positions(rec_all, off_rows):
    n = rec_all.shape[1]
    tm = n
    return pl.pallas_call(
        _positions_body,
        grid=(n // tm,),
        in_specs=[pl.BlockSpec((SUBLANES, tm), lambda i: (0, i)), _resident((ROUTER_ROWS, LANES))],
        out_specs=pl.BlockSpec((SUBLANES, tm), lambda i: (0, i)),
        out_shape=jax.ShapeDtypeStruct((SUBLANES, n), jnp.int32),
        compiler_params=_params(),
        name="positions",
    )(rec_all, off_rows)


N_LOAD_SLOTS = 3


def _pow2_sizes(n):
    return [1 << b for b in range(n.bit_length() - 1, -1, -1)]


def _dispatch_body(pos_ref, nv_ref, h_ref, hs_ref, hbuf, zbuf, lsem, ssem, zsem, *, n_tok, tm):
    i = pl.program_id(0)
    n_steps = pl.num_programs(0)
    n_tiles = nv_ref.shape[0]

    groups = tm // SUBLANES

    def load(t, slot, sem=lsem):
        return pltpu.make_async_copy(h_ref.at[pl.ds(t * groups, groups)], hbuf.at[slot], sem.at[slot])

    def clear_unowned(start):
        def per_tile(t, c):
            nv = nv_ref[t]
            z = TR - nv
            row = t * TR + nv
            head = z & (SUBLANES - 1)
            for k in range(SUBLANES - 1):
                @pl.when(k < head)
                def _(k=k):
                    cp = pltpu.make_async_copy(zbuf.at[pl.ds(0, 1), :], hs_ref.at[pl.ds(row + k, 1), :], zsem)
                    cp.start() if start else cp.wait()
            row = row + head
            for size in _pow2_sizes(TR):
                if size < SUBLANES:
                    break
                @pl.when((z & size) != 0)
                def _(row=row, size=size):
                    dst = hs_ref.at[pl.ds(pl.multiple_of(row, SUBLANES), size), :]
                    cp = pltpu.make_async_copy(zbuf.at[pl.ds(0, size), :], dst, zsem)
                    cp.start() if start else cp.wait()
                row = row + (z & size)
            return c
        lax.fori_loop(0, n_tiles, per_tile, 0)

    def scatter_wait(slot):
        for _ in range(2):
            load(0, slot, ssem).wait()

    @pl.when(i == 0)
    def _():
        zbuf[...] = jnp.zeros_like(zbuf)
        clear_unowned(True)
        load(0, 0).start()

    @pl.when(i + 1 < n_steps)
    def _():
        load(i + 1, (i + 1) % N_LOAD_SLOTS).start()

    slot = i % N_LOAD_SLOTS
    load(i, slot).wait()

    def issue(jb, c):
        for u in range(SUBLANES):
            for s in range(2):
                p = pos_ref[s * n_tok + i * tm + jb * SUBLANES + u]
                pltpu.make_async_copy(hbuf.at[slot, jb, pl.ds(u, 1), :], hs_ref.at[pl.ds(p, 1), :],
                                      ssem.at[slot]).start()
        return c
    lax.fori_loop(0, groups, issue, 0)

    @pl.when(i > 0)
    def _():
        scatter_wait((i + N_LOAD_SLOTS - 1) % N_LOAD_SLOTS)

    @pl.when(i == n_steps - 1)
    def _():
        scatter_wait(slot)
        clear_unowned(False)


def _dispatch(pos, nv, rows):
    n_tok, width = rows.shape
    n_tiles = nv.shape[0]
    tm = TM_PROJ
    assert n_tok % tm == 0
    return pl.pallas_call(
        functools.partial(_dispatch_body, n_tok=n_tok, tm=tm),
        grid_spec=pltpu.PrefetchScalarGridSpec(
            num_scalar_prefetch=2,
            grid=(n_tok // tm,),
            in_specs=[pl.BlockSpec(memory_space=pl.ANY)],
            out_specs=pl.BlockSpec(memory_space=pl.ANY),
            scratch_shapes=[pltpu.VMEM((N_LOAD_SLOTS, tm // SUBLANES, SUBLANES, width), rows.dtype),
                            pltpu.VMEM((TR, width), rows.dtype),
                            pltpu.SemaphoreType.DMA((N_LOAD_SLOTS,)), pltpu.SemaphoreType.DMA((N_LOAD_SLOTS,)),
                            pltpu.SemaphoreType.DMA(())],
        ),
        out_shape=jax.ShapeDtypeStruct((n_tiles * TR, width), rows.dtype),
        compiler_params=_params(),
        name="dispatch",
    )(pos, nv, rows.reshape(n_tok // SUBLANES, SUBLANES, width))


N_MOE_LOADS = 4


def _moe_body(ts_ref, hs_ref, g2_ref, wg_ref, wu_ref, wd_ref, ys_ref, xbuf, obuf, wg_bf, wu_bf, wd_bf, lsem, ssem):
    e = pl.program_id(0)
    n_tiles = ys_ref.shape[0] // TR
    n_used = ts_ref[N_EXPERTS]
    t0, t1 = ts_ref[e], ts_ref[e + 1]

    def load(g, slot):
        return pltpu.make_async_copy(hs_ref.at[pl.ds(pl.multiple_of(g * TR, TR), TR), :], xbuf.at[slot], lsem.at[slot])

    def store(g, slot):
        return pltpu.make_async_copy(obuf.at[slot], ys_ref.at[pl.ds(pl.multiple_of(g * TR, TR), TR), :], ssem.at[slot])

    @pl.when(e == 0)
    def _():
        for g in range(N_MOE_LOADS - 1):
            @pl.when(g < n_used)
            def _(g=g):
                load(g, g).start()

    @pl.when(t1 > t0)
    def _():
        wg_bf[...] = wg_ref[0].astype(BF16)
        wu_bf[...] = wu_ref[0].astype(BF16)
        wd_bf[...] = wd_ref[0].astype(BF16)

    def tile(g, c):
        slot = g % 2
        xslot = g % N_MOE_LOADS
        ahead = g + N_MOE_LOADS - 1

        @pl.when(ahead < n_used)
        def _():
            load(ahead, ahead % N_MOE_LOADS).start()

        load(g, xslot).wait()
        xn = _rmsnorm(xbuf[xslot], g2_ref[...]).astype(BF16)
        hg = _dot(xn, wg_bf[...])
        hu = _dot(xn, wu_bf[...])
        hid = (hg * jax.nn.sigmoid(hg) * hu).astype(BF16)
        y = _dot(hid, wd_bf[...])

        @pl.when(g >= 2)
        def _():
            store(g - 2, slot).wait()

        obuf[slot] = y
        store(g, slot).start()
        return c

    lax.fori_loop(t0, t1, tile, 0)

    @pl.when(e == pl.num_programs(0) - 1)
    def _():
        @pl.when(n_used >= 2)
        def _():
            store(n_used - 2, n_used % 2).wait()
        store(n_used - 1, (n_used - 1) % 2).wait()
        obuf[0] = jnp.zeros(obuf.shape[1:], obuf.dtype)

        def clear(g, c):
            store(g, 0).start()
            return c
        lax.fori_loop(n_used, n_tiles, clear, 0)

        def drain(g, c):
            store(g, 0).wait()
            return c
        lax.fori_loop(n_used, n_tiles, drain, 0)


def _moe(ts, hs, g2, wg, wu, wd):
    wspec = lambda shape: pl.BlockSpec((1,) + shape, lambda e, ts: (e, 0, 0))
    tile_buf = lambda slots: pltpu.VMEM((slots, TR, hs.shape[1]), hs.dtype)
    return pl.pallas_call(
        _moe_body,
        grid_spec=pltpu.PrefetchScalarGridSpec(
            num_scalar_prefetch=1,
            grid=(N_EXPERTS,),
            in_specs=[pl.BlockSpec(memory_space=pl.ANY),
                      pl.BlockSpec((1, D_MODEL), lambda e, ts: (0, 0)),
                      wspec((D_MODEL, D_EXPERT)), wspec((D_MODEL, D_EXPERT)), wspec((D_EXPERT, D_MODEL))],
            out_specs=pl.BlockSpec(memory_space=pl.ANY),
            scratch_shapes=[tile_buf(N_MOE_LOADS), tile_buf(2),
                            pltpu.VMEM((D_MODEL, D_EXPERT), BF16), pltpu.VMEM((D_MODEL, D_EXPERT), BF16),
                            pltpu.VMEM((D_EXPERT, D_MODEL), BF16),
                            pltpu.SemaphoreType.DMA((N_MOE_LOADS,)), pltpu.SemaphoreType.DMA((2,))],
        ),
        out_shape=jax.ShapeDtypeStruct(hs.shape, hs.dtype),
        compiler_params=_params(),
        name="moe",
    )(ts, hs, g2, wg, wu, wd)


def _combine_body(pos_ref, h_ref, meta_ref, g_ref, ys_ref, ys_grouped_ref, op_ref, os_ref, ybuf, sem, *,
                  n_tok, prompt_tiles):
    i = pl.program_id(0)
    n_steps = pl.num_programs(0)
    tm = h_ref.shape[0]

    groups = tm // SUBLANES

    def gather(t, slot):
        def issue(jb, c):
            for u in range(SUBLANES):
                for s in range(2):
                    p = pos_ref[s * n_tok + t * tm + jb * SUBLANES + u]
                    pltpu.make_async_copy(ys_ref.at[pl.ds(p, 1), :], ybuf.at[slot, s, jb, pl.ds(u, 1), :],
                                          sem.at[slot]).start()
            return c
        lax.fori_loop(0, groups, issue, 0)

    @pl.when(i == 0)
    def _():
        gather(0, 0)

    @pl.when(i + 1 < n_steps)
    def _():
        gather(i + 1, (i + 1) % 2)

    slot = i % 2
    for s in range(2):
        pltpu.make_async_copy(ys_grouped_ref.at[pl.ds(0, groups)], ybuf.at[slot, s], sem.at[slot]).wait()
    meta = meta_ref[...]
    g0 = meta[:, META_G0:META_G0 + 1]
    g1 = meta[:, META_G1:META_G1 + 1]
    y0 = ybuf[slot, 0].reshape(tm, D_MODEL)
    y1 = ybuf[slot, 1].reshape(tm, D_MODEL)
    h = h_ref[...] + (g0 * y0 + g1 * y1)
    out = _rmsnorm(h, g_ref[...])

    @pl.when(i < prompt_tiles)
    def _():
        op_ref[...] = out

    @pl.when(i >= prompt_tiles)
    def _():
        os_ref[...] = out


def _combine(pos, h_all, meta_all, ys, gf, n_p):
    n_tok = h_all.shape[0]
    tm = TM_OUT
    assert n_p % tm == 0 and n_tok % tm == 0 and 0 < n_p < n_tok
    tp = n_p // tm
    return pl.pallas_call(
        functools.partial(_combine_body, n_tok=n_tok, prompt_tiles=tp),
        grid_spec=pltpu.PrefetchScalarGridSpec(
            num_scalar_prefetch=1,
            grid=(n_tok // tm,),
            in_specs=[pl.BlockSpec((tm, D_MODEL), lambda i, pos: (i, 0)),
                      pl.BlockSpec((tm, LANES), lambda i, pos: (i, 0)),
                      pl.BlockSpec((1, D_MODEL), lambda i, pos: (0, 0)),
                      pl.BlockSpec(memory_space=pl.ANY), pl.BlockSpec(memory_space=pl.ANY)],
            out_specs=[pl.BlockSpec((tm, D_MODEL), lambda i, pos: (jnp.minimum(i, tp - 1), 0)),
                       pl.BlockSpec((tm, D_MODEL), lambda i, pos: (jnp.maximum(i - tp, 0), 0))],
            scratch_shapes=[pltpu.VMEM((2, 2, tm // SUBLANES, SUBLANES, ys.shape[1]), ys.dtype),
                            pltpu.SemaphoreType.DMA((2,))],
        ),
        out_shape=[jax.ShapeDtypeStruct((n_p, D_MODEL), F32), jax.ShapeDtypeStruct((n_tok - n_p, D_MODEL), F32)],
        compiler_params=_params(),
        name="combine",
    )(pos, h_all, meta_all, gf, ys, ys.reshape(ys.shape[0] // SUBLANES, SUBLANES, ys.shape[1]))


def _routing_tables(counts, n_tok):
    n_tiles = (2 * n_tok) // TR + N_EXPERTS
    cnt = counts[ROUTER_LANE0:ROUTER_LANE0 + N_EXPERTS].astype(jnp.int32)
    tiles_e = (cnt + TR - 1) // TR
    tile_end = jnp.cumsum(tiles_e)
    tile_start = tile_end - tiles_e
    off_rows = jnp.pad((tile_start * TR).astype(F32), (ROUTER_LANE0, ROUTER_ROWS - ROUTER_LANE0 - N_EXPERTS))
    off_rows = jnp.broadcast_to(off_rows[:, None], (ROUTER_ROWS, LANES))
    ts = jnp.concatenate([tile_start, tile_end[-1:]]).astype(jnp.int32)
    tile = jnp.arange(n_tiles, dtype=jnp.int32)
    owner = (tile[:, None] >= tile_start[None, :]) & (tile[:, None] < tile_end[None, :])
    rows_left = jnp.sum(jnp.where(owner, cnt[None, :] - (tile[:, None] - tile_start[None, :]) * TR, 0), axis=1)
    nv = jnp.clip(rows_left, 0, TR).astype(jnp.int32)
    return off_rows, ts, nv


def kernel(x_prompt, x_sample, cache_win_k, cache_win_v, state_conv, norm1_g, w_in, attn_sink, conv_dw_w, conv_dw_b,
           conv_ln_g, conv_ln_b, conv_pw_w, conv_pw_b, w_out, norm2_g, router_group_w, router_group_b,
           router_expert_w, router_expert_b, expert_w_gate, expert_w_up, expert_w_down, final_norm_g):
    depth = w_in.shape[0]
    assert depth == 1, "single-layer step"
    bp, sp, _ = x_prompt.shape
    assert bp == 1, "one prompt sequence"
    n_seq, t_new, _ = x_sample.shape
    n_p, n_s = bp * sp, n_seq * t_new
    n_tok = n_p + n_s
    l = 0

    row = lambda a: a.reshape(1, -1)
    w_in_bf = w_in[l].astype(BF16)
    w_out_bf = w_out[l].astype(BF16)
    pw_bf = conv_pw_w[l].astype(BF16)
    rw = jnp.concatenate([router_group_w[l], router_expert_w[l]], axis=1)
    rw_bf = jnp.pad(rw, ((0, 0), (0, LANES - rw.shape[1]))).astype(BF16)
    rb = jnp.pad(jnp.concatenate([router_group_b[l], router_expert_b[l]]), (0, LANES - rw.shape[1])).reshape(1, LANES)
    g1, g2, gf = row(norm1_g[l]), row(norm2_g[l]), row(final_norm_g)
    sink = attn_sink[l]
    conv_w = (conv_dw_w[l], row(conv_dw_b[l]), row(conv_ln_g[l]), row(conv_ln_b[l]), pw_bf, row(conv_pw_b[l]))

    xp = x_prompt.reshape(n_p, D_MODEL)
    xs = x_sample.reshape(n_s, D_MODEL)
    cache_axes, cache_axes_back = (0, 1, 3, 4, 2), (0, 1, 4, 2, 3)
    ck = jnp.transpose(cache_win_k, cache_axes)
    cv = jnp.transpose(cache_win_v, cache_axes)
    cs = jnp.transpose(state_conv, (0, 2, 1, 3))

    qp, kp, vp, up = _in_proj(xp, g1, w_in_bf)
    qs, ks, vs, us = _in_proj(xs, g1, w_in_bf)
    mix_p = _mixer_prompt(sink, qp, kp, vp, up, *conv_w)
    mix_s, wk_s, wv_s, wc_s = _mixer_sample(sink, qs, ks, vs, us, ck, cv, cs, *conv_w)

    h_all, meta_all, rec_all, cnt = _out_router(mix_p, xp, mix_s, xs, w_out_bf, g2, rw_bf, rb)

    off_rows, ts, nv = _routing_tables(cnt[:, 0], n_tok)
    pos_rows = _positions(rec_all, off_rows)
    pos = jnp.concatenate([pos_rows[0], pos_rows[1]])
    hs = _dispatch(pos, nv, h_all)
    ys = _moe(ts, hs, g2, expert_w_gate[l], expert_w_up[l], expert_w_down[l])
    y_p, y_s = _combine(pos, h_all, meta_all, ys, gf, n_p)

    kv_shape = (depth, bp, WINDOW, N_KV_HEADS, HEAD_DIM)
    return (y_p.reshape(bp, sp, D_MODEL), y_s.reshape(n_seq, t_new, D_MODEL),
            kp[n_p - WINDOW:].reshape(kv_shape), vp[n_p - WINDOW:].reshape(kv_shape),
            up[n_p - (CONV_W - 1):].reshape(depth, bp, CONV_W - 1, D_CONV),
            jnp.transpose(wk_s, cache_axes_back), jnp.transpose(wv_s, cache_axes_back),
            jnp.transpose(wc_s, (0, 2, 1, 3)))
```

```python
import functools

import jax
import jax.numpy as jnp
from jax import lax
from jax.experimental import pallas as pl
from jax.experimental.pallas import tpu as pltpu

F32 = jnp.float32
BF16 = jnp.bfloat16

D_MODEL = 2048
HEAD_DIM = 64
N_HEADS = 16
N_KV_HEADS = 2
GQA_GROUP = 8
KV_DIM = N_KV_HEADS * HEAD_DIM
D_ATTN = N_HEADS * HEAD_DIM
D_CONV = D_MODEL - D_ATTN
WINDOW = 128
CONV_W = 31
D_IN = D_ATTN + 2 * KV_DIM + 2 * D_CONV
N_GROUPS = 4
EXPERTS_PER_GROUP = 8
N_EXPERTS = N_GROUPS * EXPERTS_PER_GROUP
D_EXPERT = 256
RMS_EPS = 1e-6
LN_EPS = 1e-5

LANES = 128
SUBLANES = 8
ROUTER_LANE0 = N_GROUPS
VMEM_LIMIT = 56 * 1024 * 1024

TM_PROJ = 512
TM_MIX = 256
SEQ_TILE = 8
TR = 128
TM_OUT = 256

_NT = (((1,), (1,)), ((), ()))


def _params(n_axes=1):
    return pltpu.CompilerParams(dimension_semantics=("arbitrary",) * n_axes, vmem_limit_bytes=VMEM_LIMIT)


def _resident(shape):
    return pl.BlockSpec(shape, lambda *_: (0,) * len(shape), pipeline_mode=pl.Buffered(1))


def _rmsnorm(x, g):
    ms = jnp.mean(x * x, axis=-1, keepdims=True)
    return x * lax.rsqrt(ms + RMS_EPS) * g


def _dot(a, b):
    return jnp.dot(a, b, preferred_element_type=F32)


def _dot_nt(a, b):
    return lax.dot_general(a, b, _NT, preferred_element_type=F32)


def _in_proj_body(x_ref, g_ref, w_ref, q_ref, k_ref, v_ref, u_ref):
    xn = _rmsnorm(x_ref[...], g_ref[...]).astype(BF16)
    q_ref[...] = (_dot(xn, w_ref[:, :D_ATTN]) * (HEAD_DIM ** -0.5)).astype(BF16)
    kv = _dot(xn, w_ref[:, D_ATTN:D_ATTN + 2 * KV_DIM])
    k_ref[...] = kv[:, :KV_DIM]
    v_ref[...] = kv[:, KV_DIM:]
    c0 = D_ATTN + 2 * KV_DIM
    cw = 256
    for j in range(D_CONV // cw):
        a = _dot(xn, w_ref[:, c0 + j * cw:c0 + (j + 1) * cw])
        b = _dot(xn, w_ref[:, c0 + D_CONV + j * cw:c0 + D_CONV + (j + 1) * cw])
        u_ref[:, j * cw:(j + 1) * cw] = a * jax.nn.sigmoid(b)


def _in_proj(x, g, w_bf):
    n = x.shape[0]
    tm = min(TM_PROJ, n)
    row = lambda w: pl.BlockSpec((tm, w), lambda i: (i, 0))
    return pl.pallas_call(
        _in_proj_body,
        grid=(n // tm,),
        in_specs=[row(D_MODEL), _resident((1, D_MODEL)), _resident((D_MODEL, D_IN))],
        out_specs=[row(D_ATTN), row(KV_DIM), row(KV_DIM), row(D_CONV)],
        out_shape=[jax.ShapeDtypeStruct((n, D_ATTN), BF16), jax.ShapeDtypeStruct((n, KV_DIM), F32),
                   jax.ShapeDtypeStruct((n, KV_DIM), F32), jax.ShapeDtypeStruct((n, D_CONV), F32)],
        compiler_params=_params(),
        name="in_proj",
    )(x, g, w_bf)


def _attend(q, parts, sink_ref, kv_head, keys_on_lanes=False):
    t = q.shape[0]
    score, weigh = (_dot, _dot_nt) if keys_on_lanes else (_dot_nt, _dot)
    qs = jnp.concatenate([q[:, g * HEAD_DIM:(g + 1) * HEAD_DIM] for g in range(GQA_GROUP)], axis=0)
    scores = [score(qs, kk) for kk, _, _ in parts]
    hidden = [jnp.where(mask, 0.0, -jnp.inf) for _, _, mask in parts]
    ps = [[] for _ in parts]
    inv = []
    for g in range(GQA_GROUP):
        sg = [s[g * t:(g + 1) * t] + neg for s, neg in zip(scores, hidden)]
        sk = sink_ref[kv_head * GQA_GROUP + g]
        m = sk
        for x in sg:
            m = jnp.maximum(jnp.max(x, axis=-1, keepdims=True), m)
        den = jnp.exp(sk - m)
        for k, x in enumerate(sg):
            p = jnp.exp(x - m)
            den = den + jnp.sum(p, axis=-1, keepdims=True)
            ps[k].append(p.astype(BF16))
        inv.append(1.0 / den)
    o = None
    for k, (_, vv, _) in enumerate(parts):
        ok = weigh(jnp.concatenate(ps[k], axis=0), vv)
        o = ok if o is None else o + ok
    return jnp.concatenate([o[g * t:(g + 1) * t] * inv[g] for g in range(GQA_GROUP)], axis=1)


def _conv_tail(y, dwb_ref, lng_ref, lnb_ref, pw_ref, pwb_ref):
    y = y + dwb_ref[...]
    mu = jnp.mean(y, axis=-1, keepdims=True)
    yc = y - mu
    yn = yc * lax.rsqrt(jnp.mean(yc * yc, axis=-1, keepdims=True) + LN_EPS)
    yn = yn * lng_ref[...] + lnb_ref[...]
    act = yn * jax.nn.sigmoid(yn)
    return _dot(act.astype(BF16), pw_ref[...]) + pwb_ref[...]


CONV_PAD = 32
CONV_ROWS = 128
CONV_COLS = 128


def _mixer_prompt_body(sink_ref, q_ref, k_ref, v_ref, u_ref, dww_ref, dwb_ref, lng_ref, lnb_ref, pw_ref, pwb_ref,
                       mix_ref, kprev, vprev, uext, ushift, ybuf):
    i = pl.program_id(0)
    tm = q_ref.shape[0]

    @pl.when(i == 0)
    def _():
        kprev[...] = jnp.zeros_like(kprev)
        vprev[...] = jnp.zeros_like(vprev)
        uext[0:CONV_PAD, :] = jnp.zeros((CONV_PAD, D_CONV), F32)

    qi = lax.broadcasted_iota(jnp.int32, (WINDOW, 2 * WINDOW), 0)
    kj = lax.broadcasted_iota(jnp.int32, (WINDOW, 2 * WINDOW), 1)
    diff = qi + WINDOW - kj
    band = (diff >= 0) & (diff <= WINDOW)
    kp, vp = kprev[...], vprev[...]
    for b in range(tm // WINDOW):
        rows = slice(b * WINDOW, (b + 1) * WINDOW)
        kb = k_ref[rows, :].astype(BF16)
        vb = v_ref[rows, :].astype(BF16)
        kk = jnp.concatenate([kp, kb], axis=0)
        vv = jnp.concatenate([vp, vb], axis=0)
        if b == 0:
            mask = band & (kj >= jnp.where(i > 0, 0, WINDOW))
        else:
            mask = band
        for h in range(N_KV_HEADS):
            cols = slice(h * HEAD_DIM, (h + 1) * HEAD_DIM)
            hq = slice(h * GQA_GROUP * HEAD_DIM, (h + 1) * GQA_GROUP * HEAD_DIM)
            o = _attend(q_ref[rows, hq], [(kk[:, cols], vv[:, cols], mask)], sink_ref, h)
            mix_ref[rows, hq] = o.astype(BF16)
        kp, vp = kb, vb
    kprev[...] = kp
    vprev[...] = vp

    uext[CONV_PAD:CONV_PAD + tm, :] = u_ref[...]
    n_shift_rows = ushift.shape[1]
    for s in range(1, SUBLANES):
        ushift[s - 1] = uext[s:s + n_shift_rows, :]
    off = CONV_PAD - (CONV_W - 1)
    for r in range(tm // CONV_ROWS):
        for c in range(D_CONV // CONV_COLS):
            cs = slice(c * CONV_COLS, (c + 1) * CONV_COLS)
            acc = jnp.zeros((CONV_ROWS, CONV_COLS), F32)
            for j in range(CONV_W):
                a, s = divmod(off + j, SUBLANES)
                r0 = r * CONV_ROWS + a * SUBLANES
                src = uext[r0:r0 + CONV_ROWS, cs] if s == 0 else ushift[s - 1, r0:r0 + CONV_ROWS, cs]
                acc = acc + dww_ref[j:j + 1, cs] * src
            ybuf[r * CONV_ROWS:(r + 1) * CONV_ROWS, cs] = acc
    uext[0:CONV_PAD, :] = uext[tm:tm + CONV_PAD, :]
    conv = _conv_tail(ybuf[...], dwb_ref, lng_ref, lnb_ref, pw_ref, pwb_ref)
    mix_ref[:, D_ATTN:] = conv.astype(BF16)


def _mixer_prompt(sink, q, k, v, u, dww, dwb, lng, lnb, pw_bf, pwb):
    n = q.shape[0]
    tm = TM_MIX
    row = lambda w: pl.BlockSpec((tm, w), lambda i, s: (i, 0))
    res = lambda shape: pl.BlockSpec(shape, lambda i, s: (0,) * len(shape), pipeline_mode=pl.Buffered(1))
    return pl.pallas_call(
        _mixer_prompt_body,
        grid_spec=pltpu.PrefetchScalarGridSpec(
            num_scalar_prefetch=1,
            grid=(n // tm,),
            in_specs=[row(D_ATTN), row(KV_DIM), row(KV_DIM), row(D_CONV), res((CONV_W, D_CONV)), res((1, D_CONV)),
                      res((1, D_CONV)), res((1, D_CONV)), res((D_CONV, D_CONV)), res((1, D_CONV))],
            out_specs=row(D_MODEL),
            scratch_shapes=[pltpu.VMEM((WINDOW, KV_DIM), BF16), pltpu.VMEM((WINDOW, KV_DIM), BF16),
                            pltpu.VMEM((CONV_PAD + tm, D_CONV), F32),
                            pltpu.VMEM((SUBLANES - 1, CONV_PAD - SUBLANES + tm, D_CONV), F32),
                            pltpu.VMEM((tm, D_CONV), F32)],
        ),
        out_shape=jax.ShapeDtypeStruct((n, D_MODEL), BF16),
        compiler_params=_params(),
        name="mixer_prompt",
    )(sink, q, k, v, u, dww, dwb, lng, lnb, pw_bf, pwb)


def _mixer_sample_body(sink_ref, q_ref, k_ref, v_ref, u_ref, ck_ref, cv_ref, cs_ref, dww_ref, dwb_ref, lng_ref,
                       lnb_ref, pw_ref, pwb_ref, mix_ref, wk_ref, wv_ref, wc_ref, usel, ybuf):
    nt = q_ref.shape[0]
    sb = ck_ref.shape[0]
    t_new = nt // sb
    n_cache = sb * WINDOW
    n_state = CONV_W - 1

    pad = jnp.zeros((LANES - nt, KV_DIM), F32)
    kn_t = jnp.concatenate([k_ref[...], pad], axis=0).T
    vn_t = jnp.concatenate([v_ref[...], pad], axis=0).T

    lt, lw = t_new.bit_length() - 1, WINDOW.bit_length() - 1
    qr = lax.broadcasted_iota(jnp.int32, (nt, n_cache), 0)
    cc = lax.broadcasted_iota(jnp.int32, (nt, n_cache), 1)
    mask_c = ((cc >> lw) == (qr >> lt)) & ((cc & (WINDOW - 1)) >= (qr & (t_new - 1)))
    qr = lax.broadcasted_iota(jnp.int32, (nt, LANES), 0)
    cn = lax.broadcasted_iota(jnp.int32, (nt, LANES), 1)
    mask_n = ((cn >> lt) == (qr >> lt)) & ((cn & (t_new - 1)) <= (qr & (t_new - 1)))
    for h in range(N_KV_HEADS):
        dims = slice(h * HEAD_DIM, (h + 1) * HEAD_DIM)
        hq = slice(h * GQA_GROUP * HEAD_DIM, (h + 1) * GQA_GROUP * HEAD_DIM)
        kc = jnp.concatenate([ck_ref[s, h] for s in range(sb)], axis=1).astype(BF16)
        vc = jnp.concatenate([cv_ref[s, h] for s in range(sb)], axis=1).astype(BF16)
        parts = [(kc, vc, mask_c), (kn_t[dims, :].astype(BF16), vn_t[dims, :].astype(BF16), mask_n)]
        o = _attend(q_ref[:, hq], parts, sink_ref, h, keys_on_lanes=True)
        mix_ref[:, hq] = o.astype(BF16)

    is_new = lax.broadcasted_iota(jnp.int32, (HEAD_DIM, WINDOW), 1) >= WINDOW - t_new
    for s in range(sb):
        shift = (WINDOW - t_new - s * t_new) % LANES
        kn_s = pltpu.roll(kn_t, shift=shift, axis=1)
        vn_s = pltpu.roll(vn_t, shift=shift, axis=1)
        for h in range(N_KV_HEADS):
            dims = slice(h * HEAD_DIM, (h + 1) * HEAD_DIM)
            old_k = pltpu.roll(ck_ref[s, h], shift=WINDOW - t_new, axis=1)
            old_v = pltpu.roll(cv_ref[s, h], shift=WINDOW - t_new, axis=1)
            wk_ref[s, h] = jnp.where(is_new, kn_s[dims, :], old_k)
            wv_ref[s, h] = jnp.where(is_new, vn_s[dims, :], old_v)

    for s in range(sb):
        for t in range(t_new):
            usel[t, s:s + 1, :] = u_ref[s * t_new + t:s * t_new + t + 1, :]
    u_new = [usel[t] for t in range(t_new)]
    for t in range(t_new):
        acc = jnp.zeros((sb, D_CONV), F32)
        for r in range(t, n_state):
            acc = acc + dww_ref[r - t:r - t + 1, :] * cs_ref[r]
        for t2 in range(t + 1):
            j = n_state - t + t2
            acc = acc + dww_ref[j:j + 1, :] * u_new[t2]
        for s in range(sb):
            ybuf[s * t_new + t:s * t_new + t + 1, :] = acc[s:s + 1, :]
        wc_ref[n_state - t_new + t] = u_new[t]
    wc_ref[0:n_state - t_new] = cs_ref[t_new:n_state]
    conv = _conv_tail(ybuf[...], dwb_ref, lng_ref, lnb_ref, pw_ref, pwb_ref)
    mix_ref[:, D_ATTN:] = conv.astype(BF16)


def _mixer_sample(sink, q, k, v, u, ck, cv, cs, dww, dwb, lng, lnb, pw_bf, pwb):
    n = q.shape[0]
    n_seq = ck.shape[1]
    t_new = n // n_seq
    sb = SEQ_TILE
    nt = sb * t_new
    n_state = CONV_W - 1
    row = lambda w: pl.BlockSpec((nt, w), lambda i, s: (i, 0))
    cache = pl.BlockSpec((None, sb, N_KV_HEADS, HEAD_DIM, WINDOW), lambda i, s: (0, i, 0, 0, 0))
    state = pl.BlockSpec((None, n_state, sb, D_CONV), lambda i, s: (0, 0, i, 0))
    res = lambda shape: pl.BlockSpec(shape, lambda i, s: (0,) * len(shape), pipeline_mode=pl.Buffered(1))
    return pl.pallas_call(
        _mixer_sample_body,
        grid_spec=pltpu.PrefetchScalarGridSpec(
            num_scalar_prefetch=1,
            grid=(n_seq // sb,),
            in_specs=[row(D_ATTN), row(KV_DIM), row(KV_DIM), row(D_CONV), cache, cache, state,
                      res((CONV_W, D_CONV)), res((1, D_CONV)), res((1, D_CONV)), res((1, D_CONV)),
                      res((D_CONV, D_CONV)), res((1, D_CONV))],
            out_specs=[row(D_MODEL), cache, cache, state],
            scratch_shapes=[pltpu.VMEM((t_new, sb, D_CONV), F32), pltpu.VMEM((nt, D_CONV), F32)],
        ),
        out_shape=[jax.ShapeDtypeStruct((n, D_MODEL), BF16), jax.ShapeDtypeStruct(ck.shape, F32),
                   jax.ShapeDtypeStruct(cv.shape, F32), jax.ShapeDtypeStruct(cs.shape, F32)],
        compiler_params=_params(),
        name="mixer_sample",
    )(sink, q, k, v, u, ck, cv, cs, dww, dwb, lng, lnb, pw_bf, pwb)


META_E0, META_E1, META_G0, META_G1, META_R0, META_R1 = range(6)


ROUTER_ROWS = 40


def _out_router_body(mixp_ref, xp_ref, mixs_ref, xs_ref, wout_ref, g2_ref, rw_ref, rb_ref, h_ref, meta_ref, rec_ref,
                     cnt_ref, carry, earlier, *, prompt_tiles):
    i = pl.program_id(0)
    tm = xp_ref.shape[0]

    @pl.when(i == 0)
    def _():
        carry[...] = jnp.zeros_like(carry)
        ti = lax.broadcasted_iota(jnp.int32, (tm, tm), 0)
        tj = lax.broadcasted_iota(jnp.int32, (tm, tm), 1)
        earlier[...] = jnp.where(ti < tj, 1.0, 0.0).astype(BF16)

    is_prompt = i < prompt_tiles
    x = jnp.where(is_prompt, xp_ref[...], xs_ref[...])
    mix = jnp.where(is_prompt, mixp_ref[...], mixs_ref[...])
    h = x + _dot(mix, wout_ref[...])
    h_ref[...] = h
    xn = _rmsnorm(h, g2_ref[...]).astype(BF16)
    logits = _dot(xn, rw_ref[...]) + rb_ref[...]
    lt = logits.T[:ROUTER_ROWS, :]
    row = lax.broadcasted_iota(jnp.int32, lt.shape, 0).astype(F32)
    first = lambda cond: jnp.min(jnp.where(cond, row, float(LANES)), axis=0, keepdims=True)

    gl = jnp.where(row < N_GROUPS, lt, -jnp.inf)
    gmax = jnp.max(gl, axis=0, keepdims=True)
    gidx = first(gl == gmax)
    gval = 1.0 / jnp.sum(jnp.exp(gl - gmax), axis=0, keepdims=True)

    lo = ROUTER_LANE0 + EXPERTS_PER_GROUP * gidx
    el = jnp.where((row >= lo) & (row < lo + EXPERTS_PER_GROUP), lt, -jnp.inf)
    m1 = jnp.max(el, axis=0, keepdims=True)
    i1 = first(el == m1)
    el2 = jnp.where(row == i1, -jnp.inf, el)
    m2 = jnp.max(el2, axis=0, keepdims=True)
    i2 = first(el2 == m2)
    r = jnp.exp(m2 - m1)
    g0 = gval * (1.0 / (1.0 + r))
    g1 = gval * (r / (1.0 + r))

    sel0, sel1 = row == i1, row == i2
    onehot = jnp.where(sel0 | sel1, 1.0, 0.0)
    before = _dot(onehot.astype(BF16), earlier[...]) + carry[0:ROUTER_ROWS, 0:1]
    r0 = jnp.sum(jnp.where(sel0, before, 0.0), axis=0, keepdims=True)
    r1 = jnp.sum(jnp.where(sel1, before, 0.0), axis=0, keepdims=True)
    carry[0:ROUTER_ROWS, :] = carry[0:ROUTER_ROWS, :] + jnp.sum(onehot, axis=1, keepdims=True)
    cnt_ref[...] = carry[...]

    row8 = lax.broadcasted_iota(jnp.int32, (SUBLANES, tm), 0)
    rec = jnp.zeros((SUBLANES, tm), F32)
    for slot, val in ((META_E0, i1 - ROUTER_LANE0), (META_E1, i2 - ROUTER_LANE0), (META_G0, g0), (META_G1, g1),
                      (META_R0, r0), (META_R1, r1)):
        rec = jnp.where(row8 == slot, val, rec)
    rec_ref[...] = rec
    meta_ref[...] = jnp.concatenate([rec, jnp.zeros((LANES - SUBLANES, tm), F32)], axis=0).T


def _out_router(mix_p, x_p, mix_s, x_s, wout_bf, g2, rw_bf, rb):
    n_p, n_s = x_p.shape[0], x_s.shape[0]
    tm = TM_PROJ
    assert n_p % tm == 0 and n_s % tm == 0
    tp, ts = n_p // tm, n_s // tm
    n = n_p + n_s
    prow = lambda w: pl.BlockSpec((tm, w), lambda i: (jnp.minimum(i, tp - 1), 0))
    srow = lambda w: pl.BlockSpec((tm, w), lambda i: (jnp.maximum(i - tp, 0), 0))
    row = lambda w: pl.BlockSpec((tm, w), lambda i: (i, 0))
    return pl.pallas_call(
        functools.partial(_out_router_body, prompt_tiles=tp),
        grid=(tp + ts,),
        in_specs=[prow(D_MODEL), prow(D_MODEL), srow(D_MODEL), srow(D_MODEL), _resident((D_MODEL, D_MODEL)),
                  _resident((1, D_MODEL)), _resident((D_MODEL, LANES)), _resident((1, LANES))],
        out_specs=[row(D_MODEL), row(LANES), pl.BlockSpec((SUBLANES, tm), lambda i: (0, i)),
                   pl.BlockSpec((LANES, LANES), lambda i: (0, 0))],
        out_shape=[jax.ShapeDtypeStruct((n, D_MODEL), F32), jax.ShapeDtypeStruct((n, LANES), F32),
                   jax.ShapeDtypeStruct((SUBLANES, n), F32), jax.ShapeDtypeStruct((LANES, LANES), F32)],
        scratch_shapes=[pltpu.VMEM((LANES, LANES), F32), pltpu.VMEM((tm, tm), BF16)],
        compiler_params=_params(),
        name="out_router",
    )(mix_p, x_p, mix_s, x_s, wout_bf, g2, rw_bf, rb)


def _positions_body(rec_ref, off_ref, pos_ref):
    rec = rec_ref[...]
    tm = rec.shape[1]
    row = lax.broadcasted_iota(jnp.int32, (ROUTER_ROWS, tm), 0).astype(F32)
    row8 = lax.broadcasted_iota(jnp.int32, rec.shape, 0)
    off = off_ref[:, 0:1]
    pos = jnp.zeros(rec.shape, F32)
    for slot, (e_row, r_row) in enumerate(((META_E0, META_R0), (META_E1, META_R1))):
        e = rec[e_row:e_row + 1, :] + ROUTER_LANE0
        seg = jnp.sum(jnp.where(row == e, off, 0.0), axis=0, keepdims=True)
        pos = jnp.where(row8 == slot, seg + rec[r_row:r_row + 1, :], pos)
    pos_ref[...] = pos.astype(jnp.int32)


def _positions(rec_all, off_rows):
    n = rec_all.shape[1]
    tm = n
    return pl.pallas_call(
        _positions_body,
        grid=(n // tm,),
        in_specs=[pl.BlockSpec((SUBLANES, tm), lambda i: (0, i)), _resident((ROUTER_ROWS, LANES))],
        out_specs=pl.BlockSpec((SUBLANES, tm), lambda i: (0, i)),
        out_shape=jax.ShapeDtypeStruct((SUBLANES, n), jnp.int32),
        compiler_params=_params(),
        name="positions",
    )(rec_all, off_rows)


N_LOAD_SLOTS = 3


def _pow2_sizes(n):
    return [1 << b for b in range(n.bit_length() - 1, -1, -1)]


def _dispatch_body(pos_ref, nv_ref, h_ref, hs_ref, hbuf, zbuf, lsem, ssem, zsem, *, n_tok, tm):
    i = pl.program_id(0)
    n_steps = pl.num_programs(0)
    n_tiles = nv_ref.shape[0]

    groups = tm // SUBLANES

    def load(t, slot, sem=lsem):
        return pltpu.make_async_copy(h_ref.at[pl.ds(t * groups, groups)], hbuf.at[slot], sem.at[slot])

    def clear_unowned(start):
        def per_tile(t, c):
            nv = nv_ref[t]
            z = TR - nv
            row = t * TR + nv
            head = z & (SUBLANES - 1)
            for k in range(SUBLANES - 1):
                @pl.when(k < head)
                def _(k=k):
                    cp = pltpu.make_async_copy(zbuf.at[pl.ds(0, 1), :], hs_ref.at[pl.ds(row + k, 1), :], zsem)
                    cp.start() if start else cp.wait()
            row = row + head
            for size in _pow2_sizes(TR):
                if size < SUBLANES:
                    break
                @pl.when((z & size) != 0)
                def _(row=row, size=size):
                    dst = hs_ref.at[pl.ds(pl.multiple_of(row, SUBLANES), size), :]
                    cp = pltpu.make_async_copy(zbuf.at[pl.ds(0, size), :], dst, zsem)
                    cp.start() if start else cp.wait()
                row = row + (z & size)
            return c
        lax.fori_loop(0, n_tiles, per_tile, 0)

    def scatter_wait(slot):
        for _ in range(2):
            load(0, slot, ssem).wait()

    @pl.when(i == 0)
    def _():
        zbuf[...] = jnp.zeros_like(zbuf)
        clear_unowned(True)
        load(0, 0).start()

    @pl.when(i + 1 < n_steps)
    def _():
        load(i + 1, (i + 1) % N_LOAD_SLOTS).start()

    slot = i % N_LOAD_SLOTS
    load(i, slot).wait()

    def issue(jb, c):
        for u in range(SUBLANES):
            for s in range(2):
                p = pos_ref[s * n_tok + i * tm + jb * SUBLANES + u]
                pltpu.make_async_copy(hbuf.at[slot, jb, pl.ds(u, 1), :], hs_ref.at[pl.ds(p, 1), :],
                                      ssem.at[slot]).start()
        return c
    lax.fori_loop(0, groups, issue, 0)

    @pl.when(i > 0)
    def _():
        scatter_wait((i + N_LOAD_SLOTS - 1) % N_LOAD_SLOTS)

    @pl.when(i == n_steps - 1)
    def _():
        scatter_wait(slot)
        clear_unowned(False)


def _dispatch(pos, nv, rows):
    n_tok, width = rows.shape
    n_tiles = nv.shape[0]
    tm = TM_PROJ
    assert n_tok % tm == 0
    return pl.pallas_call(
        functools.partial(_dispatch_body, n_tok=n_tok, tm=tm),
        grid_spec=pltpu.PrefetchScalarGridSpec(
            num_scalar_prefetch=2,
            grid=(n_tok // tm,),
            in_specs=[pl.BlockSpec(memory_space=pl.ANY)],
            out_specs=pl.BlockSpec(memory_space=pl.ANY),
            scratch_shapes=[pltpu.VMEM((N_LOAD_SLOTS, tm // SUBLANES, SUBLANES, width), rows.dtype),
                            pltpu.VMEM((TR, width), rows.dtype),
                            pltpu.SemaphoreType.DMA((N_LOAD_SLOTS,)), pltpu.SemaphoreType.DMA((N_LOAD_SLOTS,)),
                            pltpu.SemaphoreType.DMA(())],
        ),
        out_shape=jax.ShapeDtypeStruct((n_tiles * TR, width), rows.dtype),
        compiler_params=_params(),
        name="dispatch",
    )(pos, nv, rows.reshape(n_tok // SUBLANES, SUBLANES, width))


N_MOE_LOADS = 4


def _moe_body(ts_ref, hs_ref, g2_ref, wg_ref, wu_ref, wd_ref, ys_ref, xbuf, obuf, wg_bf, wu_bf, wd_bf, lsem, ssem):
    e = pl.program_id(0)
    n_tiles = ys_ref.shape[0] // TR
    n_used = ts_ref[N_EXPERTS]
    t0, t1 = ts_ref[e], ts_ref[e + 1]

    def load(g, slot):
        return pltpu.make_async_copy(hs_ref.at[pl.ds(pl.multiple_of(g * TR, TR), TR), :], xbuf.at[slot], lsem.at[slot])

    def store(g, slot):
        return pltpu.make_async_copy(obuf.at[slot], ys_ref.at[pl.ds(pl.multiple_of(g * TR, TR), TR), :], ssem.at[slot])

    @pl.when(e == 0)
    def _():
        for g in range(N_MOE_LOADS - 1):
            @pl.when(g < n_used)
            def _(g=g):
                load(g, g).start()

    @pl.when(t1 > t0)
    def _():
        wg_bf[...] = wg_ref[0].astype(BF16)
        wu_bf[...] = wu_ref[0].astype(BF16)
        wd_bf[...] = wd_ref[0].astype(BF16)

    def tile(g, c):
        slot = g % 2
        xslot = g % N_MOE_LOADS
        ahead = g + N_MOE_LOADS - 1

        @pl.when(ahead < n_used)
        def _():
            load(ahead, ahead % N_MOE_LOADS).start()

        load(g, xslot).wait()
        xn = _rmsnorm(xbuf[xslot], g2_ref[...]).astype(BF16)
        hg = _dot(xn, wg_bf[...])
        hu = _dot(xn, wu_bf[...])
        hid = (hg * jax.nn.sigmoid(hg) * hu).astype(BF16)
        y = _dot(hid, wd_bf[...])

        @pl.when(g >= 2)
        def _():
            store(g - 2, slot).wait()

        obuf[slot] = y
        store(g, slot).start()
        return c

    lax.fori_loop(t0, t1, tile, 0)

    @pl.when(e == pl.num_programs(0) - 1)
    def _():
        @pl.when(n_used >= 2)
        def _():
            store(n_used - 2, n_used % 2).wait()
        store(n_used - 1, (n_used - 1) % 2).wait()
        obuf[0] = jnp.zeros(obuf.shape[1:], obuf.dtype)

        def clear(g, c):
            store(g, 0).start()
            return c
        lax.fori_loop(n_used, n_tiles, clear, 0)

        def drain(g, c):
            store(g, 0).wait()
            return c
        lax.fori_loop(n_used, n_tiles, drain, 0)


def _moe(ts, hs, g2, wg, wu, wd):
    wspec = lambda shape: pl.BlockSpec((1,) + shape, lambda e, ts: (e, 0, 0))
    tile_buf = lambda slots: pltpu.VMEM((slots, TR, hs.shape[1]), hs.dtype)
    return pl.pallas_call(
        _moe_body,
        grid_spec=pltpu.PrefetchScalarGridSpec(
            num_scalar_prefetch=1,
            grid=(N_EXPERTS,),
            in_specs=[pl.BlockSpec(memory_space=pl.ANY),
                      pl.BlockSpec((1, D_MODEL), lambda e, ts: (0, 0)),
                      wspec((D_MODEL, D_EXPERT)), wspec((D_MODEL, D_EXPERT)), wspec((D_EXPERT, D_MODEL))],
            out_specs=pl.BlockSpec(memory_space=pl.ANY),
            scratch_shapes=[tile_buf(N_MOE_LOADS), tile_buf(2),
                            pltpu.VMEM((D_MODEL, D_EXPERT), BF16), pltpu.VMEM((D_MODEL, D_EXPERT), BF16),
                            pltpu.VMEM((D_EXPERT, D_MODEL), BF16),
                            pltpu.SemaphoreType.DMA((N_MOE_LOADS,)), pltpu.SemaphoreType.DMA((2,))],
        ),
        out_shape=jax.ShapeDtypeStruct(hs.shape, hs.dtype),
        compiler_params=_params(),
        name="moe",
    )(ts, hs, g2, wg, wu, wd)


COMBINE_CHUNK = 64


def _combine_body(pos_ref, h_ref, meta_ref, g_ref, ys_ref, ys_grouped_ref, o_ref, ybuf, sem, *, n_tok, row0):
    i = pl.program_id(0)
    n_steps = pl.num_programs(0)
    tm = h_ref.shape[0]

    groups = tm // SUBLANES

    def issue(t, slot, jb):
        for u in range(SUBLANES):
            for s in range(2):
                p = pos_ref[s * n_tok + row0 + t * tm + jb * SUBLANES + u]
                pltpu.make_async_copy(ys_ref.at[pl.ds(p, 1), :], ybuf.at[slot, s, jb, pl.ds(u, 1), :],
                                      sem.at[slot]).start()

    @pl.when(i == 0)
    def _():
        lax.fori_loop(0, groups, lambda jb, c: (issue(0, 0, jb), c)[1], 0)

    slot = i % 2
    for s in range(2):
        pltpu.make_async_copy(ys_grouped_ref.at[pl.ds(0, groups)], ybuf.at[slot, s], sem.at[slot]).wait()

    chunk = COMBINE_CHUNK // SUBLANES

    def finish(c):
        rows = pl.ds(pl.multiple_of(c * COMBINE_CHUNK, COMBINE_CHUNK), COMBINE_CHUNK)
        meta = meta_ref[rows, :]
        g0 = meta[:, META_G0:META_G0 + 1]
        g1 = meta[:, META_G1:META_G1 + 1]
        grp = pl.ds(c * chunk, chunk)
        y0 = ybuf[slot, 0, grp].reshape(COMBINE_CHUNK, D_MODEL)
        y1 = ybuf[slot, 1, grp].reshape(COMBINE_CHUNK, D_MODEL)
        h = h_ref[rows, :] + (g0 * y0 + g1 * y1)
        o_ref[rows, :] = _rmsnorm(h, g_ref[...])

    @pl.when(i + 1 < n_steps)
    def _():
        def both(c, carry):
            for k in range(chunk):
                issue(i + 1, 1 - slot, c * chunk + k)
            finish(c)
            return carry
        lax.fori_loop(0, groups // chunk, both, 0)

    @pl.when(i + 1 >= n_steps)
    def _():
        lax.fori_loop(0, groups // chunk, lambda c, carry: (finish(c), carry)[1], 0)


def _combine(pos, h_all, meta_all, ys, gf, row0, n):
    n_tok = h_all.shape[0]
    tm = min(TM_OUT, n)
    assert row0 % tm == 0 and n % tm == 0
    b0 = row0 // tm
    return pl.pallas_call(
        functools.partial(_combine_body, n_tok=n_tok, row0=row0),
        grid_spec=pltpu.PrefetchScalarGridSpec(
            num_scalar_prefetch=1,
            grid=(n // tm,),
            in_specs=[pl.BlockSpec((tm, D_MODEL), lambda i, pos: (b0 + i, 0)),
                      pl.BlockSpec((tm, LANES), lambda i, pos: (b0 + i, 0)),
                      pl.BlockSpec((1, D_MODEL), lambda i, pos: (0, 0)),
                      pl.BlockSpec(memory_space=pl.ANY), pl.BlockSpec(memory_space=pl.ANY)],
            out_specs=pl.BlockSpec((tm, D_MODEL), lambda i, pos: (i, 0)),
            scratch_shapes=[pltpu.VMEM((2, 2, tm // SUBLANES, SUBLANES, ys.shape[1]), ys.dtype),
                            pltpu.SemaphoreType.DMA((2,))],
        ),
        out_shape=jax.ShapeDtypeStruct((n, D_MODEL), F32),
        compiler_params=_params(),
        name="combine",
    )(pos, h_all, meta_all, gf, ys, ys.reshape(ys.shape[0] // SUBLANES, SUBLANES, ys.shape[1]))


def _routing_tables(counts, n_tok):
    n_tiles = (2 * n_tok) // TR + N_EXPERTS
    cnt = counts[ROUTER_LANE0:ROUTER_LANE0 + N_EXPERTS].astype(jnp.int32)
    tiles_e = (cnt + TR - 1) // TR
    tile_end = jnp.cumsum(tiles_e)
    tile_start = tile_end - tiles_e
    off_rows = jnp.pad((tile_start * TR).astype(F32), (ROUTER_LANE0, ROUTER_ROWS - ROUTER_LANE0 - N_EXPERTS))
    off_rows = jnp.broadcast_to(off_rows[:, None], (ROUTER_ROWS, LANES))
    ts = jnp.concatenate([tile_start, tile_end[-1:]]).astype(jnp.int32)
    tile = jnp.arange(n_tiles, dtype=jnp.int32)
    owner = (tile[:, None] >= tile_start[None, :]) & (tile[:, None] < tile_end[None, :])
    rows_left = jnp.sum(jnp.where(owner, cnt[None, :] - (tile[:, None] - tile_start[None, :]) * TR, 0), axis=1)
    nv = jnp.clip(rows_left, 0, TR).astype(jnp.int32)
    return off_rows, ts, nv


def kernel(x_prompt, x_sample, cache_win_k, cache_win_v, state_conv, norm1_g, w_in, attn_sink, conv_dw_w, conv_dw_b,
           conv_ln_g, conv_ln_b, conv_pw_w, conv_pw_b, w_out, norm2_g, router_group_w, router_group_b,
           router_expert_w, router_expert_b, expert_w_gate, expert_w_up, expert_w_down, final_norm_g):
    depth = w_in.shape[0]
    assert depth == 1, "single-layer step"
    bp, sp, _ = x_prompt.shape
    assert bp == 1, "one prompt sequence"
    n_seq, t_new, _ = x_sample.shape
    n_p, n_s = bp * sp, n_seq * t_new
    n_tok = n_p + n_s
    l = 0

    row = lambda a: a.reshape(1, -1)
    w_in_bf = w_in[l].astype(BF16)
    w_out_bf = w_out[l].astype(BF16)
    pw_bf = conv_pw_w[l].astype(BF16)
    rw = jnp.concatenate([router_group_w[l], router_expert_w[l]], axis=1)
    rw_bf = jnp.pad(rw, ((0, 0), (0, LANES - rw.shape[1]))).astype(BF16)
    rb = jnp.pad(jnp.concatenate([router_group_b[l], router_expert_b[l]]), (0, LANES - rw.shape[1])).reshape(1, LANES)
    g1, g2, gf = row(norm1_g[l]), row(norm2_g[l]), row(final_norm_g)
    sink = attn_sink[l]
    conv_w = (conv_dw_w[l], row(conv_dw_b[l]), row(conv_ln_g[l]), row(conv_ln_b[l]), pw_bf, row(conv_pw_b[l]))

    xp = x_prompt.reshape(n_p, D_MODEL)
    xs = x_sample.reshape(n_s, D_MODEL)
    cache_axes, cache_axes_back = (0, 1, 3, 4, 2), (0, 1, 4, 2, 3)
    ck = jnp.transpose(cache_win_k, cache_axes)
    cv = jnp.transpose(cache_win_v, cache_axes)
    cs = jnp.transpose(state_conv, (0, 2, 1, 3))

    qp, kp, vp, up = _in_proj(xp, g1, w_in_bf)
    qs, ks, vs, us = _in_proj(xs, g1, w_in_bf)
    mix_p = _mixer_prompt(sink, qp, kp, vp, up, *conv_w)
    mix_s, wk_s, wv_s, wc_s = _mixer_sample(sink, qs, ks, vs, us, ck, cv, cs, *conv_w)

    h_all, meta_all, rec_all, cnt = _out_router(mix_p, xp, mix_s, xs, w_out_bf, g2, rw_bf, rb)

    off_rows, ts, nv = _routing_tables(cnt[:, 0], n_tok)
    pos_rows = _positions(rec_all, off_rows)
    pos = jnp.concatenate([pos_rows[0], pos_rows[1]])
    hs = _dispatch(pos, nv, h_all)
    ys = _moe(ts, hs, g2, expert_w_gate[l], expert_w_up[l], expert_w_down[l])
    y_p = _combine(pos, h_all, meta_all, ys, gf, 0, n_p)
    y_s = _combine(pos, h_all, meta_all, ys, gf, n_p, n_s)

    kv_shape = (depth, bp, WINDOW, N_KV_HEADS, HEAD_DIM)
    return (y_p.reshape(bp, sp, D_MODEL), y_s.reshape(n_seq, t_new, D_MODEL),
            kp[n_p - WINDOW:].reshape(kv_shape), vp[n_p - WINDOW:].reshape(kv_shape),
            up[n_p - (CONV_W - 1):].reshape(depth, bp, CONV_W - 1, D_CONV),
            jnp.transpose(wk_s, cache_axes_back), jnp.transpose(wv_s, cache_axes_back),
            jnp.transpose(wc_s, (0, 2, 1, 3)))
```

```python
import functools

import jax
import jax.numpy as jnp
from jax import lax
from jax.experimental import pallas as pl
from jax.experimental.pallas import tpu as pltpu

F32 = jnp.float32
BF16 = jnp.bfloat16

D_MODEL = 2048
HEAD_DIM = 64
N_HEADS = 16
N_KV_HEADS = 2
GQA_GROUP = 8
KV_DIM = N_KV_HEADS * HEAD_DIM
D_ATTN = N_HEADS * HEAD_DIM
D_CONV = D_MODEL - D_ATTN
WINDOW = 128
CONV_W = 31
D_IN = D_ATTN + 2 * KV_DIM + 2 * D_CONV
N_GROUPS = 4
EXPERTS_PER_GROUP = 8
N_EXPERTS = N_GROUPS * EXPERTS_PER_GROUP
D_EXPERT = 256
RMS_EPS = 1e-6
LN_EPS = 1e-5

LANES = 128
SUBLANES = 8
ROUTER_LANE0 = N_GROUPS
VMEM_LIMIT = 56 * 1024 * 1024

TM_IN = 1024
TM_PROJ = 512
TM_MIX = 512
SEQ_TILE = 8
TR = 128
TM_OUT = 256

_NT = (((1,), (1,)), ((), ()))


def _params(n_axes=1):
    return pltpu.CompilerParams(dimension_semantics=("arbitrary",) * n_axes, vmem_limit_bytes=VMEM_LIMIT)


def _resident(shape):
    return pl.BlockSpec(shape, lambda *_: (0,) * len(shape), pipeline_mode=pl.Buffered(1))


def _rmsnorm(x, g):
    ms = jnp.mean(x * x, axis=-1, keepdims=True)
    return x * lax.rsqrt(ms + RMS_EPS) * g


def _dot(a, b):
    return jnp.dot(a, b, preferred_element_type=F32)


def _dot_nt(a, b):
    return lax.dot_general(a, b, _NT, preferred_element_type=F32)


def _in_proj_body(x_ref, g_ref, w_ref, q_ref, k_ref, v_ref, u_ref):
    xn = _rmsnorm(x_ref[...], g_ref[...]).astype(BF16)
    q_ref[...] = (_dot(xn, w_ref[:, :D_ATTN]) * (HEAD_DIM ** -0.5)).astype(BF16)
    kv = _dot(xn, w_ref[:, D_ATTN:D_ATTN + 2 * KV_DIM])
    k_ref[...] = kv[:, :KV_DIM]
    v_ref[...] = kv[:, KV_DIM:]
    c0 = D_ATTN + 2 * KV_DIM
    cw = 256
    for j in range(D_CONV // cw):
        a = _dot(xn, w_ref[:, c0 + j * cw:c0 + (j + 1) * cw])
        b = _dot(xn, w_ref[:, c0 + D_CONV + j * cw:c0 + D_CONV + (j + 1) * cw])
        u_ref[:, j * cw:(j + 1) * cw] = a * jax.nn.sigmoid(b)


def _in_proj(x, g, w_bf):
    n = x.shape[0]
    tm = min(TM_IN, n)
    row = lambda w: pl.BlockSpec((tm, w), lambda i: (i, 0))
    return pl.pallas_call(
        _in_proj_body,
        grid=(n // tm,),
        in_specs=[row(D_MODEL), _resident((1, D_MODEL)), _resident((D_MODEL, D_IN))],
        out_specs=[row(D_ATTN), row(KV_DIM), row(KV_DIM), row(D_CONV)],
        out_shape=[jax.ShapeDtypeStruct((n, D_ATTN), BF16), jax.ShapeDtypeStruct((n, KV_DIM), F32),
                   jax.ShapeDtypeStruct((n, KV_DIM), F32), jax.ShapeDtypeStruct((n, D_CONV), F32)],
        compiler_params=_params(),
        name="in_proj",
    )(x, g, w_bf)


def _attend(q, parts, sink_ref, kv_head, keys_on_lanes=False):
    t = q.shape[0]
    score, weigh = (_dot, _dot_nt) if keys_on_lanes else (_dot_nt, _dot)
    qs = jnp.concatenate([q[:, g * HEAD_DIM:(g + 1) * HEAD_DIM] for g in range(GQA_GROUP)], axis=0)
    scores = [score(qs, kk) for kk, _, _ in parts]
    hidden = [jnp.where(mask, 0.0, -jnp.inf) for _, _, mask in parts]
    ps = [[] for _ in parts]
    inv = []
    for g in range(GQA_GROUP):
        sg = [s[g * t:(g + 1) * t] + neg for s, neg in zip(scores, hidden)]
        sk = sink_ref[kv_head * GQA_GROUP + g]
        m = sk
        for x in sg:
            m = jnp.maximum(jnp.max(x, axis=-1, keepdims=True), m)
        den = jnp.exp(sk - m)
        for k, x in enumerate(sg):
            p = jnp.exp(x - m)
            den = den + jnp.sum(p, axis=-1, keepdims=True)
            ps[k].append(p.astype(BF16))
        inv.append(1.0 / den)
    o = None
    for k, (_, vv, _) in enumerate(parts):
        ok = weigh(jnp.concatenate(ps[k], axis=0), vv)
        o = ok if o is None else o + ok
    return jnp.concatenate([o[g * t:(g + 1) * t] * inv[g] for g in range(GQA_GROUP)], axis=1)


def _conv_tail(y, dwb_ref, lng_ref, lnb_ref, pw_ref, pwb_ref):
    y = y + dwb_ref[...]
    mu = jnp.mean(y, axis=-1, keepdims=True)
    yc = y - mu
    yn = yc * lax.rsqrt(jnp.mean(yc * yc, axis=-1, keepdims=True) + LN_EPS)
    yn = yn * lng_ref[...] + lnb_ref[...]
    act = yn * jax.nn.sigmoid(yn)
    return _dot(act.astype(BF16), pw_ref[...]) + pwb_ref[...]


CONV_PAD = 32
CONV_ROWS = 128
CONV_COLS = 128


def _mixer_prompt_body(sink_ref, q_ref, k_ref, v_ref, u_ref, dww_ref, dwb_ref, lng_ref, lnb_ref, pw_ref, pwb_ref,
                       mix_ref, kprev, vprev, uext, ushift, ybuf):
    i = pl.program_id(0)
    tm = q_ref.shape[0]

    @pl.when(i == 0)
    def _():
        kprev[...] = jnp.zeros_like(kprev)
        vprev[...] = jnp.zeros_like(vprev)
        uext[0:CONV_PAD, :] = jnp.zeros((CONV_PAD, D_CONV), F32)

    qi = lax.broadcasted_iota(jnp.int32, (WINDOW, 2 * WINDOW), 0)
    kj = lax.broadcasted_iota(jnp.int32, (WINDOW, 2 * WINDOW), 1)
    diff = qi + WINDOW - kj
    band = (diff >= 0) & (diff <= WINDOW)
    kp, vp = kprev[...], vprev[...]
    for b in range(tm // WINDOW):
        rows = slice(b * WINDOW, (b + 1) * WINDOW)
        kb = k_ref[rows, :].astype(BF16)
        vb = v_ref[rows, :].astype(BF16)
        kk = jnp.concatenate([kp, kb], axis=0)
        vv = jnp.concatenate([vp, vb], axis=0)
        if b == 0:
            mask = band & (kj >= jnp.where(i > 0, 0, WINDOW))
        else:
            mask = band
        for h in range(N_KV_HEADS):
            cols = slice(h * HEAD_DIM, (h + 1) * HEAD_DIM)
            hq = slice(h * GQA_GROUP * HEAD_DIM, (h + 1) * GQA_GROUP * HEAD_DIM)
            o = _attend(q_ref[rows, hq], [(kk[:, cols], vv[:, cols], mask)], sink_ref, h)
            mix_ref[rows, hq] = o.astype(BF16)
        kp, vp = kb, vb
    kprev[...] = kp
    vprev[...] = vp

    uext[CONV_PAD:CONV_PAD + tm, :] = u_ref[...]
    n_shift_rows = ushift.shape[1]
    for s in range(1, SUBLANES):
        ushift[s - 1] = uext[s:s + n_shift_rows, :]
    off = CONV_PAD - (CONV_W - 1)
    for r in range(tm // CONV_ROWS):
        for c in range(D_CONV // CONV_COLS):
            cs = slice(c * CONV_COLS, (c + 1) * CONV_COLS)
            acc = jnp.zeros((CONV_ROWS, CONV_COLS), F32)
            for j in range(CONV_W):
                a, s = divmod(off + j, SUBLANES)
                r0 = r * CONV_ROWS + a * SUBLANES
                src = uext[r0:r0 + CONV_ROWS, cs] if s == 0 else ushift[s - 1, r0:r0 + CONV_ROWS, cs]
                acc = acc + dww_ref[j:j + 1, cs] * src
            ybuf[r * CONV_ROWS:(r + 1) * CONV_ROWS, cs] = acc
    uext[0:CONV_PAD, :] = uext[tm:tm + CONV_PAD, :]
    conv = _conv_tail(ybuf[...], dwb_ref, lng_ref, lnb_ref, pw_ref, pwb_ref)
    mix_ref[:, D_ATTN:] = conv.astype(BF16)


def _mixer_prompt(sink, q, k, v, u, dww, dwb, lng, lnb, pw_bf, pwb):
    n = q.shape[0]
    tm = TM_MIX
    row = lambda w: pl.BlockSpec((tm, w), lambda i, s: (i, 0))
    res = lambda shape: pl.BlockSpec(shape, lambda i, s: (0,) * len(shape), pipeline_mode=pl.Buffered(1))
    return pl.pallas_call(
        _mixer_prompt_body,
        grid_spec=pltpu.PrefetchScalarGridSpec(
            num_scalar_prefetch=1,
            grid=(n // tm,),
            in_specs=[row(D_ATTN), row(KV_DIM), row(KV_DIM), row(D_CONV), res((CONV_W, D_CONV)), res((1, D_CONV)),
                      res((1, D_CONV)), res((1, D_CONV)), res((D_CONV, D_CONV)), res((1, D_CONV))],
            out_specs=row(D_MODEL),
            scratch_shapes=[pltpu.VMEM((WINDOW, KV_DIM), BF16), pltpu.VMEM((WINDOW, KV_DIM), BF16),
                            pltpu.VMEM((CONV_PAD + tm, D_CONV), F32),
                            pltpu.VMEM((SUBLANES - 1, CONV_PAD - SUBLANES + tm, D_CONV), F32),
                            pltpu.VMEM((tm, D_CONV), F32)],
        ),
        out_shape=jax.ShapeDtypeStruct((n, D_MODEL), BF16),
        compiler_params=_params(),
        name="mixer_prompt",
    )(sink, q, k, v, u, dww, dwb, lng, lnb, pw_bf, pwb)


def _mixer_sample_body(sink_ref, q_ref, k_ref, v_ref, u_ref, ck_ref, cv_ref, cs_ref, dww_ref, dwb_ref, lng_ref,
                       lnb_ref, pw_ref, pwb_ref, mix_ref, wk_ref, wv_ref, wc_ref, usel, ybuf):
    nt = q_ref.shape[0]
    sb = ck_ref.shape[0]
    t_new = nt // sb
    n_cache = sb * WINDOW
    n_state = CONV_W - 1

    pad = jnp.zeros((LANES - nt, KV_DIM), F32)
    kn_t = jnp.concatenate([k_ref[...], pad], axis=0).T
    vn_t = jnp.concatenate([v_ref[...], pad], axis=0).T

    lt, lw = t_new.bit_length() - 1, WINDOW.bit_length() - 1
    qr = lax.broadcasted_iota(jnp.int32, (nt, n_cache), 0)
    cc = lax.broadcasted_iota(jnp.int32, (nt, n_cache), 1)
    mask_c = ((cc >> lw) == (qr >> lt)) & ((cc & (WINDOW - 1)) >= (qr & (t_new - 1)))
    qr = lax.broadcasted_iota(jnp.int32, (nt, LANES), 0)
    cn = lax.broadcasted_iota(jnp.int32, (nt, LANES), 1)
    mask_n = ((cn >> lt) == (qr >> lt)) & ((cn & (t_new - 1)) <= (qr & (t_new - 1)))
    for h in range(N_KV_HEADS):
        dims = slice(h * HEAD_DIM, (h + 1) * HEAD_DIM)
        hq = slice(h * GQA_GROUP * HEAD_DIM, (h + 1) * GQA_GROUP * HEAD_DIM)
        kc = jnp.concatenate([ck_ref[s, h] for s in range(sb)], axis=1).astype(BF16)
        vc = jnp.concatenate([cv_ref[s, h] for s in range(sb)], axis=1).astype(BF16)
        parts = [(kc, vc, mask_c), (kn_t[dims, :].astype(BF16), vn_t[dims, :].astype(BF16), mask_n)]
        o = _attend(q_ref[:, hq], parts, sink_ref, h, keys_on_lanes=True)
        mix_ref[:, hq] = o.astype(BF16)

    is_new = lax.broadcasted_iota(jnp.int32, (HEAD_DIM, WINDOW), 1) >= WINDOW - t_new
    for s in range(sb):
        shift = (WINDOW - t_new - s * t_new) % LANES
        kn_s = pltpu.roll(kn_t, shift=shift, axis=1)
        vn_s = pltpu.roll(vn_t, shift=shift, axis=1)
        for h in range(N_KV_HEADS):
            dims = slice(h * HEAD_DIM, (h + 1) * HEAD_DIM)
            old_k = pltpu.roll(ck_ref[s, h], shift=WINDOW - t_new, axis=1)
            old_v = pltpu.roll(cv_ref[s, h], shift=WINDOW - t_new, axis=1)
            wk_ref[s, h] = jnp.where(is_new, kn_s[dims, :], old_k)
            wv_ref[s, h] = jnp.where(is_new, vn_s[dims, :], old_v)

    for s in range(sb):
        for t in range(t_new):
            usel[t, s:s + 1, :] = u_ref[s * t_new + t:s * t_new + t + 1, :]
    u_new = [usel[t] for t in range(t_new)]
    for t in range(t_new):
        acc = jnp.zeros((sb, D_CONV), F32)
        for r in range(t, n_state):
            acc = acc + dww_ref[r - t:r - t + 1, :] * cs_ref[r]
        for t2 in range(t + 1):
            j = n_state - t + t2
            acc = acc + dww_ref[j:j + 1, :] * u_new[t2]
        for s in range(sb):
            ybuf[s * t_new + t:s * t_new + t + 1, :] = acc[s:s + 1, :]
        wc_ref[n_state - t_new + t] = u_new[t]
    wc_ref[0:n_state - t_new] = cs_ref[t_new:n_state]
    conv = _conv_tail(ybuf[...], dwb_ref, lng_ref, lnb_ref, pw_ref, pwb_ref)
    mix_ref[:, D_ATTN:] = conv.astype(BF16)


def _mixer_sample(sink, q, k, v, u, ck, cv, cs, dww, dwb, lng, lnb, pw_bf, pwb):
    n = q.shape[0]
    n_seq = ck.shape[1]
    t_new = n // n_seq
    sb = SEQ_TILE
    nt = sb * t_new
    n_state = CONV_W - 1
    row = lambda w: pl.BlockSpec((nt, w), lambda i, s: (i, 0))
    cache = pl.BlockSpec((None, sb, N_KV_HEADS, HEAD_DIM, WINDOW), lambda i, s: (0, i, 0, 0, 0))
    state = pl.BlockSpec((None, n_state, sb, D_CONV), lambda i, s: (0, 0, i, 0))
    res = lambda shape: pl.BlockSpec(shape, lambda i, s: (0,) * len(shape), pipeline_mode=pl.Buffered(1))
    return pl.pallas_call(
        _mixer_sample_body,
        grid_spec=pltpu.PrefetchScalarGridSpec(
            num_scalar_prefetch=1,
            grid=(n_seq // sb,),
            in_specs=[row(D_ATTN), row(KV_DIM), row(KV_DIM), row(D_CONV), cache, cache, state,
                      res((CONV_W, D_CONV)), res((1, D_CONV)), res((1, D_CONV)), res((1, D_CONV)),
                      res((D_CONV, D_CONV)), res((1, D_CONV))],
            out_specs=[row(D_MODEL), cache, cache, state],
            scratch_shapes=[pltpu.VMEM((t_new, sb, D_CONV), F32), pltpu.VMEM((nt, D_CONV), F32)],
        ),
        out_shape=[jax.ShapeDtypeStruct((n, D_MODEL), BF16), jax.ShapeDtypeStruct(ck.shape, F32),
                   jax.ShapeDtypeStruct(cv.shape, F32), jax.ShapeDtypeStruct(cs.shape, F32)],
        compiler_params=_params(),
        name="mixer_sample",
    )(sink, q, k, v, u, ck, cv, cs, dww, dwb, lng, lnb, pw_bf, pwb)


META_E0, META_E1, META_G0, META_G1, META_R0, META_R1 = range(6)


ROUTER_ROWS = 40


def _out_router_body(mixp_ref, xp_ref, mixs_ref, xs_ref, wout_ref, g2_ref, rw_ref, rb_ref, h_ref, meta_ref, rec_ref,
                     cnt_ref, carry, earlier, *, prompt_tiles):
    i = pl.program_id(0)
    tm = xp_ref.shape[0]

    @pl.when(i == 0)
    def _():
        carry[...] = jnp.zeros_like(carry)
        ti = lax.broadcasted_iota(jnp.int32, (tm, tm), 0)
        tj = lax.broadcasted_iota(jnp.int32, (tm, tm), 1)
        earlier[...] = jnp.where(ti < tj, 1.0, 0.0).astype(BF16)

    is_prompt = i < prompt_tiles
    x = jnp.where(is_prompt, xp_ref[...], xs_ref[...])
    mix = jnp.where(is_prompt, mixp_ref[...], mixs_ref[...])
    h = x + _dot(mix, wout_ref[...])
    h_ref[...] = h
    xn = _rmsnorm(h, g2_ref[...]).astype(BF16)
    logits = _dot(xn, rw_ref[...]) + rb_ref[...]
    lt = logits.T[:ROUTER_ROWS, :]
    row = lax.broadcasted_iota(jnp.int32, lt.shape, 0).astype(F32)
    first = lambda cond: jnp.min(jnp.where(cond, row, float(LANES)), axis=0, keepdims=True)

    gl = jnp.where(row < N_GROUPS, lt, -jnp.inf)
    gmax = jnp.max(gl, axis=0, keepdims=True)
    gidx = first(gl == gmax)
    gval = 1.0 / jnp.sum(jnp.exp(gl - gmax), axis=0, keepdims=True)

    lo = ROUTER_LANE0 + EXPERTS_PER_GROUP * gidx
    el = jnp.where((row >= lo) & (row < lo + EXPERTS_PER_GROUP), lt, -jnp.inf)
    m1 = jnp.max(el, axis=0, keepdims=True)
    i1 = first(el == m1)
    el2 = jnp.where(row == i1, -jnp.inf, el)
    m2 = jnp.max(el2, axis=0, keepdims=True)
    i2 = first(el2 == m2)
    r = jnp.exp(m2 - m1)
    g0 = gval * (1.0 / (1.0 + r))
    g1 = gval * (r / (1.0 + r))

    sel0, sel1 = row == i1, row == i2
    onehot = jnp.where(sel0 | sel1, 1.0, 0.0)
    before = _dot(onehot.astype(BF16), earlier[...]) + carry[0:ROUTER_ROWS, 0:1]
    r0 = jnp.sum(jnp.where(sel0, before, 0.0), axis=0, keepdims=True)
    r1 = jnp.sum(jnp.where(sel1, before, 0.0), axis=0, keepdims=True)
    carry[0:ROUTER_ROWS, :] = carry[0:ROUTER_ROWS, :] + jnp.sum(onehot, axis=1, keepdims=True)
    cnt_ref[...] = carry[...]

    row8 = lax.broadcasted_iota(jnp.int32, (SUBLANES, tm), 0)
    rec = jnp.zeros((SUBLANES, tm), F32)
    for slot, val in ((META_E0, i1 - ROUTER_LANE0), (META_E1, i2 - ROUTER_LANE0), (META_G0, g0), (META_G1, g1),
                      (META_R0, r0), (META_R1, r1)):
        rec = jnp.where(row8 == slot, val, rec)
    rec_ref[...] = rec
    meta_ref[...] = jnp.concatenate([rec, jnp.zeros((LANES - SUBLANES, tm), F32)], axis=0).T


def _out_router(mix_p, x_p, mix_s, x_s, wout_bf, g2, rw_bf, rb):
    n_p, n_s = x_p.shape[0], x_s.shape[0]
    tm = TM_PROJ
    assert n_p % tm == 0 and n_s % tm == 0
    tp, ts = n_p // tm, n_s // tm
    n = n_p + n_s
    prow = lambda w: pl.BlockSpec((tm, w), lambda i: (jnp.minimum(i, tp - 1), 0))
    srow = lambda w: pl.BlockSpec((tm, w), lambda i: (jnp.maximum(i - tp, 0), 0))
    row = lambda w: pl.BlockSpec((tm, w), lambda i: (i, 0))
    return pl.pallas_call(
        functools.partial(_out_router_body, prompt_tiles=tp),
        grid=(tp + ts,),
        in_specs=[prow(D_MODEL), prow(D_MODEL), srow(D_MODEL), srow(D_MODEL), _resident((D_MODEL, D_MODEL)),
                  _resident((1, D_MODEL)), _resident((D_MODEL, LANES)), _resident((1, LANES))],
        out_specs=[row(D_MODEL), row(LANES), pl.BlockSpec((SUBLANES, tm), lambda i: (0, i)),
                   pl.BlockSpec((LANES, LANES), lambda i: (0, 0))],
        out_shape=[jax.ShapeDtypeStruct((n, D_MODEL), F32), jax.ShapeDtypeStruct((n, LANES), F32),
                   jax.ShapeDtypeStruct((SUBLANES, n), F32), jax.ShapeDtypeStruct((LANES, LANES), F32)],
        scratch_shapes=[pltpu.VMEM((LANES, LANES), F32), pltpu.VMEM((tm, tm), BF16)],
        compiler_params=_params(),
        name="out_router",
    )(mix_p, x_p, mix_s, x_s, wout_bf, g2, rw_bf, rb)


def _positions_body(rec_ref, off_ref, pos_ref):
    rec = rec_ref[...]
    tm = rec.shape[1]
    row = lax.broadcasted_iota(jnp.int32, (ROUTER_ROWS, tm), 0).astype(F32)
    row8 = lax.broadcasted_iota(jnp.int32, rec.shape, 0)
    off = off_ref[:, 0:1]
    pos = jnp.zeros(rec.shape, F32)
    for slot, (e_row, r_row) in enumerate(((META_E0, META_R0), (META_E1, META_R1))):
        e = rec[e_row:e_row + 1, :] + ROUTER_LANE0
        seg = jnp.sum(jnp.where(row == e, off, 0.0), axis=0, keepdims=True)
        pos = jnp.where(row8 == slot, seg + rec[r_row:r_row + 1, :], pos)
    pos_ref[...] = pos.astype(jnp.int32)


def _positions(rec_all, off_rows):
    n = rec_all.shape[1]
    tm = n
    return pl.pallas_call(
        _positions_body,
        grid=(n // tm,),
        in_specs=[pl.BlockSpec((SUBLANES, tm), lambda i: (0, i)), _resident((ROUTER_ROWS, LANES))],
        out_specs=pl.BlockSpec((SUBLANES, tm), lambda i: (0, i)),
        out_shape=jax.ShapeDtypeStruct((SUBLANES, n), jnp.int32),
        compiler_params=_params(),
        name="positions",
    )(rec_all, off_rows)


N_LOAD_SLOTS = 3


def _pow2_sizes(n):
    return [1 << b for b in range(n.bit_length() - 1, -1, -1)]


def _dispatch_body(pos_ref, nv_ref, h_ref, hs_ref, hbuf, zbuf, lsem, ssem, zsem, *, n_tok, tm):
    i = pl.program_id(0)
    n_steps = pl.num_programs(0)
    n_tiles = nv_ref.shape[0]

    groups = tm // SUBLANES

    def load(t, slot, sem=lsem):
        return pltpu.make_async_copy(h_ref.at[pl.ds(t * groups, groups)], hbuf.at[slot], sem.at[slot])

    def clear_unowned(start):
        def per_tile(t, c):
            nv = nv_ref[t]
            z = TR - nv
            row = t * TR + nv
            head = z & (SUBLANES - 1)
            for k in range(SUBLANES - 1):
                @pl.when(k < head)
                def _(k=k):
                    cp = pltpu.make_async_copy(zbuf.at[pl.ds(0, 1), :], hs_ref.at[pl.ds(row + k, 1), :], zsem)
                    cp.start() if start else cp.wait()
            row = row + head
            for size in _pow2_sizes(TR):
                if size < SUBLANES:
                    break
                @pl.when((z & size) != 0)
                def _(row=row, size=size):
                    dst = hs_ref.at[pl.ds(pl.multiple_of(row, SUBLANES), size), :]
                    cp = pltpu.make_async_copy(zbuf.at[pl.ds(0, size), :], dst, zsem)
                    cp.start() if start else cp.wait()
                row = row + (z & size)
            return c
        lax.fori_loop(0, n_tiles, per_tile, 0)

    def scatter_wait(slot):
        for _ in range(2):
            load(0, slot, ssem).wait()

    @pl.when(i == 0)
    def _():
        zbuf[...] = jnp.zeros_like(zbuf)
        clear_unowned(True)
        load(0, 0).start()

    @pl.when(i + 1 < n_steps)
    def _():
        load(i + 1, (i + 1) % N_LOAD_SLOTS).start()

    slot = i % N_LOAD_SLOTS
    load(i, slot).wait()

    def issue(jb, c):
        for u in range(SUBLANES):
            for s in range(2):
                p = pos_ref[s * n_tok + i * tm + jb * SUBLANES + u]
                pltpu.make_async_copy(hbuf.at[slot, jb, pl.ds(u, 1), :], hs_ref.at[pl.ds(p, 1), :],
                                      ssem.at[slot]).start()
        return c
    lax.fori_loop(0, groups, issue, 0)

    @pl.when(i > 0)
    def _():
        scatter_wait((i + N_LOAD_SLOTS - 1) % N_LOAD_SLOTS)

    @pl.when(i == n_steps - 1)
    def _():
        scatter_wait(slot)
        clear_unowned(False)


def _dispatch(pos, nv, rows):
    n_tok, width = rows.shape
    n_tiles = nv.shape[0]
    tm = TM_PROJ
    assert n_tok % tm == 0
    return pl.pallas_call(
        functools.partial(_dispatch_body, n_tok=n_tok, tm=tm),
        grid_spec=pltpu.PrefetchScalarGridSpec(
            num_scalar_prefetch=2,
            grid=(n_tok // tm,),
            in_specs=[pl.BlockSpec(memory_space=pl.ANY)],
            out_specs=pl.BlockSpec(memory_space=pl.ANY),
            scratch_shapes=[pltpu.VMEM((N_LOAD_SLOTS, tm // SUBLANES, SUBLANES, width), rows.dtype),
                            pltpu.VMEM((TR, width), rows.dtype),
                            pltpu.SemaphoreType.DMA((N_LOAD_SLOTS,)), pltpu.SemaphoreType.DMA((N_LOAD_SLOTS,)),
                            pltpu.SemaphoreType.DMA(())],
        ),
        out_shape=jax.ShapeDtypeStruct((n_tiles * TR, width), rows.dtype),
        compiler_params=_params(),
        name="dispatch",
    )(pos, nv, rows.reshape(n_tok // SUBLANES, SUBLANES, width))


N_MOE_LOADS = 4


def _moe_body(ts_ref, hs_ref, g2_ref, wg_ref, wu_ref, wd_ref, ys_ref, xbuf, obuf, wg_bf, wu_bf, wd_bf, lsem, ssem):
    e = pl.program_id(0)
    n_tiles = ys_ref.shape[0] // TR
    n_used = ts_ref[N_EXPERTS]
    t0, t1 = ts_ref[e], ts_ref[e + 1]

    def load(g, slot):
        return pltpu.make_async_copy(hs_ref.at[pl.ds(pl.multiple_of(g * TR, TR), TR), :], xbuf.at[slot], lsem.at[slot])

    def store(g, slot):
        return pltpu.make_async_copy(obuf.at[slot], ys_ref.at[pl.ds(pl.multiple_of(g * TR, TR), TR), :], ssem.at[slot])

    @pl.when(e == 0)
    def _():
        for g in range(N_MOE_LOADS - 1):
            @pl.when(g < n_used)
            def _(g=g):
                load(g, g).start()

    @pl.when(t1 > t0)
    def _():
        wg_bf[...] = wg_ref[0].astype(BF16)
        wu_bf[...] = wu_ref[0].astype(BF16)
        wd_bf[...] = wd_ref[0].astype(BF16)

    def tile(g, c):
        slot = g % 2
        xslot = g % N_MOE_LOADS
        ahead = g + N_MOE_LOADS - 1

        @pl.when(ahead < n_used)
        def _():
            load(ahead, ahead % N_MOE_LOADS).start()

        load(g, xslot).wait()
        xn = _rmsnorm(xbuf[xslot], g2_ref[...]).astype(BF16)
        hg = _dot(xn, wg_bf[...])
        hu = _dot(xn, wu_bf[...])
        hid = (hg * jax.nn.sigmoid(hg) * hu).astype(BF16)
        y = _dot(hid, wd_bf[...])

        @pl.when(g >= 2)
        def _():
            store(g - 2, slot).wait()

        obuf[slot] = y
        store(g, slot).start()
        return c

    lax.fori_loop(t0, t1, tile, 0)

    @pl.when(e == pl.num_programs(0) - 1)
    def _():
        @pl.when(n_used >= 2)
        def _():
            store(n_used - 2, n_used % 2).wait()
        store(n_used - 1, (n_used - 1) % 2).wait()
        obuf[0] = jnp.zeros(obuf.shape[1:], obuf.dtype)

        def clear(g, c):
            store(g, 0).start()
            return c
        lax.fori_loop(n_used, n_tiles, clear, 0)

        def drain(g, c):
            store(g, 0).wait()
            return c
        lax.fori_loop(n_used, n_tiles, drain, 0)


def _moe(ts, hs, g2, wg, wu, wd):
    wspec = lambda shape: pl.BlockSpec((1,) + shape, lambda e, ts: (e, 0, 0))
    tile_buf = lambda slots: pltpu.VMEM((slots, TR, hs.shape[1]), hs.dtype)
    return pl.pallas_call(
        _moe_body,
        grid_spec=pltpu.PrefetchScalarGridSpec(
            num_scalar_prefetch=1,
            grid=(N_EXPERTS,),
            in_specs=[pl.BlockSpec(memory_space=pl.ANY),
                      pl.BlockSpec((1, D_MODEL), lambda e, ts: (0, 0)),
                      wspec((D_MODEL, D_EXPERT)), wspec((D_MODEL, D_EXPERT)), wspec((D_EXPERT, D_MODEL))],
            out_specs=pl.BlockSpec(memory_space=pl.ANY),
            scratch_shapes=[tile_buf(N_MOE_LOADS), tile_buf(2),
                            pltpu.VMEM((D_MODEL, D_EXPERT), BF16), pltpu.VMEM((D_MODEL, D_EXPERT), BF16),
                            pltpu.VMEM((D_EXPERT, D_MODEL), BF16),
                            pltpu.SemaphoreType.DMA((N_MOE_LOADS,)), pltpu.SemaphoreType.DMA((2,))],
        ),
        out_shape=jax.ShapeDtypeStruct(hs.shape, hs.dtype),
        compiler_params=_params(),
        name="moe",
    )(ts, hs, g2, wg, wu, wd)


def _combine_body(pos_ref, h_ref, meta_ref, g_ref, ys_ref, ys_grouped_ref, op_ref, os_ref, ybuf, sem, *,
                  n_tok, prompt_tiles):
    i = pl.program_id(0)
    n_steps = pl.num_programs(0)
    tm = h_ref.shape[0]

    groups = tm // SUBLANES

    def gather(t, slot):
        def issue(jb, c):
            for u in range(SUBLANES):
                for s in range(2):
                    p = pos_ref[s * n_tok + t * tm + jb * SUBLANES + u]
                    pltpu.make_async_copy(ys_ref.at[pl.ds(p, 1), :], ybuf.at[slot, s, jb, pl.ds(u, 1), :],
                                          sem.at[slot]).start()
            return c
        lax.fori_loop(0, groups, issue, 0)

    @pl.when(i == 0)
    def _():
        gather(0, 0)

    @pl.when(i + 1 < n_steps)
    def _():
        gather(i + 1, (i + 1) % 2)

    slot = i % 2
    for s in range(2):
        pltpu.make_async_copy(ys_grouped_ref.at[pl.ds(0, groups)], ybuf.at[slot, s], sem.at[slot]).wait()
    meta = meta_ref[...]
    g0 = meta[:, META_G0:META_G0 + 1]
    g1 = meta[:, META_G1:META_G1 + 1]
    y0 = ybuf[slot, 0].reshape(tm, D_MODEL)
    y1 = ybuf[slot, 1].reshape(tm, D_MODEL)
    h = h_ref[...] + (g0 * y0 + g1 * y1)
    out = _rmsnorm(h, g_ref[...])

    @pl.when(i < prompt_tiles)
    def _():
        op_ref[...] = out

    @pl.when(i >= prompt_tiles)
    def _():
        os_ref[...] = out


def _combine(pos, h_all, meta_all, ys, gf, n_p):
    n_tok = h_all.shape[0]
    tm = TM_OUT
    assert n_p % tm == 0 and n_tok % tm == 0 and 0 < n_p < n_tok
    tp = n_p // tm
    return pl.pallas_call(
        functools.partial(_combine_body, n_tok=n_tok, prompt_tiles=tp),
        grid_spec=pltpu.PrefetchScalarGridSpec(
            num_scalar_prefetch=1,
            grid=(n_tok // tm,),
            in_specs=[pl.BlockSpec((tm, D_MODEL), lambda i, pos: (i, 0)),
                      pl.BlockSpec((tm, LANES), lambda i, pos: (i, 0)),
                      pl.BlockSpec((1, D_MODEL), lambda i, pos: (0, 0)),
                      pl.BlockSpec(memory_space=pl.ANY), pl.BlockSpec(memory_space=pl.ANY)],
            out_specs=[pl.BlockSpec((tm, D_MODEL), lambda i, pos: (jnp.minimum(i, tp - 1), 0)),
                       pl.BlockSpec((tm, D_MODEL), lambda i, pos: (jnp.maximum(i - tp, 0), 0))],
            scratch_shapes=[pltpu.VMEM((2, 2, tm // SUBLANES, SUBLANES, ys.shape[1]), ys.dtype),
                            pltpu.SemaphoreType.DMA((2,))],
        ),
        out_shape=[jax.ShapeDtypeStruct((n_p, D_MODEL), F32), jax.ShapeDtypeStruct((n_tok - n_p, D_MODEL), F32)],
        compiler_params=_params(),
        name="combine",
    )(pos, h_all, meta_all, gf, ys, ys.reshape(ys.shape[0] // SUBLANES, SUBLANES, ys.shape[1]))


def _routing_tables(counts, n_tok):
    n_tiles = (2 * n_tok) // TR + N_EXPERTS
    cnt = counts[ROUTER_LANE0:ROUTER_LANE0 + N_EXPERTS].astype(jnp.int32)
    tiles_e = (cnt + TR - 1) // TR
    tile_end = jnp.cumsum(tiles_e)
    tile_start = tile_end - tiles_e
    off_rows = jnp.pad((tile_start * TR).astype(F32), (ROUTER_LANE0, ROUTER_ROWS - ROUTER_LANE0 - N_EXPERTS))
    off_rows = jnp.broadcast_to(off_rows[:, None], (ROUTER_ROWS, LANES))
    ts = jnp.concatenate([tile_start, tile_end[-1:]]).astype(jnp.int32)
    tile = jnp.arange(n_tiles, dtype=jnp.int32)
    owner = (tile[:, None] >= tile_start[None, :]) & (tile[:, None] < tile_end[None, :])
    rows_left = jnp.sum(jnp.where(owner, cnt[None, :] - (tile[:, None] - tile_start[None, :]) * TR, 0), axis=1)
    nv = jnp.clip(rows_left, 0, TR).astype(jnp.int32)
    return off_rows, ts, nv


def kernel(x_prompt, x_sample, cache_win_k, cache_win_v, state_conv, norm1_g, w_in, attn_sink, conv_dw_w, conv_dw_b,
           conv_ln_g, conv_ln_b, conv_pw_w, conv_pw_b, w_out, norm2_g, router_group_w, router_group_b,
           router_expert_w, router_expert_b, expert_w_gate, expert_w_up, expert_w_down, final_norm_g):
    depth = w_in.shape[0]
    assert depth == 1, "single-layer step"
    bp, sp, _ = x_prompt.shape
    assert bp == 1, "one prompt sequence"
    n_seq, t_new, _ = x_sample.shape
    n_p, n_s = bp * sp, n_seq * t_new
    n_tok = n_p + n_s
    l = 0

    row = lambda a: a.reshape(1, -1)
    w_in_bf = w_in[l].astype(BF16)
    w_out_bf = w_out[l].astype(BF16)
    pw_bf = conv_pw_w[l].astype(BF16)
    rw = jnp.concatenate([router_group_w[l], router_expert_w[l]], axis=1)
    rw_bf = jnp.pad(rw, ((0, 0), (0, LANES - rw.shape[1]))).astype(BF16)
    rb = jnp.pad(jnp.concatenate([router_group_b[l], router_expert_b[l]]), (0, LANES - rw.shape[1])).reshape(1, LANES)
    g1, g2, gf = row(norm1_g[l]), row(norm2_g[l]), row(final_norm_g)
    sink = attn_sink[l]
    conv_w = (conv_dw_w[l], row(conv_dw_b[l]), row(conv_ln_g[l]), row(conv_ln_b[l]), pw_bf, row(conv_pw_b[l]))

    xp = x_prompt.reshape(n_p, D_MODEL)
    xs = x_sample.reshape(n_s, D_MODEL)
    cache_axes, cache_axes_back = (0, 1, 3, 4, 2), (0, 1, 4, 2, 3)
    ck = jnp.transpose(cache_win_k, cache_axes)
    cv = jnp.transpose(cache_win_v, cache_axes)
    cs = jnp.transpose(state_conv, (0, 2, 1, 3))

    qp, kp, vp, up = _in_proj(xp, g1, w_in_bf)
    qs, ks, vs, us = _in_proj(xs, g1, w_in_bf)
    mix_p = _mixer_prompt(sink, qp, kp, vp, up, *conv_w)
    mix_s, wk_s, wv_s, wc_s = _mixer_sample(sink, qs, ks, vs, us, ck, cv, cs, *conv_w)

    h_all, meta_all, rec_all, cnt = _out_router(mix_p, xp, mix_s, xs, w_out_bf, g2, rw_bf, rb)

    off_rows, ts, nv = _routing_tables(cnt[:, 0], n_tok)
    pos_rows = _positions(rec_all, off_rows)
    pos = jnp.concatenate([pos_rows[0], pos_rows[1]])
    hs = _dispatch(pos, nv, h_all)
    ys = _moe(ts, hs, g2, expert_w_gate[l], expert_w_up[l], expert_w_down[l])
    y_p, y_s = _combine(pos, h_all, meta_all, ys, gf, n_p)

    kv_shape = (depth, bp, WINDOW, N_KV_HEADS, HEAD_DIM)
    return (y_p.reshape(bp, sp, D_MODEL), y_s.reshape(n_seq, t_new, D_MODEL),
            kp[n_p - WINDOW:].reshape(kv_shape), vp[n_p - WINDOW:].reshape(kv_shape),
            up[n_p - (CONV_W - 1):].reshape(depth, bp, CONV_W - 1, D_CONV),
            jnp.transpose(wk_s, cache_axes_back), jnp.transpose(wv_s, cache_axes_back),
            jnp.transpose(wc_s, (0, 2, 1, 3)))
```

```python
import functools

import jax
import jax.numpy as jnp
from jax import lax
from jax.experimental import pallas as pl
from jax.experimental.pallas import tpu as pltpu

F32 = jnp.float32
BF16 = jnp.bfloat16

D_MODEL = 2048
HEAD_DIM = 64
N_HEADS = 16
N_KV_HEADS = 2
GQA_GROUP = 8
KV_DIM = N_KV_HEADS * HEAD_DIM
D_ATTN = N_HEADS * HEAD_DIM
D_CONV = D_MODEL - D_ATTN
WINDOW = 128
CONV_W = 31
D_IN = D_ATTN + 2 * KV_DIM + 2 * D_CONV
N_GROUPS = 4
EXPERTS_PER_GROUP = 8
N_EXPERTS = N_GROUPS * EXPERTS_PER_GROUP
D_EXPERT = 256
RMS_EPS = 1e-6
LN_EPS = 1e-5

LANES = 128
SUBLANES = 8
ROUTER_LANE0 = N_GROUPS
VMEM_LIMIT = 56 * 1024 * 1024

TM_IN = 1024
TM_PROJ = 512
TM_MIX = 512
SEQ_TILE = 8
TR = 128
TM_OUT = 256

_NT = (((1,), (1,)), ((), ()))


def _params(n_axes=1):
    return pltpu.CompilerParams(dimension_semantics=("arbitrary",) * n_axes, vmem_limit_bytes=VMEM_LIMIT)


def _resident(shape):
    return pl.BlockSpec(shape, lambda *_: (0,) * len(shape), pipeline_mode=pl.Buffered(1))


def _rmsnorm(x, g):
    ms = jnp.mean(x * x, axis=-1, keepdims=True)
    return x * lax.rsqrt(ms + RMS_EPS) * g


def _dot(a, b):
    return jnp.dot(a, b, preferred_element_type=F32)


def _dot_nt(a, b):
    return lax.dot_general(a, b, _NT, preferred_element_type=F32)


def _in_proj_body(x_ref, g_ref, w_ref, q_ref, k_ref, v_ref, u_ref):
    xn = _rmsnorm(x_ref[...], g_ref[...]).astype(BF16)
    q_ref[...] = (_dot(xn, w_ref[:, :D_ATTN]) * (HEAD_DIM ** -0.5)).astype(BF16)
    kv = _dot(xn, w_ref[:, D_ATTN:D_ATTN + 2 * KV_DIM])
    k_ref[...] = kv[:, :KV_DIM]
    v_ref[...] = kv[:, KV_DIM:]
    c0 = D_ATTN + 2 * KV_DIM
    cw = 256
    for j in range(D_CONV // cw):
        a = _dot(xn, w_ref[:, c0 + j * cw:c0 + (j + 1) * cw])
        b = _dot(xn, w_ref[:, c0 + D_CONV + j * cw:c0 + D_CONV + (j + 1) * cw])
        u_ref[:, j * cw:(j + 1) * cw] = a * jax.nn.sigmoid(b)


def _in_proj(x, g, w_bf):
    n = x.shape[0]
    tm = min(TM_IN, n)
    row = lambda w: pl.BlockSpec((tm, w), lambda i: (i, 0))
    return pl.pallas_call(
        _in_proj_body,
        grid=(n // tm,),
        in_specs=[row(D_MODEL), _resident((1, D_MODEL)), _resident((D_MODEL, D_IN))],
        out_specs=[row(D_ATTN), row(KV_DIM), row(KV_DIM), row(D_CONV)],
        out_shape=[jax.ShapeDtypeStruct((n, D_ATTN), BF16), jax.ShapeDtypeStruct((n, KV_DIM), F32),
                   jax.ShapeDtypeStruct((n, KV_DIM), F32), jax.ShapeDtypeStruct((n, D_CONV), F32)],
        compiler_params=_params(),
        name="in_proj",
    )(x, g, w_bf)


def _attend(q, parts, sink_ref, kv_head, keys_on_lanes=False):
    t = q.shape[0]
    score, weigh = (_dot, _dot_nt) if keys_on_lanes else (_dot_nt, _dot)
    qs = jnp.concatenate([q[:, g * HEAD_DIM:(g + 1) * HEAD_DIM] for g in range(GQA_GROUP)], axis=0)
    scores = [score(qs, kk) for kk, _, _ in parts]
    hidden = [jnp.where(mask, 0.0, -jnp.inf) for _, _, mask in parts]
    ps = [[] for _ in parts]
    inv = []
    for g in range(GQA_GROUP):
        sg = [s[g * t:(g + 1) * t] + neg for s, neg in zip(scores, hidden)]
        sk = sink_ref[kv_head * GQA_GROUP + g]
        m = sk
        for x in sg:
            m = jnp.maximum(jnp.max(x, axis=-1, keepdims=True), m)
        den = jnp.exp(sk - m)
        for k, x in enumerate(sg):
            p = jnp.exp(x - m)
            den = den + jnp.sum(p, axis=-1, keepdims=True)
            ps[k].append(p.astype(BF16))
        inv.append(1.0 / den)
    o = None
    for k, (_, vv, _) in enumerate(parts):
        ok = weigh(jnp.concatenate(ps[k], axis=0), vv)
        o = ok if o is None else o + ok
    return jnp.concatenate([o[g * t:(g + 1) * t] * inv[g] for g in range(GQA_GROUP)], axis=1)


def _conv_tail(y, dwb_ref, lng_ref, lnb_ref, pw_ref, pwb_ref):
    y = y + dwb_ref[...]
    mu = jnp.mean(y, axis=-1, keepdims=True)
    yc = y - mu
    yn = yc * lax.rsqrt(jnp.mean(yc * yc, axis=-1, keepdims=True) + LN_EPS)
    yn = yn * lng_ref[...] + lnb_ref[...]
    act = yn * jax.nn.sigmoid(yn)
    return _dot(act.astype(BF16), pw_ref[...]) + pwb_ref[...]


CONV_PAD = 32
CONV_ROWS = 128
CONV_COLS = 128


def _mixer_prompt_body(sink_ref, q_ref, k_ref, v_ref, u_ref, dww_ref, dwb_ref, lng_ref, lnb_ref, pw_ref, pwb_ref,
                       mix_ref, kprev, vprev, uext, ushift, ybuf):
    i = pl.program_id(0)
    tm = q_ref.shape[0]

    @pl.when(i == 0)
    def _():
        kprev[...] = jnp.zeros_like(kprev)
        vprev[...] = jnp.zeros_like(vprev)
        uext[0:CONV_PAD, :] = jnp.zeros((CONV_PAD, D_CONV), F32)

    qi = lax.broadcasted_iota(jnp.int32, (WINDOW, 2 * WINDOW), 0)
    kj = lax.broadcasted_iota(jnp.int32, (WINDOW, 2 * WINDOW), 1)
    diff = qi + WINDOW - kj
    band = (diff >= 0) & (diff <= WINDOW)
    kp, vp = kprev[...], vprev[...]
    for b in range(tm // WINDOW):
        rows = slice(b * WINDOW, (b + 1) * WINDOW)
        kb = k_ref[rows, :].astype(BF16)
        vb = v_ref[rows, :].astype(BF16)
        kk = jnp.concatenate([kp, kb], axis=0)
        vv = jnp.concatenate([vp, vb], axis=0)
        if b == 0:
            mask = band & (kj >= jnp.where(i > 0, 0, WINDOW))
        else:
            mask = band
        for h in range(N_KV_HEADS):
            cols = slice(h * HEAD_DIM, (h + 1) * HEAD_DIM)
            hq = slice(h * GQA_GROUP * HEAD_DIM, (h + 1) * GQA_GROUP * HEAD_DIM)
            o = _attend(q_ref[rows, hq], [(kk[:, cols], vv[:, cols], mask)], sink_ref, h)
            mix_ref[rows, hq] = o.astype(BF16)
        kp, vp = kb, vb
    kprev[...] = kp
    vprev[...] = vp

    uext[CONV_PAD:CONV_PAD + tm, :] = u_ref[...]
    n_shift_rows = ushift.shape[1]
    for s in range(1, SUBLANES):
        ushift[s - 1] = uext[s:s + n_shift_rows, :]
    off = CONV_PAD - (CONV_W - 1)
    for r in range(tm // CONV_ROWS):
        for c in range(D_CONV // CONV_COLS):
            cs = slice(c * CONV_COLS, (c + 1) * CONV_COLS)
            acc = jnp.zeros((CONV_ROWS, CONV_COLS), F32)
            for j in range(CONV_W):
                a, s = divmod(off + j, SUBLANES)
                r0 = r * CONV_ROWS + a * SUBLANES
                src = uext[r0:r0 + CONV_ROWS, cs] if s == 0 else ushift[s - 1, r0:r0 + CONV_ROWS, cs]
                acc = acc + dww_ref[j:j + 1, cs] * src
            ybuf[r * CONV_ROWS:(r + 1) * CONV_ROWS, cs] = acc
    uext[0:CONV_PAD, :] = uext[tm:tm + CONV_PAD, :]
    conv = _conv_tail(ybuf[...], dwb_ref, lng_ref, lnb_ref, pw_ref, pwb_ref)
    mix_ref[:, D_ATTN:] = conv.astype(BF16)


def _mixer_prompt(sink, q, k, v, u, dww, dwb, lng, lnb, pw_bf, pwb):
    n = q.shape[0]
    tm = TM_MIX
    row = lambda w: pl.BlockSpec((tm, w), lambda i, s: (i, 0))
    res = lambda shape: pl.BlockSpec(shape, lambda i, s: (0,) * len(shape), pipeline_mode=pl.Buffered(1))
    return pl.pallas_call(
        _mixer_prompt_body,
        grid_spec=pltpu.PrefetchScalarGridSpec(
            num_scalar_prefetch=1,
            grid=(n // tm,),
            in_specs=[row(D_ATTN), row(KV_DIM), row(KV_DIM), row(D_CONV), res((CONV_W, D_CONV)), res((1, D_CONV)),
                      res((1, D_CONV)), res((1, D_CONV)), res((D_CONV, D_CONV)), res((1, D_CONV))],
            out_specs=row(D_MODEL),
            scratch_shapes=[pltpu.VMEM((WINDOW, KV_DIM), BF16), pltpu.VMEM((WINDOW, KV_DIM), BF16),
                            pltpu.VMEM((CONV_PAD + tm, D_CONV), F32),
                            pltpu.VMEM((SUBLANES - 1, CONV_PAD - SUBLANES + tm, D_CONV), F32),
                            pltpu.VMEM((tm, D_CONV), F32)],
        ),
        out_shape=jax.ShapeDtypeStruct((n, D_MODEL), BF16),
        compiler_params=_params(),
        name="mixer_prompt",
    )(sink, q, k, v, u, dww, dwb, lng, lnb, pw_bf, pwb)


def _mixer_sample_body(sink_ref, q_ref, k_ref, v_ref, u_ref, ck_ref, cv_ref, cs_ref, dww_ref, dwb_ref, lng_ref,
                       lnb_ref, pw_ref, pwb_ref, mix_ref, wk_ref, wv_ref, wc_ref, usel, ybuf):
    nt = q_ref.shape[0]
    sb = ck_ref.shape[0]
    t_new = nt // sb
    n_cache = sb * WINDOW
    n_state = CONV_W - 1

    pad = jnp.zeros((LANES - nt, KV_DIM), F32)
    kn_t = jnp.concatenate([k_ref[...], pad], axis=0).T
    vn_t = jnp.concatenate([v_ref[...], pad], axis=0).T

    lt, lw = t_new.bit_length() - 1, WINDOW.bit_length() - 1
    qr = lax.broadcasted_iota(jnp.int32, (nt, n_cache), 0)
    cc = lax.broadcasted_iota(jnp.int32, (nt, n_cache), 1)
    mask_c = ((cc >> lw) == (qr >> lt)) & ((cc & (WINDOW - 1)) >= (qr & (t_new - 1)))
    qr = lax.broadcasted_iota(jnp.int32, (nt, LANES), 0)
    cn = lax.broadcasted_iota(jnp.int32, (nt, LANES), 1)
    mask_n = ((cn >> lt) == (qr >> lt)) & ((cn & (t_new - 1)) <= (qr & (t_new - 1)))
    for h in range(N_KV_HEADS):
        dims = slice(h * HEAD_DIM, (h + 1) * HEAD_DIM)
        hq = slice(h * GQA_GROUP * HEAD_DIM, (h + 1) * GQA_GROUP * HEAD_DIM)
        kc = jnp.concatenate([ck_ref[s, h] for s in range(sb)], axis=1).astype(BF16)
        vc = jnp.concatenate([cv_ref[s, h] for s in range(sb)], axis=1).astype(BF16)
        parts = [(kc, vc, mask_c), (kn_t[dims, :].astype(BF16), vn_t[dims, :].astype(BF16), mask_n)]
        o = _attend(q_ref[:, hq], parts, sink_ref, h, keys_on_lanes=True)
        mix_ref[:, hq] = o.astype(BF16)

    is_new = lax.broadcasted_iota(jnp.int32, (HEAD_DIM, WINDOW), 1) >= WINDOW - t_new
    for s in range(sb):
        shift = (WINDOW - t_new - s * t_new) % LANES
        kn_s = pltpu.roll(kn_t, shift=shift, axis=1)
        vn_s = pltpu.roll(vn_t, shift=shift, axis=1)
        for h in range(N_KV_HEADS):
            dims = slice(h * HEAD_DIM, (h + 1) * HEAD_DIM)
            old_k = pltpu.roll(ck_ref[s, h], shift=WINDOW - t_new, axis=1)
            old_v = pltpu.roll(cv_ref[s, h], shift=WINDOW - t_new, axis=1)
            wk_ref[s, h] = jnp.where(is_new, kn_s[dims, :], old_k)
            wv_ref[s, h] = jnp.where(is_new, vn_s[dims, :], old_v)

    for s in range(sb):
        for t in range(t_new):
            usel[t, s:s + 1, :] = u_ref[s * t_new + t:s * t_new + t + 1, :]
    u_new = [usel[t] for t in range(t_new)]
    for t in range(t_new):
        acc = jnp.zeros((sb, D_CONV), F32)
        for r in range(t, n_state):
            acc = acc + dww_ref[r - t:r - t + 1, :] * cs_ref[r]
        for t2 in range(t + 1):
            j = n_state - t + t2
            acc = acc + dww_ref[j:j + 1, :] * u_new[t2]
        for s in range(sb):
            ybuf[s * t_new + t:s * t_new + t + 1, :] = acc[s:s + 1, :]
        wc_ref[n_state - t_new + t] = u_new[t]
    wc_ref[0:n_state - t_new] = cs_ref[t_new:n_state]
    conv = _conv_tail(ybuf[...], dwb_ref, lng_ref, lnb_ref, pw_ref, pwb_ref)
    mix_ref[:, D_ATTN:] = conv.astype(BF16)


def _mixer_sample(sink, q, k, v, u, ck, cv, cs, dww, dwb, lng, lnb, pw_bf, pwb):
    n = q.shape[0]
    n_seq = ck.shape[1]
    t_new = n // n_seq
    sb = SEQ_TILE
    nt = sb * t_new
    n_state = CONV_W - 1
    row = lambda w: pl.BlockSpec((nt, w), lambda i, s: (i, 0))
    cache = pl.BlockSpec((None, sb, N_KV_HEADS, HEAD_DIM, WINDOW), lambda i, s: (0, i, 0, 0, 0))
    state = pl.BlockSpec((None, n_state, sb, D_CONV), lambda i, s: (0, 0, i, 0))
    res = lambda shape: pl.BlockSpec(shape, lambda i, s: (0,) * len(shape), pipeline_mode=pl.Buffered(1))
    return pl.pallas_call(
        _mixer_sample_body,
        grid_spec=pltpu.PrefetchScalarGridSpec(
            num_scalar_prefetch=1,
            grid=(n_seq // sb,),
            in_specs=[row(D_ATTN), row(KV_DIM), row(KV_DIM), row(D_CONV), cache, cache, state,
                      res((CONV_W, D_CONV)), res((1, D_CONV)), res((1, D_CONV)), res((1, D_CONV)),
                      res((D_CONV, D_CONV)), res((1, D_CONV))],
            out_specs=[row(D_MODEL), cache, cache, state],
            scratch_shapes=[pltpu.VMEM((t_new, sb, D_CONV), F32), pltpu.VMEM((nt, D_CONV), F32)],
        ),
        out_shape=[jax.ShapeDtypeStruct((n, D_MODEL), BF16), jax.ShapeDtypeStruct(ck.shape, F32),
                   jax.ShapeDtypeStruct(cv.shape, F32), jax.ShapeDtypeStruct(cs.shape, F32)],
        compiler_params=_params(),
        name="mixer_sample",
    )(sink, q, k, v, u, ck, cv, cs, dww, dwb, lng, lnb, pw_bf, pwb)


META_E0, META_E1, META_G0, META_G1, META_R0, META_R1 = range(6)


ROUTER_ROWS = 40


def _out_router_body(mixp_ref, xp_ref, mixs_ref, xs_ref, wout_ref, g2_ref, rw_ref, rb_ref, h_ref, meta_ref, rec_ref,
                     cnt_ref, carry, earlier, *, prompt_tiles):
    i = pl.program_id(0)
    tm = xp_ref.shape[0]

    @pl.when(i == 0)
    def _():
        carry[...] = jnp.zeros_like(carry)
        ti = lax.broadcasted_iota(jnp.int32, (tm, tm), 0)
        tj = lax.broadcasted_iota(jnp.int32, (tm, tm), 1)
        earlier[...] = jnp.where(ti < tj, 1.0, 0.0).astype(BF16)

    is_prompt = i < prompt_tiles
    x = jnp.where(is_prompt, xp_ref[...], xs_ref[...])
    mix = jnp.where(is_prompt, mixp_ref[...], mixs_ref[...])
    h = x + _dot(mix, wout_ref[...])
    h_ref[...] = h
    xn = _rmsnorm(h, g2_ref[...]).astype(BF16)
    logits = _dot(xn, rw_ref[...]) + rb_ref[...]
    lt = logits.T[:ROUTER_ROWS, :]
    row = lax.broadcasted_iota(jnp.int32, lt.shape, 0).astype(F32)
    first = lambda cond: jnp.min(jnp.where(cond, row, float(LANES)), axis=0, keepdims=True)

    gl = jnp.where(row < N_GROUPS, lt, -jnp.inf)
    gmax = jnp.max(gl, axis=0, keepdims=True)
    gidx = first(gl == gmax)
    gval = 1.0 / jnp.sum(jnp.exp(gl - gmax), axis=0, keepdims=True)

    lo = ROUTER_LANE0 + EXPERTS_PER_GROUP * gidx
    el = jnp.where((row >= lo) & (row < lo + EXPERTS_PER_GROUP), lt, -jnp.inf)
    m1 = jnp.max(el, axis=0, keepdims=True)
    i1 = first(el == m1)
    el2 = jnp.where(row == i1, -jnp.inf, el)
    m2 = jnp.max(el2, axis=0, keepdims=True)
    i2 = first(el2 == m2)
    r = jnp.exp(m2 - m1)
    g0 = gval * (1.0 / (1.0 + r))
    g1 = gval * (r / (1.0 + r))

    sel0, sel1 = row == i1, row == i2
    onehot = jnp.where(sel0 | sel1, 1.0, 0.0)
    before = _dot(onehot.astype(BF16), earlier[...]) + carry[0:ROUTER_ROWS, 0:1]
    r0 = jnp.sum(jnp.where(sel0, before, 0.0), axis=0, keepdims=True)
    r1 = jnp.sum(jnp.where(sel1, before, 0.0), axis=0, keepdims=True)
    carry[0:ROUTER_ROWS, :] = carry[0:ROUTER_ROWS, :] + jnp.sum(onehot, axis=1, keepdims=True)
    cnt_ref[...] = carry[...]

    row8 = lax.broadcasted_iota(jnp.int32, (SUBLANES, tm), 0)
    rec = jnp.zeros((SUBLANES, tm), F32)
    for slot, val in ((META_E0, i1 - ROUTER_LANE0), (META_E1, i2 - ROUTER_LANE0), (META_G0, g0), (META_G1, g1),
                      (META_R0, r0), (META_R1, r1)):
        rec = jnp.where(row8 == slot, val, rec)
    rec_ref[...] = rec
    meta_ref[...] = jnp.concatenate([rec, jnp.zeros((LANES - SUBLANES, tm), F32)], axis=0).T


def _out_router(mix_p, x_p, mix_s, x_s, wout_bf, g2, rw_bf, rb):
    n_p, n_s = x_p.shape[0], x_s.shape[0]
    tm = TM_PROJ
    assert n_p % tm == 0 and n_s % tm == 0
    tp, ts = n_p // tm, n_s // tm
    n = n_p + n_s
    prow = lambda w: pl.BlockSpec((tm, w), lambda i: (jnp.minimum(i, tp - 1), 0))
    srow = lambda w: pl.BlockSpec((tm, w), lambda i: (jnp.maximum(i - tp, 0), 0))
    row = lambda w: pl.BlockSpec((tm, w), lambda i: (i, 0))
    return pl.pallas_call(
        functools.partial(_out_router_body, prompt_tiles=tp),
        grid=(tp + ts,),
        in_specs=[prow(D_MODEL), prow(D_MODEL), srow(D_MODEL), srow(D_MODEL), _resident((D_MODEL, D_MODEL)),
                  _resident((1, D_MODEL)), _resident((D_MODEL, LANES)), _resident((1, LANES))],
        out_specs=[row(D_MODEL), row(LANES), pl.BlockSpec((SUBLANES, tm), lambda i: (0, i)),
                   pl.BlockSpec((LANES, LANES), lambda i: (0, 0))],
        out_shape=[jax.ShapeDtypeStruct((n, D_MODEL), F32), jax.ShapeDtypeStruct((n, LANES), F32),
                   jax.ShapeDtypeStruct((SUBLANES, n), F32), jax.ShapeDtypeStruct((LANES, LANES), F32)],
        scratch_shapes=[pltpu.VMEM((LANES, LANES), F32), pltpu.VMEM((tm, tm), BF16)],
        compiler_params=_params(),
        name="out_router",
    )(mix_p, x_p, mix_s, x_s, wout_bf, g2, rw_bf, rb)


def _positions_body(rec_ref, off_ref, pos_ref):
    rec = rec_ref[...]
    tm = rec.shape[1]
    row = lax.broadcasted_iota(jnp.int32, (ROUTER_ROWS, tm), 0).astype(F32)
    row8 = lax.broadcasted_iota(jnp.int32, rec.shape, 0)
    off = off_ref[:, 0:1]
    pos = jnp.zeros(rec.shape, F32)
    for slot, (e_row, r_row) in enumerate(((META_E0, META_R0), (META_E1, META_R1))):
        e = rec[e_row:e_row + 1, :] + ROUTER_LANE0
        seg = jnp.sum(jnp.where(row == e, off, 0.0), axis=0, keepdims=True)
        pos = jnp.where(row8 == slot, seg + rec[r_row:r_row + 1, :], pos)
    pos_ref[...] = pos.astype(jnp.int32)


def _positions(rec_all, off_rows):
    n = rec_all.shape[1]
    tm = n
    return pl.pallas_call(
        _positions_body,
        grid=(n // tm,),
        in_specs=[pl.BlockSpec((SUBLANES, tm), lambda i: (0, i)), _resident((ROUTER_ROWS, LANES))],
        out_specs=pl.BlockSpec((SUBLANES, tm), lambda i: (0, i)),
        out_shape=jax.ShapeDtypeStruct((SUBLANES, n), jnp.int32),
        compiler_params=_params(),
        name="positions",
    )(rec_all, off_rows)


N_LOAD_SLOTS = 3


def _pow2_sizes(n):
    return [1 << b for b in range(n.bit_length() - 1, -1, -1)]


def _dispatch_body(pos_ref, nv_ref, h_ref, hs_ref, hbuf, zbuf, lsem, ssem, zsem, *, n_tok, tm):
    i = pl.program_id(0)
    n_steps = pl.num_programs(0)
    n_tiles = hs_ref.shape[0] // TR

    groups = tm // SUBLANES

    def load(t, slot, sem=lsem):
        return pltpu.make_async_copy(h_ref.at[pl.ds(t * groups, groups)], hbuf.at[slot], sem.at[slot])

    def clear_unowned(start):
        def per_expert(e, c):
            row = nv_ref[e]
            z = nv_ref[N_EXPERTS + e]
            head = z & (SUBLANES - 1)
            for k in range(SUBLANES - 1):
                @pl.when(k < head)
                def _(k=k):
                    cp = pltpu.make_async_copy(zbuf.at[pl.ds(0, 1), :], hs_ref.at[pl.ds(row + k, 1), :], zsem)
                    cp.start() if start else cp.wait()
            row = row + head
            for size in _pow2_sizes(TR):
                if size < SUBLANES:
                    break
                @pl.when((z & size) != 0)
                def _(row=row, size=size):
                    dst = hs_ref.at[pl.ds(pl.multiple_of(row, SUBLANES), size), :]
                    cp = pltpu.make_async_copy(zbuf.at[pl.ds(0, size), :], dst, zsem)
                    cp.start() if start else cp.wait()
                row = row + (z & size)
            return c
        lax.fori_loop(0, N_EXPERTS, per_expert, 0)

        def per_tail_tile(t, c):
            dst = hs_ref.at[pl.ds(pl.multiple_of(t * TR, TR), TR), :]
            cp = pltpu.make_async_copy(zbuf, dst, zsem)
            cp.start() if start else cp.wait()
            return c
        lax.fori_loop(nv_ref[2 * N_EXPERTS], n_tiles, per_tail_tile, 0)

    def scatter_wait(slot):
        for _ in range(2):
            load(0, slot, ssem).wait()

    @pl.when(i == 0)
    def _():
        zbuf[...] = jnp.zeros_like(zbuf)
        clear_unowned(True)
        load(0, 0).start()

    @pl.when(i + 1 < n_steps)
    def _():
        load(i + 1, (i + 1) % N_LOAD_SLOTS).start()

    slot = i % N_LOAD_SLOTS
    load(i, slot).wait()

    def issue(jb, c):
        for u in range(SUBLANES):
            for s in range(2):
                p = pos_ref[s * n_tok + i * tm + jb * SUBLANES + u]
                pltpu.make_async_copy(hbuf.at[slot, jb, pl.ds(u, 1), :], hs_ref.at[pl.ds(p, 1), :],
                                      ssem.at[slot]).start()
        return c
    lax.fori_loop(0, groups, issue, 0)

    @pl.when(i > 0)
    def _():
        scatter_wait((i + N_LOAD_SLOTS - 1) % N_LOAD_SLOTS)

    @pl.when(i == n_steps - 1)
    def _():
        scatter_wait(slot)
        clear_unowned(False)


def _dispatch(pos, nv, rows):
    n_tok, width = rows.shape
    n_tiles = (2 * n_tok) // TR + N_EXPERTS
    tm = TM_PROJ
    assert n_tok % tm == 0
    return pl.pallas_call(
        functools.partial(_dispatch_body, n_tok=n_tok, tm=tm),
        grid_spec=pltpu.PrefetchScalarGridSpec(
            num_scalar_prefetch=2,
            grid=(n_tok // tm,),
            in_specs=[pl.BlockSpec(memory_space=pl.ANY)],
            out_specs=pl.BlockSpec(memory_space=pl.ANY),
            scratch_shapes=[pltpu.VMEM((N_LOAD_SLOTS, tm // SUBLANES, SUBLANES, width), rows.dtype),
                            pltpu.VMEM((TR, width), rows.dtype),
                            pltpu.SemaphoreType.DMA((N_LOAD_SLOTS,)), pltpu.SemaphoreType.DMA((N_LOAD_SLOTS,)),
                            pltpu.SemaphoreType.DMA(())],
        ),
        out_shape=jax.ShapeDtypeStruct((n_tiles * TR, width), rows.dtype),
        compiler_params=_params(),
        name="dispatch",
    )(pos, nv, rows.reshape(n_tok // SUBLANES, SUBLANES, width))


N_MOE_LOADS = 4


def _moe_body(ts_ref, hs_ref, g2_ref, wg_ref, wu_ref, wd_ref, ys_ref, xbuf, obuf, wg_bf, wu_bf, wd_bf, lsem, ssem):
    e = pl.program_id(0)
    n_tiles = ys_ref.shape[0] // TR
    n_used = ts_ref[N_EXPERTS]
    t0, t1 = ts_ref[e], ts_ref[e + 1]

    def load(g, slot):
        return pltpu.make_async_copy(hs_ref.at[pl.ds(pl.multiple_of(g * TR, TR), TR), :], xbuf.at[slot], lsem.at[slot])

    def store(g, slot):
        return pltpu.make_async_copy(obuf.at[slot], ys_ref.at[pl.ds(pl.multiple_of(g * TR, TR), TR), :], ssem.at[slot])

    @pl.when(e == 0)
    def _():
        for g in range(N_MOE_LOADS - 1):
            @pl.when(g < n_used)
            def _(g=g):
                load(g, g).start()

    @pl.when(t1 > t0)
    def _():
        wg_bf[...] = wg_ref[0].astype(BF16)
        wu_bf[...] = wu_ref[0].astype(BF16)
        wd_bf[...] = wd_ref[0].astype(BF16)

    def tile(g, c):
        slot = g % 2
        xslot = g % N_MOE_LOADS
        ahead = g + N_MOE_LOADS - 1

        @pl.when(ahead < n_used)
        def _():
            load(ahead, ahead % N_MOE_LOADS).start()

        load(g, xslot).wait()
        xn = _rmsnorm(xbuf[xslot], g2_ref[...]).astype(BF16)
        hg = _dot(xn, wg_bf[...])
        hu = _dot(xn, wu_bf[...])
        hid = (hg * jax.nn.sigmoid(hg) * hu).astype(BF16)
        y = _dot(hid, wd_bf[...])

        @pl.when(g >= 2)
        def _():
            store(g - 2, slot).wait()

        obuf[slot] = y
        store(g, slot).start()
        return c

    lax.fori_loop(t0, t1, tile, 0)

    @pl.when(e == pl.num_programs(0) - 1)
    def _():
        @pl.when(n_used >= 2)
        def _():
            store(n_used - 2, n_used % 2).wait()
        store(n_used - 1, (n_used - 1) % 2).wait()
        obuf[0] = jnp.zeros(obuf.shape[1:], obuf.dtype)

        def clear(g, c):
            store(g, 0).start()
            return c
        lax.fori_loop(n_used, n_tiles, clear, 0)

        def drain(g, c):
            store(g, 0).wait()
            return c
        lax.fori_loop(n_used, n_tiles, drain, 0)


def _moe(ts, hs, g2, wg, wu, wd):
    wspec = lambda shape: pl.BlockSpec((1,) + shape, lambda e, ts: (e, 0, 0))
    tile_buf = lambda slots: pltpu.VMEM((slots, TR, hs.shape[1]), hs.dtype)
    return pl.pallas_call(
        _moe_body,
        grid_spec=pltpu.PrefetchScalarGridSpec(
            num_scalar_prefetch=1,
            grid=(N_EXPERTS,),
            in_specs=[pl.BlockSpec(memory_space=pl.ANY),
                      pl.BlockSpec((1, D_MODEL), lambda e, ts: (0, 0)),
                      wspec((D_MODEL, D_EXPERT)), wspec((D_MODEL, D_EXPERT)), wspec((D_EXPERT, D_MODEL))],
            out_specs=pl.BlockSpec(memory_space=pl.ANY),
            scratch_shapes=[tile_buf(N_MOE_LOADS), tile_buf(2),
                            pltpu.VMEM((D_MODEL, D_EXPERT), BF16), pltpu.VMEM((D_MODEL, D_EXPERT), BF16),
                            pltpu.VMEM((D_EXPERT, D_MODEL), BF16),
                            pltpu.SemaphoreType.DMA((N_MOE_LOADS,)), pltpu.SemaphoreType.DMA((2,))],
        ),
        out_shape=jax.ShapeDtypeStruct(hs.shape, hs.dtype),
        compiler_params=_params(),
        name="moe",
    )(ts, hs, g2, wg, wu, wd)


def _combine_body(pos_ref, h_ref, meta_ref, g_ref, ys_ref, ys_grouped_ref, op_ref, os_ref, ybuf, sem, *,
                  n_tok, prompt_tiles):
    i = pl.program_id(0)
    n_steps = pl.num_programs(0)
    tm = h_ref.shape[0]

    groups = tm // SUBLANES

    def gather(t, slot):
        def issue(jb, c):
            for u in range(SUBLANES):
                for s in range(2):
                    p = pos_ref[s * n_tok + t * tm + jb * SUBLANES + u]
                    pltpu.make_async_copy(ys_ref.at[pl.ds(p, 1), :], ybuf.at[slot, s, jb, pl.ds(u, 1), :],
                                          sem.at[slot]).start()
            return c
        lax.fori_loop(0, groups, issue, 0)

    @pl.when(i == 0)
    def _():
        gather(0, 0)

    @pl.when(i + 1 < n_steps)
    def _():
        gather(i + 1, (i + 1) % 2)

    slot = i % 2
    for s in range(2):
        pltpu.make_async_copy(ys_grouped_ref.at[pl.ds(0, groups)], ybuf.at[slot, s], sem.at[slot]).wait()
    meta = meta_ref[...]
    g0 = meta[:, META_G0:META_G0 + 1]
    g1 = meta[:, META_G1:META_G1 + 1]
    y0 = ybuf[slot, 0].reshape(tm, D_MODEL)
    y1 = ybuf[slot, 1].reshape(tm, D_MODEL)
    h = h_ref[...] + (g0 * y0 + g1 * y1)
    out = _rmsnorm(h, g_ref[...])

    @pl.when(i < prompt_tiles)
    def _():
        op_ref[...] = out

    @pl.when(i >= prompt_tiles)
    def _():
        os_ref[...] = out


def _combine(pos, h_all, meta_all, ys, gf, n_p):
    n_tok = h_all.shape[0]
    tm = TM_OUT
    assert n_p % tm == 0 and n_tok % tm == 0 and 0 < n_p < n_tok
    tp = n_p // tm
    return pl.pallas_call(
        functools.partial(_combine_body, n_tok=n_tok, prompt_tiles=tp),
        grid_spec=pltpu.PrefetchScalarGridSpec(
            num_scalar_prefetch=1,
            grid=(n_tok // tm,),
            in_specs=[pl.BlockSpec((tm, D_MODEL), lambda i, pos: (i, 0)),
                      pl.BlockSpec((tm, LANES), lambda i, pos: (i, 0)),
                      pl.BlockSpec((1, D_MODEL), lambda i, pos: (0, 0)),
                      pl.BlockSpec(memory_space=pl.ANY), pl.BlockSpec(memory_space=pl.ANY)],
            out_specs=[pl.BlockSpec((tm, D_MODEL), lambda i, pos: (jnp.minimum(i, tp - 1), 0)),
                       pl.BlockSpec((tm, D_MODEL), lambda i, pos: (jnp.maximum(i - tp, 0), 0))],
            scratch_shapes=[pltpu.VMEM((2, 2, tm // SUBLANES, SUBLANES, ys.shape[1]), ys.dtype),
                            pltpu.SemaphoreType.DMA((2,))],
        ),
        out_shape=[jax.ShapeDtypeStruct((n_p, D_MODEL), F32), jax.ShapeDtypeStruct((n_tok - n_p, D_MODEL), F32)],
        compiler_params=_params(),
        name="combine",
    )(pos, h_all, meta_all, gf, ys, ys.reshape(ys.shape[0] // SUBLANES, SUBLANES, ys.shape[1]))


def _routing_tables(counts, n_tok):
    cnt = counts[ROUTER_LANE0:ROUTER_LANE0 + N_EXPERTS].astype(jnp.int32)
    tiles_e = (cnt + TR - 1) // TR
    tile_end = jnp.cumsum(tiles_e)
    tile_start = tile_end - tiles_e
    off_rows = jnp.pad((tile_start * TR).astype(F32), (ROUTER_LANE0, ROUTER_ROWS - ROUTER_LANE0 - N_EXPERTS))
    off_rows = jnp.broadcast_to(off_rows[:, None], (ROUTER_ROWS, LANES))
    ts = jnp.concatenate([tile_start, tile_end[-1:]]).astype(jnp.int32)
    pad_row = tile_start * TR + cnt
    pad_rows = tiles_e * TR - cnt
    pads = jnp.concatenate([pad_row, pad_rows, tile_end[-1:]]).astype(jnp.int32)
    return off_rows, ts, pads


def kernel(x_prompt, x_sample, cache_win_k, cache_win_v, state_conv, norm1_g, w_in, attn_sink, conv_dw_w, conv_dw_b,
           conv_ln_g, conv_ln_b, conv_pw_w, conv_pw_b, w_out, norm2_g, router_group_w, router_group_b,
           router_expert_w, router_expert_b, expert_w_gate, expert_w_up, expert_w_down, final_norm_g):
    depth = w_in.shape[0]
    assert depth == 1, "single-layer step"
    bp, sp, _ = x_prompt.shape
    assert bp == 1, "one prompt sequence"
    n_seq, t_new, _ = x_sample.shape
    n_p, n_s = bp * sp, n_seq * t_new
    n_tok = n_p + n_s
    l = 0

    row = lambda a: a.reshape(1, -1)
    w_in_bf = w_in[l].astype(BF16)
    w_out_bf = w_out[l].astype(BF16)
    pw_bf = conv_pw_w[l].astype(BF16)
    rw = jnp.concatenate([router_group_w[l], router_expert_w[l]], axis=1)
    rw_bf = jnp.pad(rw, ((0, 0), (0, LANES - rw.shape[1]))).astype(BF16)
    rb = jnp.pad(jnp.concatenate([router_group_b[l], router_expert_b[l]]), (0, LANES - rw.shape[1])).reshape(1, LANES)
    g1, g2, gf = row(norm1_g[l]), row(norm2_g[l]), row(final_norm_g)
    sink = attn_sink[l]
    conv_w = (conv_dw_w[l], row(conv_dw_b[l]), row(conv_ln_g[l]), row(conv_ln_b[l]), pw_bf, row(conv_pw_b[l]))

    xp = x_prompt.reshape(n_p, D_MODEL)
    xs = x_sample.reshape(n_s, D_MODEL)
    cache_axes, cache_axes_back = (0, 1, 3, 4, 2), (0, 1, 4, 2, 3)
    ck = jnp.transpose(cache_win_k, cache_axes)
    cv = jnp.transpose(cache_win_v, cache_axes)
    cs = jnp.transpose(state_conv, (0, 2, 1, 3))

    qp, kp, vp, up = _in_proj(xp, g1, w_in_bf)
    qs, ks, vs, us = _in_proj(xs, g1, w_in_bf)
    mix_p = _mixer_prompt(sink, qp, kp, vp, up, *conv_w)
    mix_s, wk_s, wv_s, wc_s = _mixer_sample(sink, qs, ks, vs, us, ck, cv, cs, *conv_w)

    h_all, meta_all, rec_all, cnt = _out_router(mix_p, xp, mix_s, xs, w_out_bf, g2, rw_bf, rb)

    off_rows, ts, nv = _routing_tables(cnt[:, 0], n_tok)
    pos_rows = _positions(rec_all, off_rows)
    pos = jnp.concatenate([pos_rows[0], pos_rows[1]])
    hs = _dispatch(pos, nv, h_all)
    ys = _moe(ts, hs, g2, expert_w_gate[l], expert_w_up[l], expert_w_down[l])
    y_p, y_s = _combine(pos, h_all, meta_all, ys, gf, n_p)

    kv_shape = (depth, bp, WINDOW, N_KV_HEADS, HEAD_DIM)
    return (y_p.reshape(bp, sp, D_MODEL), y_s.reshape(n_seq, t_new, D_MODEL),
            kp[n_p - WINDOW:].reshape(kv_shape), vp[n_p - WINDOW:].reshape(kv_shape),
            up[n_p - (CONV_W - 1):].reshape(depth, bp, CONV_W - 1, D_CONV),
            jnp.transpose(wk_s, cache_axes_back), jnp.transpose(wv_s, cache_axes_back),
            jnp.transpose(wc_s, (0, 2, 1, 3)))
```

```python
import functools

import jax
import jax.numpy as jnp
from jax import lax
from jax.experimental import pallas as pl
from jax.experimental.pallas import tpu as pltpu

F32 = jnp.float32
BF16 = jnp.bfloat16

D_MODEL = 2048
HEAD_DIM = 64
N_HEADS = 16
N_KV_HEADS = 2
GQA_GROUP = 8
KV_DIM = N_KV_HEADS * HEAD_DIM
D_ATTN = N_HEADS * HEAD_DIM
D_CONV = D_MODEL - D_ATTN
WINDOW = 128
CONV_W = 31
D_IN = D_ATTN + 2 * KV_DIM + 2 * D_CONV
N_GROUPS = 4
EXPERTS_PER_GROUP = 8
N_EXPERTS = N_GROUPS * EXPERTS_PER_GROUP
D_EXPERT = 256
RMS_EPS = 1e-6
LN_EPS = 1e-5

LANES = 128
SUBLANES = 8
ROUTER_LANE0 = N_GROUPS
VMEM_LIMIT = 56 * 1024 * 1024

TM_IN = 1024
TM_PROJ = 512
TM_MIX = 512
SEQ_TILE = 8
TR = 128
TM_OUT = 256

_NT = (((1,), (1,)), ((), ()))


def _params(n_axes=1):
    return pltpu.CompilerParams(dimension_semantics=("arbitrary",) * n_axes, vmem_limit_bytes=VMEM_LIMIT)


def _resident(shape):
    return pl.BlockSpec(shape, lambda *_: (0,) * len(shape), pipeline_mode=pl.Buffered(1))


def _rmsnorm(x, g):
    ms = jnp.mean(x * x, axis=-1, keepdims=True)
    return x * lax.rsqrt(ms + RMS_EPS) * g


def _dot(a, b):
    return jnp.dot(a, b, preferred_element_type=F32)


def _dot_nt(a, b):
    return lax.dot_general(a, b, _NT, preferred_element_type=F32)


def _in_proj_body(x_ref, g_ref, w_ref, q_ref, k_ref, v_ref, u_ref):
    xn = _rmsnorm(x_ref[...], g_ref[...]).astype(BF16)
    q_ref[...] = (_dot(xn, w_ref[:, :D_ATTN]) * (HEAD_DIM ** -0.5)).astype(BF16)
    kv = _dot(xn, w_ref[:, D_ATTN:D_ATTN + 2 * KV_DIM])
    k_ref[...] = kv[:, :KV_DIM]
    v_ref[...] = kv[:, KV_DIM:]
    c0 = D_ATTN + 2 * KV_DIM
    cw = 256
    for j in range(D_CONV // cw):
        a = _dot(xn, w_ref[:, c0 + j * cw:c0 + (j + 1) * cw])
        b = _dot(xn, w_ref[:, c0 + D_CONV + j * cw:c0 + D_CONV + (j + 1) * cw])
        u_ref[:, j * cw:(j + 1) * cw] = a * jax.nn.sigmoid(b)


def _in_proj(x, g, w_bf):
    n = x.shape[0]
    tm = min(TM_IN, n)
    row = lambda w: pl.BlockSpec((tm, w), lambda i: (i, 0))
    return pl.pallas_call(
        _in_proj_body,
        grid=(n // tm,),
        in_specs=[row(D_MODEL), _resident((1, D_MODEL)), _resident((D_MODEL, D_IN))],
        out_specs=[row(D_ATTN), row(KV_DIM), row(KV_DIM), row(D_CONV)],
        out_shape=[jax.ShapeDtypeStruct((n, D_ATTN), BF16), jax.ShapeDtypeStruct((n, KV_DIM), F32),
                   jax.ShapeDtypeStruct((n, KV_DIM), F32), jax.ShapeDtypeStruct((n, D_CONV), F32)],
        compiler_params=_params(),
        name="in_proj",
    )(x, g, w_bf)


def _attend(q, parts, sink_ref, kv_head, keys_on_lanes=False):
    t = q.shape[0]
    score, weigh = (_dot, _dot_nt) if keys_on_lanes else (_dot_nt, _dot)
    qs = jnp.concatenate([q[:, g * HEAD_DIM:(g + 1) * HEAD_DIM] for g in range(GQA_GROUP)], axis=0)
    scores = [score(qs, kk) for kk, _, _ in parts]
    hidden = [jnp.where(mask, 0.0, -jnp.inf) for _, _, mask in parts]
    ps = [[] for _ in parts]
    inv = []
    for g in range(GQA_GROUP):
        sg = [s[g * t:(g + 1) * t] + neg for s, neg in zip(scores, hidden)]
        sk = sink_ref[kv_head * GQA_GROUP + g]
        m = sk
        for x in sg:
            m = jnp.maximum(jnp.max(x, axis=-1, keepdims=True), m)
        den = jnp.exp(sk - m)
        for k, x in enumerate(sg):
            p = jnp.exp(x - m)
            den = den + jnp.sum(p, axis=-1, keepdims=True)
            ps[k].append(p.astype(BF16))
        inv.append(1.0 / den)
    o = None
    for k, (_, vv, _) in enumerate(parts):
        ok = weigh(jnp.concatenate(ps[k], axis=0), vv)
        o = ok if o is None else o + ok
    return jnp.concatenate([o[g * t:(g + 1) * t] * inv[g] for g in range(GQA_GROUP)], axis=1)


def _conv_tail(y, dwb_ref, lng_ref, lnb_ref, pw_ref, pwb_ref):
    y = y + dwb_ref[...]
    mu = jnp.mean(y, axis=-1, keepdims=True)
    yc = y - mu
    yn = yc * lax.rsqrt(jnp.mean(yc * yc, axis=-1, keepdims=True) + LN_EPS)
    yn = yn * lng_ref[...] + lnb_ref[...]
    act = yn * jax.nn.sigmoid(yn)
    return _dot(act.astype(BF16), pw_ref[...]) + pwb_ref[...]


CONV_PAD = 32
CONV_ROWS = 128
CONV_COLS = 128


def _mixer_prompt_body(sink_ref, q_ref, k_ref, v_ref, u_ref, dww_ref, dwb_ref, lng_ref, lnb_ref, pw_ref, pwb_ref,
                       mix_ref, kprev, vprev, uext, ushift, ybuf):
    i = pl.program_id(0)
    tm = q_ref.shape[0]

    @pl.when(i == 0)
    def _():
        kprev[...] = jnp.zeros_like(kprev)
        vprev[...] = jnp.zeros_like(vprev)
        uext[0:CONV_PAD, :] = jnp.zeros((CONV_PAD, D_CONV), F32)

    qi = lax.broadcasted_iota(jnp.int32, (WINDOW, 2 * WINDOW), 0)
    kj = lax.broadcasted_iota(jnp.int32, (WINDOW, 2 * WINDOW), 1)
    diff = qi + WINDOW - kj
    band = (diff >= 0) & (diff <= WINDOW)
    kp, vp = kprev[...], vprev[...]
    for b in range(tm // WINDOW):
        rows = slice(b * WINDOW, (b + 1) * WINDOW)
        kb = k_ref[rows, :].astype(BF16)
        vb = v_ref[rows, :].astype(BF16)
        kk = jnp.concatenate([kp, kb], axis=0)
        vv = jnp.concatenate([vp, vb], axis=0)
        if b == 0:
            mask = band & (kj >= jnp.where(i > 0, 0, WINDOW))
        else:
            mask = band
        for h in range(N_KV_HEADS):
            cols = slice(h * HEAD_DIM, (h + 1) * HEAD_DIM)
            hq = slice(h * GQA_GROUP * HEAD_DIM, (h + 1) * GQA_GROUP * HEAD_DIM)
            o = _attend(q_ref[rows, hq], [(kk[:, cols], vv[:, cols], mask)], sink_ref, h)
            mix_ref[rows, hq] = o.astype(BF16)
        kp, vp = kb, vb
    kprev[...] = kp
    vprev[...] = vp

    uext[CONV_PAD:CONV_PAD + tm, :] = u_ref[...]
    n_shift_rows = ushift.shape[1]
    for s in range(1, SUBLANES):
        ushift[s - 1] = uext[s:s + n_shift_rows, :]
    off = CONV_PAD - (CONV_W - 1)
    for r in range(tm // CONV_ROWS):
        for c in range(D_CONV // CONV_COLS):
            cs = slice(c * CONV_COLS, (c + 1) * CONV_COLS)
            acc = jnp.zeros((CONV_ROWS, CONV_COLS), F32)
            for j in range(CONV_W):
                a, s = divmod(off + j, SUBLANES)
                r0 = r * CONV_ROWS + a * SUBLANES
                src = uext[r0:r0 + CONV_ROWS, cs] if s == 0 else ushift[s - 1, r0:r0 + CONV_ROWS, cs]
                acc = acc + dww_ref[j:j + 1, cs] * src
            ybuf[r * CONV_ROWS:(r + 1) * CONV_ROWS, cs] = acc
    uext[0:CONV_PAD, :] = uext[tm:tm + CONV_PAD, :]
    conv = _conv_tail(ybuf[...], dwb_ref, lng_ref, lnb_ref, pw_ref, pwb_ref)
    mix_ref[:, D_ATTN:] = conv.astype(BF16)


def _mixer_prompt(sink, q, k, v, u, dww, dwb, lng, lnb, pw_bf, pwb):
    n = q.shape[0]
    tm = TM_MIX
    row = lambda w: pl.BlockSpec((tm, w), lambda i, s: (i, 0))
    res = lambda shape: pl.BlockSpec(shape, lambda i, s: (0,) * len(shape), pipeline_mode=pl.Buffered(1))
    return pl.pallas_call(
        _mixer_prompt_body,
        grid_spec=pltpu.PrefetchScalarGridSpec(
            num_scalar_prefetch=1,
            grid=(n // tm,),
            in_specs=[row(D_ATTN), row(KV_DIM), row(KV_DIM), row(D_CONV), res((CONV_W, D_CONV)), res((1, D_CONV)),
                      res((1, D_CONV)), res((1, D_CONV)), res((D_CONV, D_CONV)), res((1, D_CONV))],
            out_specs=row(D_MODEL),
            scratch_shapes=[pltpu.VMEM((WINDOW, KV_DIM), BF16), pltpu.VMEM((WINDOW, KV_DIM), BF16),
                            pltpu.VMEM((CONV_PAD + tm, D_CONV), F32),
                            pltpu.VMEM((SUBLANES - 1, CONV_PAD - SUBLANES + tm, D_CONV), F32),
                            pltpu.VMEM((tm, D_CONV), F32)],
        ),
        out_shape=jax.ShapeDtypeStruct((n, D_MODEL), BF16),
        compiler_params=_params(),
        name="mixer_prompt",
    )(sink, q, k, v, u, dww, dwb, lng, lnb, pw_bf, pwb)


def _mixer_sample_body(sink_ref, q_ref, k_ref, v_ref, u_ref, ck_ref, cv_ref, cs_ref, dww_ref, dwb_ref, lng_ref,
                       lnb_ref, pw_ref, pwb_ref, mix_ref, wk_ref, wv_ref, wc_ref, usel, ybuf):
    nt = q_ref.shape[0]
    sb = ck_ref.shape[0]
    t_new = nt // sb
    n_cache = sb * WINDOW
    n_state = CONV_W - 1

    pad = jnp.zeros((LANES - nt, KV_DIM), F32)
    kn_t = jnp.concatenate([k_ref[...], pad], axis=0).T
    vn_t = jnp.concatenate([v_ref[...], pad], axis=0).T

    lt, lw = t_new.bit_length() - 1, WINDOW.bit_length() - 1
    qr = lax.broadcasted_iota(jnp.int32, (nt, n_cache), 0)
    cc = lax.broadcasted_iota(jnp.int32, (nt, n_cache), 1)
    mask_c = ((cc >> lw) == (qr >> lt)) & ((cc & (WINDOW - 1)) >= (qr & (t_new - 1)))
    qr = lax.broadcasted_iota(jnp.int32, (nt, LANES), 0)
    cn = lax.broadcasted_iota(jnp.int32, (nt, LANES), 1)
    mask_n = ((cn >> lt) == (qr >> lt)) & ((cn & (t_new - 1)) <= (qr & (t_new - 1)))
    for h in range(N_KV_HEADS):
        dims = slice(h * HEAD_DIM, (h + 1) * HEAD_DIM)
        hq = slice(h * GQA_GROUP * HEAD_DIM, (h + 1) * GQA_GROUP * HEAD_DIM)
        kc = jnp.concatenate([ck_ref[s, h] for s in range(sb)], axis=1).astype(BF16)
        vc = jnp.concatenate([cv_ref[s, h] for s in range(sb)], axis=1).astype(BF16)
        parts = [(kc, vc, mask_c), (kn_t[dims, :].astype(BF16), vn_t[dims, :].astype(BF16), mask_n)]
        o = _attend(q_ref[:, hq], parts, sink_ref, h, keys_on_lanes=True)
        mix_ref[:, hq] = o.astype(BF16)

    is_new = lax.broadcasted_iota(jnp.int32, (HEAD_DIM, WINDOW), 1) >= WINDOW - t_new
    for s in range(sb):
        shift = (WINDOW - t_new - s * t_new) % LANES
        kn_s = pltpu.roll(kn_t, shift=shift, axis=1)
        vn_s = pltpu.roll(vn_t, shift=shift, axis=1)
        for h in range(N_KV_HEADS):
            dims = slice(h * HEAD_DIM, (h + 1) * HEAD_DIM)
            old_k = pltpu.roll(ck_ref[s, h], shift=WINDOW - t_new, axis=1)
            old_v = pltpu.roll(cv_ref[s, h], shift=WINDOW - t_new, axis=1)
            wk_ref[s, h] = jnp.where(is_new, kn_s[dims, :], old_k)
            wv_ref[s, h] = jnp.where(is_new, vn_s[dims, :], old_v)

    for s in range(sb):
        for t in range(t_new):
            usel[t, s:s + 1, :] = u_ref[s * t_new + t:s * t_new + t + 1, :]
    u_new = [usel[t] for t in range(t_new)]
    for t in range(t_new):
        acc = jnp.zeros((sb, D_CONV), F32)
        for r in range(t, n_state):
            acc = acc + dww_ref[r - t:r - t + 1, :] * cs_ref[r]
        for t2 in range(t + 1):
            j = n_state - t + t2
            acc = acc + dww_ref[j:j + 1, :] * u_new[t2]
        for s in range(sb):
            ybuf[s * t_new + t:s * t_new + t + 1, :] = acc[s:s + 1, :]
        wc_ref[n_state - t_new + t] = u_new[t]
    wc_ref[0:n_state - t_new] = cs_ref[t_new:n_state]
    conv = _conv_tail(ybuf[...], dwb_ref, lng_ref, lnb_ref, pw_ref, pwb_ref)
    mix_ref[:, D_ATTN:] = conv.astype(BF16)


def _mixer_sample(sink, q, k, v, u, ck, cv, cs, dww, dwb, lng, lnb, pw_bf, pwb):
    n = q.shape[0]
    n_seq = ck.shape[1]
    t_new = n // n_seq
    sb = SEQ_TILE
    nt = sb * t_new
    n_state = CONV_W - 1
    row = lambda w: pl.BlockSpec((nt, w), lambda i, s: (i, 0))
    cache = pl.BlockSpec((None, sb, N_KV_HEADS, HEAD_DIM, WINDOW), lambda i, s: (0, i, 0, 0, 0))
    state = pl.BlockSpec((None, n_state, sb, D_CONV), lambda i, s: (0, 0, i, 0))
    res = lambda shape: pl.BlockSpec(shape, lambda i, s: (0,) * len(shape), pipeline_mode=pl.Buffered(1))
    return pl.pallas_call(
        _mixer_sample_body,
        grid_spec=pltpu.PrefetchScalarGridSpec(
            num_scalar_prefetch=1,
            grid=(n_seq // sb,),
            in_specs=[row(D_ATTN), row(KV_DIM), row(KV_DIM), row(D_CONV), cache, cache, state,
                      res((CONV_W, D_CONV)), res((1, D_CONV)), res((1, D_CONV)), res((1, D_CONV)),
                      res((D_CONV, D_CONV)), res((1, D_CONV))],
            out_specs=[row(D_MODEL), cache, cache, state],
            scratch_shapes=[pltpu.VMEM((t_new, sb, D_CONV), F32), pltpu.VMEM((nt, D_CONV), F32)],
        ),
        out_shape=[jax.ShapeDtypeStruct((n, D_MODEL), BF16), jax.ShapeDtypeStruct(ck.shape, F32),
                   jax.ShapeDtypeStruct(cv.shape, F32), jax.ShapeDtypeStruct(cs.shape, F32)],
        compiler_params=_params(),
        name="mixer_sample",
    )(sink, q, k, v, u, ck, cv, cs, dww, dwb, lng, lnb, pw_bf, pwb)


META_E0, META_E1, META_G0, META_G1, META_R0, META_R1 = range(6)


ROUTER_ROWS = 40


def _out_router_body(mixp_ref, xp_ref, mixs_ref, xs_ref, wout_ref, g2_ref, rw_ref, rb_ref, h_ref, meta_ref, rec_ref,
                     cnt_ref, carry, earlier, *, prompt_tiles):
    i = pl.program_id(0)
    tm = xp_ref.shape[0]

    @pl.when(i == 0)
    def _():
        carry[...] = jnp.zeros_like(carry)
        ti = lax.broadcasted_iota(jnp.int32, (tm, tm), 0)
        tj = lax.broadcasted_iota(jnp.int32, (tm, tm), 1)
        earlier[...] = jnp.where(ti < tj, 1.0, 0.0).astype(BF16)

    is_prompt = i < prompt_tiles
    x = jnp.where(is_prompt, xp_ref[...], xs_ref[...])
    mix = jnp.where(is_prompt, mixp_ref[...], mixs_ref[...])
    h = x + _dot(mix, wout_ref[...])
    h_ref[...] = h
    xn = _rmsnorm(h, g2_ref[...]).astype(BF16)
    logits = _dot(xn, rw_ref[...]) + rb_ref[...]
    lt = logits.T[:ROUTER_ROWS, :]
    row = lax.broadcasted_iota(jnp.int32, lt.shape, 0).astype(F32)
    first = lambda cond: jnp.min(jnp.where(cond, row, float(LANES)), axis=0, keepdims=True)

    gl = jnp.where(row < N_GROUPS, lt, -jnp.inf)
    gmax = jnp.max(gl, axis=0, keepdims=True)
    gidx = first(gl == gmax)
    gval = 1.0 / jnp.sum(jnp.exp(gl - gmax), axis=0, keepdims=True)

    lo = ROUTER_LANE0 + EXPERTS_PER_GROUP * gidx
    el = jnp.where((row >= lo) & (row < lo + EXPERTS_PER_GROUP), lt, -jnp.inf)
    m1 = jnp.max(el, axis=0, keepdims=True)
    i1 = first(el == m1)
    el2 = jnp.where(row == i1, -jnp.inf, el)
    m2 = jnp.max(el2, axis=0, keepdims=True)
    i2 = first(el2 == m2)
    r = jnp.exp(m2 - m1)
    g0 = gval * (1.0 / (1.0 + r))
    g1 = gval * (r / (1.0 + r))

    sel0, sel1 = row == i1, row == i2
    onehot = jnp.where(sel0 | sel1, 1.0, 0.0)
    before = _dot(onehot.astype(BF16), earlier[...]) + carry[0:ROUTER_ROWS, 0:1]
    r0 = jnp.sum(jnp.where(sel0, before, 0.0), axis=0, keepdims=True)
    r1 = jnp.sum(jnp.where(sel1, before, 0.0), axis=0, keepdims=True)
    carry[0:ROUTER_ROWS, :] = carry[0:ROUTER_ROWS, :] + jnp.sum(onehot, axis=1, keepdims=True)
    cnt_ref[...] = carry[...]

    row8 = lax.broadcasted_iota(jnp.int32, (SUBLANES, tm), 0)
    rec = jnp.zeros((SUBLANES, tm), F32)
    for slot, val in ((META_E0, i1 - ROUTER_LANE0), (META_E1, i2 - ROUTER_LANE0), (META_G0, g0), (META_G1, g1),
                      (META_R0, r0), (META_R1, r1)):
        rec = jnp.where(row8 == slot, val, rec)
    rec_ref[...] = rec
    meta_ref[...] = jnp.concatenate([rec, jnp.zeros((LANES - SUBLANES, tm), F32)], axis=0).T


def _out_router(mix_p, x_p, mix_s, x_s, wout_bf, g2, rw_bf, rb):
    n_p, n_s = x_p.shape[0], x_s.shape[0]
    tm = TM_PROJ
    assert n_p % tm == 0 and n_s % tm == 0
    tp, ts = n_p // tm, n_s // tm
    n = n_p + n_s
    prow = lambda w: pl.BlockSpec((tm, w), lambda i: (jnp.minimum(i, tp - 1), 0))
    srow = lambda w: pl.BlockSpec((tm, w), lambda i: (jnp.maximum(i - tp, 0), 0))
    row = lambda w: pl.BlockSpec((tm, w), lambda i: (i, 0))
    return pl.pallas_call(
        functools.partial(_out_router_body, prompt_tiles=tp),
        grid=(tp + ts,),
        in_specs=[prow(D_MODEL), prow(D_MODEL), srow(D_MODEL), srow(D_MODEL), _resident((D_MODEL, D_MODEL)),
                  _resident((1, D_MODEL)), _resident((D_MODEL, LANES)), _resident((1, LANES))],
        out_specs=[row(D_MODEL), row(LANES), pl.BlockSpec((SUBLANES, tm), lambda i: (0, i)),
                   pl.BlockSpec((LANES, LANES), lambda i: (0, 0))],
        out_shape=[jax.ShapeDtypeStruct((n, D_MODEL), F32), jax.ShapeDtypeStruct((n, LANES), F32),
                   jax.ShapeDtypeStruct((SUBLANES, n), F32), jax.ShapeDtypeStruct((LANES, LANES), F32)],
        scratch_shapes=[pltpu.VMEM((LANES, LANES), F32), pltpu.VMEM((tm, tm), BF16)],
        compiler_params=_params(),
        name="out_router",
    )(mix_p, x_p, mix_s, x_s, wout_bf, g2, rw_bf, rb)


def _positions_body(rec_ref, off_ref, pos_ref):
    rec = rec_ref[...]
    tm = rec.shape[1]
    row = lax.broadcasted_iota(jnp.int32, (ROUTER_ROWS, tm), 0).astype(F32)
    row8 = lax.broadcasted_iota(jnp.int32, rec.shape, 0)
    off = off_ref[:, 0:1]
    pos = jnp.zeros(rec.shape, F32)
    for slot, (e_row, r_row) in enumerate(((META_E0, META_R0), (META_E1, META_R1))):
        e = rec[e_row:e_row + 1, :] + ROUTER_LANE0
        seg = jnp.sum(jnp.where(row == e, off, 0.0), axis=0, keepdims=True)
        pos = jnp.where(row8 == slot, seg + rec[r_row:r_row + 1, :], pos)
    pos_ref[...] = pos.astype(jnp.int32)


def _positions(rec_all, off_rows):
    n = rec_all.shape[1]
    tm = n
    return pl.pallas_call(
        _positions_body,
        grid=(n // tm,),
        in_specs=[pl.BlockSpec((SUBLANES, tm), lambda i: (0, i)), _resident((ROUTER_ROWS, LANES))],
        out_specs=pl.BlockSpec((SUBLANES, tm), lambda i: (0, i)),
        out_shape=jax.ShapeDtypeStruct((SUBLANES, n), jnp.int32),
        compiler_params=_params(),
        name="positions",
    )(rec_all, off_rows)


N_LOAD_SLOTS = 3


def _pow2_sizes(n):
    return [1 << b for b in range(n.bit_length() - 1, -1, -1)]


def _dispatch_body(pos_ref, nv_ref, h_ref, hs_ref, hbuf, zbuf, lsem, ssem, zsem, *, n_tok, tm):
    i = pl.program_id(0)
    n_steps = pl.num_programs(0)
    n_tiles = hs_ref.shape[0] // TR

    groups = tm // SUBLANES

    def load(t, slot, sem=lsem):
        return pltpu.make_async_copy(h_ref.at[pl.ds(t * groups, groups)], hbuf.at[slot], sem.at[slot])

    def clear_unowned(start):
        def per_expert(e, c):
            row = nv_ref[e]
            z = nv_ref[N_EXPERTS + e]
            head = z & (SUBLANES - 1)
            for k in range(SUBLANES - 1):
                @pl.when(k < head)
                def _(k=k):
                    cp = pltpu.make_async_copy(zbuf.at[pl.ds(0, 1), :], hs_ref.at[pl.ds(row + k, 1), :], zsem)
                    cp.start() if start else cp.wait()
            row = row + head
            for size in _pow2_sizes(TR):
                if size < SUBLANES:
                    break
                @pl.when((z & size) != 0)
                def _(row=row, size=size):
                    dst = hs_ref.at[pl.ds(pl.multiple_of(row, SUBLANES), size), :]
                    cp = pltpu.make_async_copy(zbuf.at[pl.ds(0, size), :], dst, zsem)
                    cp.start() if start else cp.wait()
                row = row + (z & size)
            return c
        lax.fori_loop(0, N_EXPERTS, per_expert, 0)

        def per_tail_tile(t, c):
            dst = hs_ref.at[pl.ds(pl.multiple_of(t * TR, TR), TR), :]
            cp = pltpu.make_async_copy(zbuf, dst, zsem)
            cp.start() if start else cp.wait()
            return c
        lax.fori_loop(nv_ref[2 * N_EXPERTS], n_tiles, per_tail_tile, 0)

    def scatter_wait(slot):
        for _ in range(2):
            load(0, slot, ssem).wait()

    @pl.when(i == 0)
    def _():
        zbuf[...] = jnp.zeros_like(zbuf)
        clear_unowned(True)
        load(0, 0).start()

    @pl.when(i + 1 < n_steps)
    def _():
        load(i + 1, (i + 1) % N_LOAD_SLOTS).start()

    slot = i % N_LOAD_SLOTS
    load(i, slot).wait()

    def issue(jb, c):
        for u in range(SUBLANES):
            for s in range(2):
                p = pos_ref[s * n_tok + i * tm + jb * SUBLANES + u]
                pltpu.make_async_copy(hbuf.at[slot, jb, pl.ds(u, 1), :], hs_ref.at[pl.ds(p, 1), :],
                                      ssem.at[slot]).start(priority=s)
        return c
    lax.fori_loop(0, groups, issue, 0)

    @pl.when(i > 0)
    def _():
        scatter_wait((i + N_LOAD_SLOTS - 1) % N_LOAD_SLOTS)

    @pl.when(i == n_steps - 1)
    def _():
        scatter_wait(slot)
        clear_unowned(False)


def _dispatch(pos, nv, rows):
    n_tok, width = rows.shape
    n_tiles = (2 * n_tok) // TR + N_EXPERTS
    tm = TM_PROJ
    assert n_tok % tm == 0
    return pl.pallas_call(
        functools.partial(_dispatch_body, n_tok=n_tok, tm=tm),
        grid_spec=pltpu.PrefetchScalarGridSpec(
            num_scalar_prefetch=2,
            grid=(n_tok // tm,),
            in_specs=[pl.BlockSpec(memory_space=pl.ANY)],
            out_specs=pl.BlockSpec(memory_space=pl.ANY),
            scratch_shapes=[pltpu.VMEM((N_LOAD_SLOTS, tm // SUBLANES, SUBLANES, width), rows.dtype),
                            pltpu.VMEM((TR, width), rows.dtype),
                            pltpu.SemaphoreType.DMA((N_LOAD_SLOTS,)), pltpu.SemaphoreType.DMA((N_LOAD_SLOTS,)),
                            pltpu.SemaphoreType.DMA(())],
        ),
        out_shape=jax.ShapeDtypeStruct((n_tiles * TR, width), rows.dtype),
        compiler_params=_params(),
        name="dispatch",
    )(pos, nv, rows.reshape(n_tok // SUBLANES, SUBLANES, width))


N_MOE_LOADS = 4


def _moe_body(ts_ref, hs_ref, g2_ref, wg_ref, wu_ref, wd_ref, ys_ref, xbuf, obuf, wg_bf, wu_bf, wd_bf, lsem, ssem):
    e = pl.program_id(0)
    n_tiles = ys_ref.shape[0] // TR
    n_used = ts_ref[N_EXPERTS]
    t0, t1 = ts_ref[e], ts_ref[e + 1]

    def load(g, slot):
        return pltpu.make_async_copy(hs_ref.at[pl.ds(pl.multiple_of(g * TR, TR), TR), :], xbuf.at[slot], lsem.at[slot])

    def store(g, slot):
        return pltpu.make_async_copy(obuf.at[slot], ys_ref.at[pl.ds(pl.multiple_of(g * TR, TR), TR), :], ssem.at[slot])

    @pl.when(e == 0)
    def _():
        for g in range(N_MOE_LOADS - 1):
            @pl.when(g < n_used)
            def _(g=g):
                load(g, g).start()

    @pl.when(t1 > t0)
    def _():
        wg_bf[...] = wg_ref[0].astype(BF16)
        wu_bf[...] = wu_ref[0].astype(BF16)
        wd_bf[...] = wd_ref[0].astype(BF16)

    def tile(g, c):
        slot = g % 2
        xslot = g % N_MOE_LOADS
        ahead = g + N_MOE_LOADS - 1

        @pl.when(ahead < n_used)
        def _():
            load(ahead, ahead % N_MOE_LOADS).start()

        load(g, xslot).wait()
        xn = _rmsnorm(xbuf[xslot], g2_ref[...]).astype(BF16)
        hg = _dot(xn, wg_bf[...])
        hu = _dot(xn, wu_bf[...])
        hid = (hg * jax.nn.sigmoid(hg) * hu).astype(BF16)
        y = _dot(hid, wd_bf[...])

        @pl.when(g >= 2)
        def _():
            store(g - 2, slot).wait()

        obuf[slot] = y
        store(g, slot).start()
        return c

    lax.fori_loop(t0, t1, tile, 0)

    @pl.when(e == pl.num_programs(0) - 1)
    def _():
        @pl.when(n_used >= 2)
        def _():
            store(n_used - 2, n_used % 2).wait()
        store(n_used - 1, (n_used - 1) % 2).wait()
        obuf[0] = jnp.zeros(obuf.shape[1:], obuf.dtype)

        def clear(g, c):
            store(g, 0).start()
            return c
        lax.fori_loop(n_used, n_tiles, clear, 0)

        def drain(g, c):
            store(g, 0).wait()
            return c
        lax.fori_loop(n_used, n_tiles, drain, 0)


def _moe(ts, hs, g2, wg, wu, wd):
    wspec = lambda shape: pl.BlockSpec((1,) + shape, lambda e, ts: (e, 0, 0))
    tile_buf = lambda slots: pltpu.VMEM((slots, TR, hs.shape[1]), hs.dtype)
    return pl.pallas_call(
        _moe_body,
        grid_spec=pltpu.PrefetchScalarGridSpec(
            num_scalar_prefetch=1,
            grid=(N_EXPERTS,),
            in_specs=[pl.BlockSpec(memory_space=pl.ANY),
                      pl.BlockSpec((1, D_MODEL), lambda e, ts: (0, 0)),
                      wspec((D_MODEL, D_EXPERT)), wspec((D_MODEL, D_EXPERT)), wspec((D_EXPERT, D_MODEL))],
            out_specs=pl.BlockSpec(memory_space=pl.ANY),
            scratch_shapes=[tile_buf(N_MOE_LOADS), tile_buf(2),
                            pltpu.VMEM((D_MODEL, D_EXPERT), BF16), pltpu.VMEM((D_MODEL, D_EXPERT), BF16),
                            pltpu.VMEM((D_EXPERT, D_MODEL), BF16),
                            pltpu.SemaphoreType.DMA((N_MOE_LOADS,)), pltpu.SemaphoreType.DMA((2,))],
        ),
        out_shape=jax.ShapeDtypeStruct(hs.shape, hs.dtype),
        compiler_params=_params(),
        name="moe",
    )(ts, hs, g2, wg, wu, wd)


def _combine_body(pos_ref, h_ref, meta_ref, g_ref, ys_ref, ys_grouped_ref, op_ref, os_ref, ybuf, sem, *,
                  n_tok, prompt_tiles):
    i = pl.program_id(0)
    n_steps = pl.num_programs(0)
    tm = h_ref.shape[0]

    groups = tm // SUBLANES

    def gather(t, slot):
        def issue(jb, c):
            for u in range(SUBLANES):
                for s in range(2):
                    p = pos_ref[s * n_tok + t * tm + jb * SUBLANES + u]
                    pltpu.make_async_copy(ys_ref.at[pl.ds(p, 1), :], ybuf.at[slot, s, jb, pl.ds(u, 1), :],
                                          sem.at[slot]).start(priority=s)
            return c
        lax.fori_loop(0, groups, issue, 0)

    @pl.when(i == 0)
    def _():
        gather(0, 0)

    @pl.when(i + 1 < n_steps)
    def _():
        gather(i + 1, (i + 1) % 2)

    slot = i % 2
    for s in range(2):
        pltpu.make_async_copy(ys_grouped_ref.at[pl.ds(0, groups)], ybuf.at[slot, s], sem.at[slot]).wait()
    meta = meta_ref[...]
    g0 = meta[:, META_G0:META_G0 + 1]
    g1 = meta[:, META_G1:META_G1 + 1]
    y0 = ybuf[slot, 0].reshape(tm, D_MODEL)
    y1 = ybuf[slot, 1].reshape(tm, D_MODEL)
    h = h_ref[...] + (g0 * y0 + g1 * y1)
    out = _rmsnorm(h, g_ref[...])

    @pl.when(i < prompt_tiles)
    def _():
        op_ref[...] = out

    @pl.when(i >= prompt_tiles)
    def _():
        os_ref[...] = out


def _combine(pos, h_all, meta_all, ys, gf, n_p):
    n_tok = h_all.shape[0]
    tm = TM_OUT
    assert n_p % tm == 0 and n_tok % tm == 0 and 0 < n_p < n_tok
    tp = n_p // tm
    return pl.pallas_call(
        functools.partial(_combine_body, n_tok=n_tok, prompt_tiles=tp),
        grid_spec=pltpu.PrefetchScalarGridSpec(
            num_scalar_prefetch=1,
            grid=(n_tok // tm,),
            in_specs=[pl.BlockSpec((tm, D_MODEL), lambda i, pos: (i, 0)),
                      pl.BlockSpec((tm, LANES), lambda i, pos: (i, 0)),
                      pl.BlockSpec((1, D_MODEL), lambda i, pos: (0, 0)),
                      pl.BlockSpec(memory_space=pl.ANY), pl.BlockSpec(memory_space=pl.ANY)],
            out_specs=[pl.BlockSpec((tm, D_MODEL), lambda i, pos: (jnp.minimum(i, tp - 1), 0)),
                       pl.BlockSpec((tm, D_MODEL), lambda i, pos: (jnp.maximum(i - tp, 0), 0))],
            scratch_shapes=[pltpu.VMEM((2, 2, tm // SUBLANES, SUBLANES, ys.shape[1]), ys.dtype),
                            pltpu.SemaphoreType.DMA((2,))],
        ),
        out_shape=[jax.ShapeDtypeStruct((n_p, D_MODEL), F32), jax.ShapeDtypeStruct((n_tok - n_p, D_MODEL), F32)],
        compiler_params=_params(),
        name="combine",
    )(pos, h_all, meta_all, gf, ys, ys.reshape(ys.shape[0] // SUBLANES, SUBLANES, ys.shape[1]))


def _routing_tables(counts, n_tok):
    cnt = counts[ROUTER_LANE0:ROUTER_LANE0 + N_EXPERTS].astype(jnp.int32)
    tiles_e = (cnt + TR - 1) // TR
    tile_end = jnp.cumsum(tiles_e)
    tile_start = tile_end - tiles_e
    off_rows = jnp.pad((tile_start * TR).astype(F32), (ROUTER_LANE0, ROUTER_ROWS - ROUTER_LANE0 - N_EXPERTS))
    off_rows = jnp.broadcast_to(off_rows[:, None], (ROUTER_ROWS, LANES))
    ts = jnp.concatenate([tile_start, tile_end[-1:]]).astype(jnp.int32)
    pad_row = tile_start * TR + cnt
    pad_rows = tiles_e * TR - cnt
    pads = jnp.concatenate([pad_row, pad_rows, tile_end[-1:]]).astype(jnp.int32)
    return off_rows, ts, pads


def kernel(x_prompt, x_sample, cache_win_k, cache_win_v, state_conv, norm1_g, w_in, attn_sink, conv_dw_w, conv_dw_b,
           conv_ln_g, conv_ln_b, conv_pw_w, conv_pw_b, w_out, norm2_g, router_group_w, router_group_b,
           router_expert_w, router_expert_b, expert_w_gate, expert_w_up, expert_w_down, final_norm_g):
    depth = w_in.shape[0]
    assert depth == 1, "single-layer step"
    bp, sp, _ = x_prompt.shape
    assert bp == 1, "one prompt sequence"
    n_seq, t_new, _ = x_sample.shape
    n_p, n_s = bp * sp, n_seq * t_new
    n_tok = n_p + n_s
    l = 0

    row = lambda a: a.reshape(1, -1)
    w_in_bf = w_in[l].astype(BF16)
    w_out_bf = w_out[l].astype(BF16)
    pw_bf = conv_pw_w[l].astype(BF16)
    rw = jnp.concatenate([router_group_w[l], router_expert_w[l]], axis=1)
    rw_bf = jnp.pad(rw, ((0, 0), (0, LANES - rw.shape[1]))).astype(BF16)
    rb = jnp.pad(jnp.concatenate([router_group_b[l], router_expert_b[l]]), (0, LANES - rw.shape[1])).reshape(1, LANES)
    g1, g2, gf = row(norm1_g[l]), row(norm2_g[l]), row(final_norm_g)
    sink = attn_sink[l]
    conv_w = (conv_dw_w[l], row(conv_dw_b[l]), row(conv_ln_g[l]), row(conv_ln_b[l]), pw_bf, row(conv_pw_b[l]))

    xp = x_prompt.reshape(n_p, D_MODEL)
    xs = x_sample.reshape(n_s, D_MODEL)
    cache_axes, cache_axes_back = (0, 1, 3, 4, 2), (0, 1, 4, 2, 3)
    ck = jnp.transpose(cache_win_k, cache_axes)
    cv = jnp.transpose(cache_win_v, cache_axes)
    cs = jnp.transpose(state_conv, (0, 2, 1, 3))

    qp, kp, vp, up = _in_proj(xp, g1, w_in_bf)
    qs, ks, vs, us = _in_proj(xs, g1, w_in_bf)
    mix_p = _mixer_prompt(sink, qp, kp, vp, up, *conv_w)
    mix_s, wk_s, wv_s, wc_s = _mixer_sample(sink, qs, ks, vs, us, ck, cv, cs, *conv_w)

    h_all, meta_all, rec_all, cnt = _out_router(mix_p, xp, mix_s, xs, w_out_bf, g2, rw_bf, rb)

    off_rows, ts, nv = _routing_tables(cnt[:, 0], n_tok)
    pos_rows = _positions(rec_all, off_rows)
    pos = jnp.concatenate([pos_rows[0], pos_rows[1]])
    hs = _dispatch(pos, nv, h_all)
    ys = _moe(ts, hs, g2, expert_w_gate[l], expert_w_up[l], expert_w_down[l])
    y_p, y_s = _combine(pos, h_all, meta_all, ys, gf, n_p)

    kv_shape = (depth, bp, WINDOW, N_KV_HEADS, HEAD_DIM)
    return (y_p.reshape(bp, sp, D_MODEL), y_s.reshape(n_seq, t_new, D_MODEL),
            kp[n_p - WINDOW:].reshape(kv_shape), vp[n_p - WINDOW:].reshape(kv_shape),
            up[n_p - (CONV_W - 1):].reshape(depth, bp, CONV_W - 1, D_CONV),
            jnp.transpose(wk_s, cache_axes_back), jnp.transpose(wv_s, cache_axes_back),
            jnp.transpose(wc_s, (0, 2, 1, 3)))
```
